```python
import math
import jax, jax.numpy as jnp
from jax import lax
import numpy as np


D_MODEL = 1024
BATCH = 2
SEQ = 16384
DEPTH = 2

CHUNK = 64
Q_BLOCK = 128
F32 = jnp.float32
RMS_EPS = 1e-6
ROPE_BASE = 10000.0

DA_HEADS = 4
DA_QK_DIM = 64
DA_V_DIM = 128
DA_WIDTH = DA_HEADS * DA_V_DIM
RET_HEADS = 4
RET_QK_DIM = 64
RET_V_DIM = 128
RET_WIDTH = RET_HEADS * RET_V_DIM
S5_WIDTH = 256
S5_GROUP = 16
S5_GROUPS = S5_WIDTH // S5_GROUP
S5_STATE = 64
S5_DT_MIN = 0.001
S5_DT_MAX = 0.1
GLA_HEADS = 6
GLA_QK_DIM = 64
GLA_V_DIM = 128
GLA_WIDTH = GLA_HEADS * GLA_V_DIM
GLA_GATE_RANK = 16
GLA_TAU = 16.0
MLP_HIDDEN = 4 * D_MODEL

AB_SIZES = (DA_HEADS * 2 * DA_QK_DIM, DA_HEADS * 2 * DA_QK_DIM, DA_WIDTH,
            RET_HEADS * RET_QK_DIM, RET_HEADS * RET_QK_DIM, RET_WIDTH, RET_WIDTH)
AB_IN = sum(AB_SIZES)
CD_SIZES = (S5_WIDTH, GLA_HEADS * GLA_QK_DIM, GLA_HEADS * GLA_QK_DIM, GLA_WIDTH, GLA_WIDTH, GLA_GATE_RANK)
CD_IN = sum(CD_SIZES)

kernel_name = 'hybrid_diffattn_retnet_s5_gla_block'


def _rmsnorm(x, gain):
    xf = x.astype(F32)
    y = xf * lax.rsqrt(jnp.mean(xf * xf, axis=-1, keepdims=True) + RMS_EPS)
    return (y * gain.astype(F32)).astype(x.dtype)


def _split_cols(h, sizes):
    out = []
    start = 0
    for size in sizes:
        out.append(h[..., start:start + size])
        start += size
    return out


def _rotary(x):
    seq, d = x.shape[1], x.shape[-1]
    half = d // 2
    inv_freq = 1.0 / (ROPE_BASE ** jnp.linspace(0.0, 1.0, half, dtype=F32))
    ang = jnp.arange(seq, dtype=F32)[:, None] * inv_freq[None, :]
    cos = jnp.cos(ang)[None, :, None, :]
    sin = jnp.sin(ang)[None, :, None, :]
    x1, x2 = x[..., :half], x[..., half:]
    return jnp.concatenate([x1 * cos - x2 * sin, x1 * sin + x2 * cos], axis=-1)


def _chunk_states(decay, kv):
    def step(state, inp):
        d, u = inp
        return d * state + u, state
    init = jnp.zeros(kv.shape[:1] + kv.shape[2:], F32)
    _, prev = lax.scan(step, init, (jnp.moveaxis(decay, 1, 0), jnp.moveaxis(kv, 1, 0)))
    return jnp.moveaxis(prev, 0, 1)


def _diff_attention(q, k, v, lam):
    bsz, seq, heads, _, dq = q.shape
    n_blocks = seq // Q_BLOCK
    scale = dq ** -0.5
    key_chunk = jnp.arange(seq) // CHUNK
    q_blocks = jnp.moveaxis(q.reshape(bsz, n_blocks, Q_BLOCK, heads, 2, dq), 1, 0)

    def block(args):
        q_blk, blk = args
        q_chunk = (blk * Q_BLOCK + jnp.arange(Q_BLOCK)) // CHUNK
        mask = key_chunk[None, :] <= q_chunk[:, None]
        s = jnp.einsum('bqhmd,bkhmd->bhmqk', q_blk, k).astype(F32) * scale
        p = jax.nn.softmax(jnp.where(mask, s, -jnp.inf), axis=-1)
        w = p[:, :, 0] - lam * p[:, :, 1]
        return jnp.einsum('bhqk,bkhe->bqhe', w.astype(v.dtype), v)

    out = lax.map(block, (q_blocks, jnp.arange(n_blocks)))
    return jnp.moveaxis(out, 0, 1).reshape(bsz, seq, heads, v.shape[-1])


def _retention(q, k, v):
    bsz, seq, heads, dk = q.shape
    dv = v.shape[-1]
    nc = seq // CHUNK
    shp = (bsz, nc, CHUNK, heads)
    q = _rotary(q.astype(F32)).reshape(shp + (dk,))
    k = (_rotary(k.astype(F32)) * dk ** -0.5).reshape(shp + (dk,))
    v = v.astype(F32).reshape(shp + (dv,))
    log_gamma = jnp.log(1.0 - 2.0 ** (-5.0 - jnp.arange(heads, dtype=F32)))
    idx = jnp.arange(CHUNK, dtype=F32)
    intra_decay = jnp.exp(log_gamma[:, None, None] * jnp.abs(idx[:, None] - idx[None, :]))
    scores = jnp.einsum('bnihd,bnjhd->bnhij', q, k) * intra_decay
    o_intra = jnp.einsum('bnhij,bnjhe->bnihe', scores, v)
    k_w = jnp.exp((CHUNK - 1.0 - idx)[:, None] * log_gamma[None, :])[..., None]
    kv = jnp.einsum('bnjhd,bnjhe->bnhde', k * k_w, v)
    decay = jnp.broadcast_to(jnp.exp(CHUNK * log_gamma)[:, None, None], (bsz, nc, heads, 1, 1))
    prev = _chunk_states(decay, kv)
    q_w = jnp.exp((idx + 1.0)[:, None] * log_gamma[None, :])[..., None]
    o_cross = jnp.einsum('bnihd,bnhde->bnihe', q * q_w, prev)
    return (o_intra + o_cross).reshape(bsz, seq, heads, dv)


def _gla(q, k, v, log_a):
    bsz, seq, heads, dk = q.shape
    dv = v.shape[-1]
    nc = seq // CHUNK
    shp = (bsz, nc, CHUNK, heads)
    q = (q.astype(F32) * dk ** -0.5).reshape(shp + (dk,))
    k = k.astype(F32).reshape(shp + (dk,))
    v = v.astype(F32).reshape(shp + (dv,))
    b = jnp.cumsum(log_a.astype(F32).reshape(shp + (dk,)), axis=2)
    b_last = b[:, :, -1:]
    e_pos, e_neg = jnp.exp(b), jnp.exp(-b)
    fwd = jnp.einsum('bnihd,bnjhd->bnhij', q * e_pos, k * e_neg)
    bwd = jnp.einsum('bnihd,bnjhd->bnhij', q * e_neg, k * e_pos)
    idx = jnp.arange(CHUNK)
    scores = jnp.where(idx[:, None] >= idx[None, :], fwd, bwd)
    o_intra = jnp.einsum('bnhij,bnjhe->bnihe', scores, v)
    kv = jnp.einsum('bnjhd,bnjhe->bnhde', k * jnp.exp(b_last - b), v)
    prev = _chunk_states(jnp.exp(b_last[:, :, 0])[..., None], kv)
    o_cross = jnp.einsum('bnihd,bnhde->bnihe', q * e_pos, prev)
    return (o_intra + o_cross).reshape(bsz, seq, heads, dv)


def _ssm_combine(left, right):
    a_l, b_l = left
    a_r, b_r = right
    return a_r * a_l, a_r * b_l + b_r


def _s5(u, a_re, a_im, log_step, b_re, b_im, c_re, c_im, d_skip, w_glu):
    bsz, seq, _ = u.shape
    uf = u.astype(F32).reshape(bsz, seq, S5_GROUPS, S5_GROUP)
    lam = lax.complex(a_re.astype(F32), a_im.astype(F32))
    delta = jnp.exp(log_step.astype(F32))[:, None]
    lam_bar = jnp.exp(lam * delta)
    b_bar = ((lam_bar - 1.0) / lam)[:, :, None] * lax.complex(b_re.astype(F32), b_im.astype(F32))
    bu = lax.complex(jnp.einsum('bsgc,gpc->bsgp', uf, jnp.real(b_bar)),
                     jnp.einsum('bsgc,gpc->bsgp', uf, jnp.imag(b_bar)))
    a = jnp.broadcast_to(lam_bar, (1, seq) + lam_bar.shape)
    _, states = lax.associative_scan(_ssm_combine, (a, bu), axis=1)
    y = (jnp.einsum('gcp,bsgp->bsgc', c_re.astype(F32), jnp.real(states))
         - jnp.einsum('gcp,bsgp->bsgc', c_im.astype(F32), jnp.imag(states)))
    y = (y + d_skip.astype(F32).reshape(S5_GROUPS, S5_GROUP) * uf).reshape(bsz, seq, S5_WIDTH)
    z = jax.nn.gelu(y)
    out = z * jax.nn.sigmoid(z @ w_glu.astype(F32))
    return out.astype(u.dtype)


def setup_inputs(seed: int = 0) -> dict:
    key = jax.random.key(seed)
    ks = jax.random.split(key, 32)
    n_even = (DEPTH + 1) // 2
    n_odd = DEPTH // 2

    def normal(i, shape, scale):
        return scale * jax.random.normal(ks[i], shape, F32)

    def gain(i, shape):
        return 1.0 + 0.02 * jax.random.normal(ks[i], shape, F32)

    x = normal(0, (BATCH, SEQ, D_MODEL), 1.0)
    norm_mix_g = gain(1, (DEPTH, D_MODEL))
    norm_mlp_g = gain(2, (DEPTH, D_MODEL))
    w_up = normal(3, (DEPTH, D_MODEL, MLP_HIDDEN), D_MODEL ** -0.5)
    w_down = normal(4, (DEPTH, MLP_HIDDEN, D_MODEL), MLP_HIDDEN ** -0.5)
    ab_w_in = normal(5, (n_even, D_MODEL, AB_IN), D_MODEL ** -0.5)
    ab_w_out = normal(6, (n_even, DA_WIDTH + RET_WIDTH, D_MODEL), (DA_WIDTH + RET_WIDTH) ** -0.5)
    da_q_norm = gain(7, (n_even, DA_QK_DIM))
    da_k_norm = gain(8, (n_even, DA_QK_DIM))
    da_lam_q1 = normal(9, (n_even, DA_QK_DIM), 0.1)
    da_lam_k1 = normal(10, (n_even, DA_QK_DIM), 0.1)
    da_lam_q2 = normal(11, (n_even, DA_QK_DIM), 0.1)
    da_lam_k2 = normal(12, (n_even, DA_QK_DIM), 0.1)
    da_out_norm = gain(13, (n_even, DA_V_DIM))
    ret_out_norm = gain(14, (n_even, RET_V_DIM))
    cd_w_in = normal(15, (n_odd, D_MODEL, CD_IN), D_MODEL ** -0.5)
    cd_w_out = normal(16, (n_odd, S5_WIDTH + GLA_WIDTH, D_MODEL), (S5_WIDTH + GLA_WIDTH) ** -0.5)
    s5_a_re = -0.5 + normal(17, (n_odd, S5_GROUPS, S5_STATE), 0.01)
    s5_a_im = math.pi * jnp.arange(S5_STATE, dtype=F32) + normal(18, (n_odd, S5_GROUPS, S5_STATE), 0.01)
    s5_log_step = jax.random.uniform(ks[19], (n_odd, S5_GROUPS), F32, math.log(S5_DT_MIN), math.log(S5_DT_MAX))
    s5_b_re = normal(20, (n_odd, S5_GROUPS, S5_STATE, S5_GROUP), (2.0 * S5_GROUP) ** -0.5)
    s5_b_im = normal(21, (n_odd, S5_GROUPS, S5_STATE, S5_GROUP), (2.0 * S5_GROUP) ** -0.5)
    s5_c_re = normal(22, (n_odd, S5_GROUPS, S5_GROUP, S5_STATE), (2.0 * S5_STATE) ** -0.5)
    s5_c_im = normal(23, (n_odd, S5_GROUPS, S5_GROUP, S5_STATE), (2.0 * S5_STATE) ** -0.5)
    s5_d = normal(24, (n_odd, S5_WIDTH), 1.0)
    s5_w_glu = normal(25, (n_odd, S5_WIDTH, S5_WIDTH), S5_WIDTH ** -0.5)
    gla_w_a2 = normal(26, (n_odd, GLA_GATE_RANK, GLA_HEADS * GLA_QK_DIM), GLA_GATE_RANK ** -0.5)
    gla_b_a2 = normal(27, (n_odd, GLA_HEADS * GLA_QK_DIM), 0.1)
    gla_out_norm = gain(28, (n_odd, GLA_V_DIM))
    return {'x': x, 'norm_mix_g': norm_mix_g, 'norm_mlp_g': norm_mlp_g, 'w_up': w_up, 'w_down': w_down,
            'ab_w_in': ab_w_in, 'ab_w_out': ab_w_out, 'da_q_norm': da_q_norm, 'da_k_norm': da_k_norm,
            'da_lam_q1': da_lam_q1, 'da_lam_k1': da_lam_k1, 'da_lam_q2': da_lam_q2, 'da_lam_k2': da_lam_k2,
            'da_out_norm': da_out_norm, 'ret_out_norm': ret_out_norm, 'cd_w_in': cd_w_in, 'cd_w_out': cd_w_out,
            's5_a_re': s5_a_re, 's5_a_im': s5_a_im, 's5_log_step': s5_log_step, 's5_b_re': s5_b_re,
            's5_b_im': s5_b_im, 's5_c_re': s5_c_re, 's5_c_im': s5_c_im, 's5_d': s5_d, 's5_w_glu': s5_w_glu,
            'gla_w_a2': gla_w_a2, 'gla_b_a2': gla_b_a2, 'gla_out_norm': gla_out_norm}


def reference(x, norm_mix_g, norm_mlp_g, w_up, w_down, ab_w_in, ab_w_out, da_q_norm, da_k_norm,
              da_lam_q1, da_lam_k1, da_lam_q2, da_lam_k2, da_out_norm, ret_out_norm, cd_w_in, cd_w_out,
              s5_a_re, s5_a_im, s5_log_step, s5_b_re, s5_b_im, s5_c_re, s5_c_im, s5_d, s5_w_glu,
              gla_w_a2, gla_b_a2, gla_out_norm):
    bsz, seq, _ = x.shape
    for layer in range(DEPTH):
        h = _rmsnorm(x, norm_mix_g[layer])
        if layer % 2 == 0:
            j = layer // 2
            q_a, k_a, v_a, q_r, k_r, v_r, g_r = _split_cols(h @ ab_w_in[j], AB_SIZES)
            lam_init = 0.8 - 0.6 * math.exp(-0.3 * layer)
            lam = (jnp.exp(jnp.sum(da_lam_q1[j].astype(F32) * da_lam_k1[j].astype(F32)))
                   - jnp.exp(jnp.sum(da_lam_q2[j].astype(F32) * da_lam_k2[j].astype(F32))) + lam_init)
            qa = _rmsnorm(q_a.reshape(bsz, seq, DA_HEADS, 2, DA_QK_DIM), da_q_norm[j])
            ka = _rmsnorm(k_a.reshape(bsz, seq, DA_HEADS, 2, DA_QK_DIM), da_k_norm[j])
            o_a = _diff_attention(qa, ka, v_a.reshape(bsz, seq, DA_HEADS, DA_V_DIM), lam)
            o_a = (_rmsnorm(o_a, da_out_norm[j]) * (1.0 - lam_init)).reshape(bsz, seq, DA_WIDTH)
            o_r = _retention(q_r.reshape(bsz, seq, RET_HEADS, RET_QK_DIM),
                             k_r.reshape(bsz, seq, RET_HEADS, RET_QK_DIM),
                             v_r.reshape(bsz, seq, RET_HEADS, RET_V_DIM)).astype(x.dtype)
            o_r = _rmsnorm(o_r, ret_out_norm[j]).reshape(bsz, seq, RET_WIDTH) * jax.nn.silu(g_r)
            mixed = jnp.concatenate([o_a, o_r], axis=-1) @ ab_w_out[j]
        else:
            j = layer // 2
            u, q_g, k_g, v_g, r_g, a_lr = _split_cols(h @ cd_w_in[j], CD_SIZES)
            o_c = _s5(u, s5_a_re[j], s5_a_im[j], s5_log_step[j], s5_b_re[j], s5_b_im[j],
                      s5_c_re[j], s5_c_im[j], s5_d[j], s5_w_glu[j])
            log_a = jax.nn.log_sigmoid((a_lr @ gla_w_a2[j] + gla_b_a2[j]).astype(F32)) / GLA_TAU
            o_d = _gla(q_g.reshape(bsz, seq, GLA_HEADS, GLA_QK_DIM),
                       k_g.reshape(bsz, seq, GLA_HEADS, GLA_QK_DIM),
                       v_g.reshape(bsz, seq, GLA_HEADS, GLA_V_DIM),
                       log_a.reshape(bsz, seq, GLA_HEADS, GLA_QK_DIM)).astype(x.dtype)
            o_d = _rmsnorm(o_d, gla_out_norm[j]).reshape(bsz, seq, GLA_WIDTH) * jax.nn.silu(r_g)
            mixed = jnp.concatenate([o_c, o_d], axis=-1) @ cd_w_out[j]
        x = x + mixed
        h = _rmsnorm(x, norm_mlp_g[layer])
        x = x + jnp.square(jax.nn.relu(h @ w_up[layer])) @ w_down[layer]
    return x
```

```python
import functools
import math

import jax
import jax.numpy as jnp
from jax import lax
from jax.experimental import pallas as pl
from jax.experimental.pallas import tpu as pltpu

F32 = jnp.float32
BF16 = jnp.bfloat16

D_MODEL = 1024
CHUNK = 64
RMS_EPS = 1e-6
ROPE_BASE = 10000.0
DA_HEADS = 4
DA_QK_DIM = 64
DA_V_DIM = 128
DA_WIDTH = DA_HEADS * DA_V_DIM
RET_HEADS = 4
RET_QK_DIM = 64
RET_V_DIM = 128
RET_WIDTH = RET_HEADS * RET_V_DIM
S5_WIDTH = 256
S5_GROUP = 16
S5_GROUPS = S5_WIDTH // S5_GROUP
S5_STATE = 64
GLA_HEADS = 6
GLA_QK_DIM = 64
GLA_V_DIM = 128
GLA_WIDTH = GLA_HEADS * GLA_V_DIM
GLA_QK_WIDTH = GLA_HEADS * GLA_QK_DIM
GLA_GATE_RANK = 16
GLA_TAU = 16.0
MLP_HIDDEN = 4 * D_MODEL
AB_IN = 3072
CD_IN = 2576

LANES_V7X = 128
VMEM_LIMIT_BYTES_V7X = 56 * 1024 * 1024

CD_IN_PADDED = 2688
NEG_BIG = -1e30

TOKEN_TILE = 512
ATTN_BLOCK = 512
RET_BLOCK = 512
GLA_BLOCK = 256
S5_CHUNK = 64
MLP_HIDDEN_TILE = 1024


def _params(*semantics):
    return pltpu.CompilerParams(dimension_semantics=semantics, vmem_limit_bytes=VMEM_LIMIT_BYTES_V7X)


def _const_spec(shape):
    zeros = (0,) * len(shape)
    return pl.BlockSpec(shape, lambda *_: zeros, pipeline_mode=pl.Buffered(1))


def _rms(xf, gain):
    return xf * lax.rsqrt(jnp.mean(xf * xf, axis=-1, keepdims=True) + RMS_EPS) * gain


def _dot(a, b):
    return jnp.dot(a, b, preferred_element_type=F32)


def _dot_nt(a, b):
    return lax.dot_general(a, b, (((1,), (1,)), ((), ())), preferred_element_type=F32)


def _dot_tn(a, b):
    return lax.dot_general(a, b, (((0,), (0,)), ((), ())), preferred_element_type=F32)


def _swap_halves(x, group):
    n = x.shape[-1]
    half = group // 2
    lane = lax.broadcasted_iota(jnp.int32, x.shape, x.ndim - 1)
    from_right = pltpu.roll(x, n - half, axis=x.ndim - 1)
    from_left = pltpu.roll(x, half, axis=x.ndim - 1)
    return jnp.where((lane % group) < half, from_right, from_left)


def _ab_in_kernel(x_ref, g_ref, w_ref, gq_ref, gk_ref, gsum_ref, cos_ref, sin_ref,
                  qa_ref, ka_ref, va_ref, qr_ref, kr_ref, vr_ref, gr_ref):
    h = _rms(x_ref[...], g_ref[...]).astype(BF16)

    def proj(lo, hi):
        return _dot(h, w_ref[:, lo:hi])

    def qk_norm(y, gain):
        ms = _dot((y * y).astype(BF16), gsum_ref[...])
        return (y * lax.rsqrt(ms + RMS_EPS) * gain).astype(BF16)

    def rotary(y):
        return y * cos_ref[...] + _swap_halves(y, RET_QK_DIM) * sin_ref[...]

    qa_ref[...] = qk_norm(proj(0, 512), gq_ref[...])
    ka_ref[...] = qk_norm(proj(512, 1024), gk_ref[...])
    va_ref[...] = proj(1024, 1536).astype(BF16)
    qr_ref[...] = rotary(proj(1536, 1792))
    kr_ref[...] = rotary(proj(1792, 2048)) * (RET_QK_DIM ** -0.5)
    vr_ref[...] = proj(2048, 2560).astype(BF16)
    gr_ref[...] = proj(2560, 3072)


def _ab_in(x2, g, w, gq, gk, gsum, cos_t, sin_t):
    t = x2.shape[0]
    tm = TOKEN_TILE
    row = lambda width: pl.BlockSpec((tm, width), lambda i: (i, 0))
    out_shapes = (
        jax.ShapeDtypeStruct((t, 512), BF16), jax.ShapeDtypeStruct((t, 512), BF16),
        jax.ShapeDtypeStruct((t, 512), BF16), jax.ShapeDtypeStruct((t, 256), F32),
        jax.ShapeDtypeStruct((t, 256), F32), jax.ShapeDtypeStruct((t, 512), BF16),
        jax.ShapeDtypeStruct((t, 512), F32))
    return pl.pallas_call(
        _ab_in_kernel,
        grid=(t // tm,),
        in_specs=[row(D_MODEL), _const_spec((1, D_MODEL)), _const_spec((D_MODEL, AB_IN)),
                  _const_spec((1, 512)), _const_spec((1, 512)), _const_spec((512, 512)),
                  row(256), row(256)],
        out_specs=(row(512), row(512), row(512), row(256), row(256), row(512), row(512)),
        out_shape=out_shapes,
        compiler_params=_params("parallel"),
        name="ab_in_proj",
    )(x2, g, w, gq, gk, gsum, cos_t, sin_t)


def _diff_attn_kernel(lq1_ref, lk1_ref, lq2_ref, lk2_ref, gout_ref, q_ref, k_ref, v_ref, o_ref, *, blk, lam_init):
    i = pl.program_id(2)
    q = q_ref[0]
    lane = lax.broadcasted_iota(jnp.int32, q.shape, 1)
    zero = jnp.zeros_like(q)
    qz = jnp.concatenate([jnp.where(lane < DA_QK_DIM, q, zero), jnp.where(lane >= DA_QK_DIM, q, zero)], axis=0)

    def update(carry, s, vj):
        m, l, acc = carry
        m_new = jnp.maximum(m, jnp.max(s, axis=-1, keepdims=True))
        alpha = jnp.exp(m - m_new)
        p = jnp.exp(s - m_new)
        l = alpha * l + jnp.sum(p, axis=-1, keepdims=True)
        acc = alpha * acc + _dot(p.astype(BF16), vj)
        return m_new, l, acc

    def body(j, carry):
        start = pl.multiple_of(j * blk, blk)
        kj = k_ref[0, pl.ds(start, blk), :]
        vj = v_ref[0, pl.ds(start, blk), :]
        return update(carry, _dot_nt(qz, kj), vj)

    init = (jnp.full((2 * blk, 1), NEG_BIG, F32), jnp.zeros((2 * blk, 1), F32),
            jnp.zeros((2 * blk, DA_V_DIM), F32))
    carry = lax.fori_loop(0, i, body, init)

    start = pl.multiple_of(i * blk, blk)
    kj = k_ref[0, pl.ds(start, blk), :]
    vj = v_ref[0, pl.ds(start, blk), :]
    s = _dot_nt(qz, kj)
    row = lax.broadcasted_iota(jnp.int32, s.shape, 0)
    col = lax.broadcasted_iota(jnp.int32, s.shape, 1)
    s = jnp.where(col // CHUNK <= (row % blk) // CHUNK, s, NEG_BIG)
    _, l, acc = update(carry, s, vj)

    lam = (jnp.exp(jnp.sum(lq1_ref[...] * lk1_ref[...], axis=-1, keepdims=True))
           - jnp.exp(jnp.sum(lq2_ref[...] * lk2_ref[...], axis=-1, keepdims=True)) + lam_init)
    o = acc / l
    o = o[:blk] - lam * o[blk:]
    o_ref[0] = (_rms(o, gout_ref[...]) * (1.0 - lam_init)).astype(BF16)


def _diff_attn(qa, ka, va, lq1, lk1, lq2, lk2, gout, lam_init):
    b, s, _ = qa.shape
    blk = min(ATTN_BLOCK, s)
    vec = _const_spec((1, DA_QK_DIM))
    q_spec = pl.BlockSpec((1, blk, 128), lambda bi, h, i: (bi, i, h))
    kv_spec = pl.BlockSpec((1, s, 128), lambda bi, h, i: (bi, 0, h))
    return pl.pallas_call(
        functools.partial(_diff_attn_kernel, blk=blk, lam_init=lam_init),
        grid=(b, DA_HEADS, s // blk),
        in_specs=[vec, vec, vec, vec, _const_spec((1, DA_V_DIM)), q_spec, kv_spec, kv_spec],
        out_specs=q_spec,
        out_shape=jax.ShapeDtypeStruct((b, s, DA_WIDTH), BF16),
        compiler_params=_params("parallel", "parallel", "arbitrary"),
        name="diff_attention",
    )(lq1, lk1, lq2, lk2, gout, qa, ka, va)


def _retention_kernel(dec_ref, qdec_ref, kdec_ref, sdec_ref, smask_ref, gain_ref,
                      q_ref, k_ref, v_ref, g_ref, o_ref, state_ref):
    @pl.when(pl.program_id(1) == 0)
    def _():
        state_ref[...] = jnp.zeros_like(state_ref)

    q = q_ref[0]
    k = k_ref[0]
    v = v_ref[0]
    kb = k.astype(BF16)
    lane = lax.broadcasted_iota(jnp.int32, q.shape, 1)
    state = state_ref[...]
    o_cross = _dot((q * qdec_ref[...]).astype(BF16), state.astype(BF16))
    for h in range(RET_HEADS):
        qh = jnp.where(lane // RET_QK_DIM == h, q, 0.0).astype(BF16)
        w = (_dot_nt(qh, kb) * dec_ref[h]).astype(BF16)
        vs = slice(h * RET_V_DIM, (h + 1) * RET_V_DIM)
        o = _dot(w, v[:, vs]) + o_cross[:, vs]
        o = _rms(o, gain_ref[...])
        g = g_ref[0, :, vs]
        o_ref[0, :, vs] = (o * (g * jax.nn.sigmoid(g))).astype(BF16)
    kv = _dot_tn((k * kdec_ref[...]).astype(BF16), v)
    state_ref[...] = sdec_ref[...] * state + kv * smask_ref[...]


def _retention_tables(blk):
    heads = jnp.arange(RET_HEADS, dtype=F32)
    log_gamma = jnp.log(1.0 - 2.0 ** (-5.0 - heads))
    idx = jnp.arange(blk)
    diff = (idx[:, None] - idx[None, :]).astype(F32)
    same_chunk = (idx[:, None] // CHUNK) == (idx[None, :] // CHUNK)
    visible = (idx[None, :] <= idx[:, None]) | same_chunk
    dec = jnp.where(visible[None], jnp.exp(log_gamma[:, None, None] * jnp.abs(diff)[None]), 0.0)
    per_lane = jnp.repeat(log_gamma, RET_QK_DIM)
    pos = jnp.arange(blk, dtype=F32)
    qdec = jnp.exp((pos[:, None] + 1.0) * per_lane[None, :])
    kdec = jnp.exp((blk - 1.0 - pos)[:, None] * per_lane[None, :])
    sdec = jnp.exp(blk * per_lane)[:, None]
    smask = (jnp.arange(RET_HEADS * RET_QK_DIM)[:, None] // RET_QK_DIM
             == jnp.arange(RET_WIDTH)[None, :] // RET_V_DIM).astype(F32)
    return dec, qdec, kdec, sdec, smask


def _retention(qr, kr, vr, gr, gain):
    b, s, _ = qr.shape
    blk = min(RET_BLOCK, s)
    dec, qdec, kdec, sdec, smask = _retention_tables(blk)
    tok = lambda width: pl.BlockSpec((1, blk, width), lambda bi, i: (bi, i, 0))
    return pl.pallas_call(
        _retention_kernel,
        grid=(b, s // blk),
        in_specs=[_const_spec(dec.shape), _const_spec(qdec.shape), _const_spec(kdec.shape),
                  _const_spec(sdec.shape), _const_spec(smask.shape), _const_spec((1, RET_V_DIM)),
                  tok(256), tok(256), tok(512), tok(512)],
        out_specs=tok(RET_WIDTH),
        out_shape=jax.ShapeDtypeStruct((b, s, RET_WIDTH), BF16),
        scratch_shapes=[pltpu.VMEM((RET_HEADS * RET_QK_DIM, RET_WIDTH), F32)],
        compiler_params=_params("parallel", "arbitrary"),
        name="retention",
    )(dec, qdec, kdec, sdec, smask, gain, qr, kr, vr, gr)


def _mlp_tail(x1, g_ref, wup_ref, wdn_ref, o_ref):
    h = _rms(x1, g_ref[...]).astype(BF16)
    acc = x1
    for c in range(MLP_HIDDEN // MLP_HIDDEN_TILE):
        cs = slice(c * MLP_HIDDEN_TILE, (c + 1) * MLP_HIDDEN_TILE)
        u = jnp.maximum(_dot(h, wup_ref[:, cs]), 0.0)
        acc = acc + _dot((u * u).astype(BF16), wdn_ref[cs, :])
    o_ref[...] = acc


def _out_mlp_kernel(x_ref, a_ref, b_ref, wout_ref, g_ref, wup_ref, wdn_ref, o_ref):
    ka = a_ref.shape[1]
    mixed = _dot(a_ref[...], wout_ref[:ka, :]) + _dot(b_ref[...], wout_ref[ka:, :])
    _mlp_tail(x_ref[...] + mixed, g_ref, wup_ref, wdn_ref, o_ref)


def _s5_out_mlp_kernel(x_ref, y_ref, u_ref, d_ref, wglu_ref, b_ref, wout_ref, g_ref, wup_ref, wdn_ref, o_ref):
    y = y_ref[...] + d_ref[...] * u_ref[...]
    z = jax.nn.gelu(y)
    a = (z * jax.nn.sigmoid(_dot(z.astype(BF16), wglu_ref[...]))).astype(BF16)
    mixed = _dot(a, wout_ref[:S5_WIDTH, :]) + _dot(b_ref[...], wout_ref[S5_WIDTH:, :])
    _mlp_tail(x_ref[...] + mixed, g_ref, wup_ref, wdn_ref, o_ref)


def _mlp_specs():
    return [_const_spec((D_MODEL, D_MODEL)), _const_spec((1, D_MODEL)),
            _const_spec((D_MODEL, MLP_HIDDEN)), _const_spec((MLP_HIDDEN, D_MODEL))]


def _out_mlp(x2, a, bb, wout, g, wup, wdn):
    t = x2.shape[0]
    tm = TOKEN_TILE
    row = lambda width: pl.BlockSpec((tm, width), lambda i: (i, 0))
    return pl.pallas_call(
        _out_mlp_kernel,
        grid=(t // tm,),
        in_specs=[row(D_MODEL), row(a.shape[1]), row(bb.shape[1])] + _mlp_specs(),
        out_specs=row(D_MODEL),
        out_shape=jax.ShapeDtypeStruct((t, D_MODEL), F32),
        compiler_params=_params("parallel"),
        name="ab_out_mlp",
    )(x2, a, bb, wout, g, wup, wdn)


def _s5_out_mlp(x2, y, u, d, wglu, bb, wout, g, wup, wdn):
    t = x2.shape[0]
    tm = TOKEN_TILE
    row = lambda width: pl.BlockSpec((tm, width), lambda i: (i, 0))
    return pl.pallas_call(
        _s5_out_mlp_kernel,
        grid=(t // tm,),
        in_specs=[row(D_MODEL), row(S5_WIDTH), row(S5_WIDTH), _const_spec((1, S5_WIDTH)),
                  _const_spec((S5_WIDTH, S5_WIDTH)), row(GLA_WIDTH)] + _mlp_specs(),
        out_specs=row(D_MODEL),
        out_shape=jax.ShapeDtypeStruct((t, D_MODEL), F32),
        compiler_params=_params("parallel"),
        name="cd_out_mlp",
    )(x2, y, u, d, wglu, bb, wout, g, wup, wdn)


def _cd_in_kernel(x_ref, g_ref, w_ref, wa_ref, ba_ref, u_ref, q_ref, k_ref, v_ref, r_ref, la_ref):
    h = _rms(x_ref[...], g_ref[...]).astype(BF16)

    def proj(lo, hi):
        return _dot(h, w_ref[:, lo:hi])

    u_ref[...] = proj(0, 256)
    q_ref[...] = proj(256, 640) * (GLA_QK_DIM ** -0.5)
    k_ref[...] = proj(640, 1024)
    v_ref[...] = proj(1024, 1792).astype(BF16)
    r_ref[...] = proj(1792, 2560)
    a_lr = proj(2560, CD_IN_PADDED)
    pre = _dot(a_lr.astype(BF16), wa_ref[...]) + ba_ref[...]
    log_sig = jnp.minimum(pre, 0.0) - jnp.log1p(jnp.exp(-jnp.abs(pre)))
    la_ref[...] = log_sig / GLA_TAU


def _cd_in(x2, g, w, wa, ba):
    t = x2.shape[0]
    tm = TOKEN_TILE
    row = lambda width: pl.BlockSpec((tm, width), lambda i: (i, 0))
    out_shapes = (
        jax.ShapeDtypeStruct((t, S5_WIDTH), F32), jax.ShapeDtypeStruct((t, GLA_QK_WIDTH), F32),
        jax.ShapeDtypeStruct((t, GLA_QK_WIDTH), F32), jax.ShapeDtypeStruct((t, GLA_WIDTH), BF16),
        jax.ShapeDtypeStruct((t, GLA_WIDTH), F32), jax.ShapeDtypeStruct((t, GLA_QK_WIDTH), F32))
    return pl.pallas_call(
        _cd_in_kernel,
        grid=(t // tm,),
        in_specs=[row(D_MODEL), _const_spec((1, D_MODEL)), _const_spec((D_MODEL, CD_IN_PADDED)),
                  _const_spec((LANES_V7X, GLA_QK_WIDTH)), _const_spec((1, GLA_QK_WIDTH))],
        out_specs=(row(S5_WIDTH), row(GLA_QK_WIDTH), row(GLA_QK_WIDTH), row(GLA_WIDTH), row(GLA_WIDTH),
                   row(GLA_QK_WIDTH)),
        out_shape=out_shapes,
        compiler_params=_params("parallel"),
        name="cd_in_proj",
    )(x2, g, w, wa, ba)


def _split3(x):
    hi = x.astype(BF16)
    r = x - hi.astype(F32)
    mid = r.astype(BF16)
    lo = (r - mid.astype(F32)).astype(BF16)
    return hi, mid, lo


def _gla_kernel(tri_ref, gain_ref, q_ref, k_ref, v_ref, r_ref, la_ref, o_ref, state_ref, *, blk):
    @pl.when(pl.program_id(1) == 0)
    def _():
        state_ref[...] = jnp.zeros_like(state_ref)

    nc = blk // CHUNK
    q = q_ref[0]
    k = k_ref[0]
    tri = tri_ref[...]
    hi, mid, lo = _split3(la_ref[0])
    b = _dot(tri, hi) + _dot(tri, mid) + _dot(tri, lo)
    b3 = b.reshape(nc, CHUNK, GLA_QK_WIDTH)
    b_last = b3[:, CHUNK - 1:CHUNK, :]
    e_pos = jnp.exp(b)
    e_neg = jnp.exp(-b)
    k_tail = jnp.exp(b_last - b3).reshape(blk, GLA_QK_WIDTH)
    chunk_decay = jnp.exp(b_last)
    qp = (q * e_pos).astype(BF16)
    qn = (q * e_neg).astype(BF16)
    kp = (k * e_pos).astype(BF16)
    kn = (k * e_neg).astype(BF16)
    kw = (k * k_tail).astype(BF16)

    row = lax.broadcasted_iota(jnp.int32, (blk, blk), 0)
    col = lax.broadcasted_iota(jnp.int32, (blk, blk), 1)
    same_chunk = row // CHUNK == col // CHUNK
    causal = row >= col
    lane = lax.broadcasted_iota(jnp.int32, (blk, 2 * GLA_QK_DIM), 1)
    smask = (lax.broadcasted_iota(jnp.int32, (2 * GLA_V_DIM, 2 * GLA_QK_DIM), 0) // GLA_V_DIM
             == lax.broadcasted_iota(jnp.int32, (2 * GLA_V_DIM, 2 * GLA_QK_DIM), 1) // GLA_QK_DIM)

    for p in range(GLA_HEADS // 2):
        ks = slice(p * 2 * GLA_QK_DIM, (p + 1) * 2 * GLA_QK_DIM)
        vs = slice(p * 2 * GLA_V_DIM, (p + 1) * 2 * GLA_V_DIM)
        v = v_ref[0, :, vs]
        intra = []
        for half in range(2):
            sel = (lane // GLA_QK_DIM) == half
            fwd = _dot_nt(jnp.where(sel, qp[:, ks], 0).astype(BF16), kn[:, ks])
            bwd = _dot_nt(jnp.where(sel, qn[:, ks], 0).astype(BF16), kp[:, ks])
            sc = jnp.where(same_chunk, jnp.where(causal, fwd, bwd), 0.0).astype(BF16)
            intra.append(_dot(sc, v[:, half * GLA_V_DIM:(half + 1) * GLA_V_DIM]))
        o_intra = jnp.concatenate(intra, axis=1)

        state = state_ref[p]
        cross = []
        for c in range(nc):
            rs = slice(c * CHUNK, (c + 1) * CHUNK)
            cross.append(_dot_nt(qp[rs, ks], state.astype(BF16)))
            kv_t = _dot_tn(v[rs, :], kw[rs, ks])
            state = state * chunk_decay[c][:, ks] + jnp.where(smask, kv_t, 0.0)
        state_ref[p] = state
        o = o_intra + jnp.concatenate(cross, axis=0)
        for half in range(2):
            hs = slice(half * GLA_V_DIM, (half + 1) * GLA_V_DIM)
            os_ = slice(p * 2 * GLA_V_DIM + half * GLA_V_DIM, p * 2 * GLA_V_DIM + (half + 1) * GLA_V_DIM)
            g = r_ref[0, :, os_]
            o_ref[0, :, os_] = (_rms(o[:, hs], gain_ref[...]) * (g * jax.nn.sigmoid(g))).astype(BF16)


def _gla(qg, kg, vg, rg, la, gain):
    b, s, _ = qg.shape
    blk = min(GLA_BLOCK, s)
    idx = jnp.arange(blk)
    tri = (((idx[:, None] // CHUNK) == (idx[None, :] // CHUNK)) & (idx[None, :] <= idx[:, None])).astype(BF16)
    tok = lambda width: pl.BlockSpec((1, blk, width), lambda bi, i: (bi, i, 0))
    return pl.pallas_call(
        functools.partial(_gla_kernel, blk=blk),
        grid=(b, s // blk),
        in_specs=[_const_spec((blk, blk)), _const_spec((1, GLA_V_DIM)),
                  tok(GLA_QK_WIDTH), tok(GLA_QK_WIDTH), tok(GLA_WIDTH), tok(GLA_WIDTH), tok(GLA_QK_WIDTH)],
        out_specs=tok(GLA_WIDTH),
        out_shape=jax.ShapeDtypeStruct((b, s, GLA_WIDTH), BF16),
        scratch_shapes=[pltpu.VMEM((GLA_HEADS // 2, 2 * GLA_V_DIM, 2 * GLA_QK_DIM), F32)],
        compiler_params=_params("parallel", "arbitrary"),
        name="gla",
    )(tri, gain, qg, kg, vg, rg, la)


def _s5_kernel(u_ref, toep_ref, fre_ref, fim_ref, ere_ref, eim_ref, lre_ref, lim_ref, y_ref,
               vre_ref, vim_ref, hre_ref, him_ref, *, batch, chunks):
    u = u_ref[0]
    vre_ref[...] = _dot(u, fre_ref[0])
    vim_ref[...] = _dot(u, fim_ref[0])
    lre = lre_ref[0]
    lim = lim_ref[0]

    def step(n, carry):
        new = []
        for bi in range(batch):
            hr, hi = carry[2 * bi], carry[2 * bi + 1]
            r = bi * chunks + n
            hre_ref[pl.ds(r, 1), :] = hr
            him_ref[pl.ds(r, 1), :] = hi
            vr = vre_ref[pl.ds(r, 1), :]
            vi = vim_ref[pl.ds(r, 1), :]
            new += [lre * hr - lim * hi + vr, lre * hi + lim * hr + vi]
        return tuple(new)

    zero = jnp.zeros((1, LANES_V7X), F32)
    lax.fori_loop(0, chunks, step, (zero,) * (2 * batch))
    y_ref[0] = (_dot(u, toep_ref[0]) + _dot(hre_ref[...].astype(BF16), ere_ref[0])
                + _dot(him_ref[...].astype(BF16), eim_ref[0]))


def _s5_tables(a_re, a_im, log_step, b_re, b_im, c_re, c_im, length):
    hp = lax.Precision.HIGHEST
    g, p, c = S5_GROUPS, S5_STATE, S5_GROUP
    lam = lax.complex(a_re, a_im)
    delta = jnp.exp(log_step)[:, None]
    lam_bar = jnp.exp(lam * delta)
    b_bar = ((lam_bar - 1.0) / lam)[:, :, None] * lax.complex(b_re, b_im)
    cc = lax.complex(c_re, c_im)
    tau = jnp.arange(length + 1, dtype=F32)
    pw = jnp.exp((lam * delta)[:, None, :] * tau[None, :, None])
    cp = cc[:, None, :, :] * pw[:, :length, None, :]
    kern = (jnp.einsum('gtcp,gpd->gtcd', jnp.real(cp), jnp.real(b_bar), precision=hp)
            - jnp.einsum('gtcp,gpd->gtcd', jnp.imag(cp), jnp.imag(b_bar), precision=hp))
    s_idx = jnp.arange(length)
    lag = s_idx[None, :] - s_idx[:, None]
    toep = jnp.where((lag >= 0)[None, :, :, None, None], kern[:, jnp.clip(lag, 0, length - 1)], 0.0)
    toep = toep.transpose(0, 1, 4, 2, 3).reshape(g, length * c, length * c)
    f = pw[:, length - 1 - s_idx, :][:, :, None, :] * b_bar.transpose(0, 2, 1)[:, None, :, :]
    f = f.reshape(g, length * c, p)
    pad = lambda z: jnp.pad(z, ((0, 0), (0, 0), (0, LANES_V7X - p)))
    e = cc[:, None, :, :] * pw[:, 1:, None, :]
    e = e.transpose(0, 3, 1, 2).reshape(g, p, length * c)
    pad_rows = lambda z: jnp.pad(z, ((0, 0), (0, LANES_V7X - p), (0, 0)))
    lam_l = pw[:, length, :][:, None, :]
    return (toep.astype(BF16), pad(jnp.real(f)).astype(BF16), pad(jnp.imag(f)).astype(BF16),
            pad_rows(jnp.real(e)).astype(BF16), pad_rows(-jnp.imag(e)).astype(BF16),
            pad(jnp.real(lam_l)), pad(jnp.imag(lam_l)))


def _s5(u, tables):
    b, s, _ = u.shape
    length = min(S5_CHUNK, s)
    chunks = s // length
    rows = b * chunks
    width = length * S5_GROUP
    ug = u.astype(BF16).reshape(b, chunks, length, S5_GROUPS, S5_GROUP)
    ug = ug.transpose(3, 0, 1, 2, 4).reshape(S5_GROUPS, rows, width)
    toep, fre, fim, ere, eim, lre, lim = tables
    grp = lambda r, c: pl.BlockSpec((1, r, c), lambda gi: (gi, 0, 0))
    y = pl.pallas_call(
        functools.partial(_s5_kernel, batch=b, chunks=chunks),
        grid=(S5_GROUPS,),
        in_specs=[grp(rows, width), grp(width, width), grp(width, LANES_V7X), grp(width, LANES_V7X),
                  grp(LANES_V7X, width), grp(LANES_V7X, width), grp(1, LANES_V7X), grp(1, LANES_V7X)],
        out_specs=grp(rows, width),
        out_shape=jax.ShapeDtypeStruct((S5_GROUPS, rows, width), F32),
        scratch_shapes=[pltpu.VMEM((rows, LANES_V7X), F32)] * 4,
        compiler_params=_params("parallel"),
        name="s5",
    )(ug, toep, fre, fim, ere, eim, lre, lim)
    y = y.reshape(S5_GROUPS, b, chunks, length, S5_GROUP).transpose(1, 2, 3, 0, 4)
    return y.reshape(b, s, S5_WIDTH)


def _rotary_tables(seq):
    half = RET_QK_DIM // 2
    inv_freq = 1.0 / (ROPE_BASE ** jnp.linspace(0.0, 1.0, half, dtype=F32))
    ang = jnp.arange(seq, dtype=F32)[:, None] * inv_freq[None, :]
    cos, sin = jnp.cos(ang), jnp.sin(ang)
    cos_t = jnp.tile(jnp.concatenate([cos, cos], axis=1), (1, RET_HEADS))
    sin_t = jnp.tile(jnp.concatenate([-sin, sin], axis=1), (1, RET_HEADS))
    return cos_t, sin_t


def _row(v):
    return v.reshape(1, -1).astype(F32)


def kernel(x, norm_mix_g, norm_mlp_g, w_up, w_down, ab_w_in, ab_w_out, da_q_norm, da_k_norm,
           da_lam_q1, da_lam_k1, da_lam_q2, da_lam_k2, da_out_norm, ret_out_norm, cd_w_in, cd_w_out,
           s5_a_re, s5_a_im, s5_log_step, s5_b_re, s5_b_im, s5_c_re, s5_c_im, s5_d, s5_w_glu,
           gla_w_a2, gla_b_a2, gla_out_norm):
    bsz, seq, _ = x.shape
    t = bsz * seq
    x2 = x.reshape(t, D_MODEL)

    cos_t, sin_t = _rotary_tables(seq)
    cos_t = jnp.tile(cos_t, (bsz, 1))
    sin_t = jnp.tile(sin_t, (bsz, 1))
    gsum = ((jnp.arange(512)[:, None] // DA_QK_DIM) == (jnp.arange(512)[None, :] // DA_QK_DIM))
    gsum = (gsum.astype(F32) / DA_QK_DIM).astype(BF16)
    gq = _row(jnp.tile(da_q_norm[0], 2 * DA_HEADS)) * (DA_QK_DIM ** -0.5)
    gk = _row(jnp.tile(da_k_norm[0], 2 * DA_HEADS))
    qa, ka, va, qr, kr, vr, gr = _ab_in(x2, _row(norm_mix_g[0]), ab_w_in[0].astype(BF16), gq, gk, gsum,
                                        cos_t, sin_t)
    lam_init = 0.8 - 0.6 * math.exp(-0.3 * 0)
    seq3 = lambda a: a.reshape(bsz, seq, a.shape[-1])
    o_a = _diff_attn(seq3(qa), seq3(ka), seq3(va), _row(da_lam_q1[0]), _row(da_lam_k1[0]),
                     _row(da_lam_q2[0]), _row(da_lam_k2[0]), _row(da_out_norm[0]), lam_init)
    o_r = _retention(seq3(qr), seq3(kr), seq3(vr), seq3(gr), _row(ret_out_norm[0]))
    x2 = _out_mlp(x2, o_a.reshape(t, DA_WIDTH), o_r.reshape(t, RET_WIDTH), ab_w_out[0].astype(BF16),
                  _row(norm_mlp_g[0]), w_up[0].astype(BF16), w_down[0].astype(BF16))

    w_cd = jnp.pad(cd_w_in[0], ((0, 0), (0, CD_IN_PADDED - CD_IN))).astype(BF16)
    wa = jnp.pad(gla_w_a2[0], ((0, LANES_V7X - GLA_GATE_RANK), (0, 0))).astype(BF16)
    u, qg, kg, vg, rg, la = _cd_in(x2, _row(norm_mix_g[1]), w_cd, wa, _row(gla_b_a2[0]))
    o_d = _gla(seq3(qg), seq3(kg), seq3(vg), seq3(rg), seq3(la), _row(gla_out_norm[0]))
    tables = _s5_tables(s5_a_re[0], s5_a_im[0], s5_log_step[0], s5_b_re[0], s5_b_im[0], s5_c_re[0],
                        s5_c_im[0], min(S5_CHUNK, seq))
    y = _s5(seq3(u), tables)
    x2 = _s5_out_mlp(x2, y.reshape(t, S5_WIDTH), u, _row(s5_d[0]), s5_w_glu[0].astype(BF16),
                     o_d.reshape(t, GLA_WIDTH), cd_w_out[0].astype(BF16), _row(norm_mlp_g[1]),
                     w_up[1].astype(BF16), w_down[1].astype(BF16))
    return x2.reshape(bsz, seq, D_MODEL)
```

```python
import functools
import math

import jax
import jax.numpy as jnp
from jax import lax
from jax.experimental import pallas as pl
from jax.experimental.pallas import tpu as pltpu

F32 = jnp.float32
BF16 = jnp.bfloat16

D_MODEL = 1024
CHUNK = 64
RMS_EPS = 1e-6
ROPE_BASE = 10000.0
DA_HEADS = 4
DA_QK_DIM = 64
DA_V_DIM = 128
DA_WIDTH = DA_HEADS * DA_V_DIM
RET_HEADS = 4
RET_QK_DIM = 64
RET_V_DIM = 128
RET_WIDTH = RET_HEADS * RET_V_DIM
S5_WIDTH = 256
S5_GROUP = 16
S5_GROUPS = S5_WIDTH // S5_GROUP
S5_STATE = 64
GLA_HEADS = 6
GLA_QK_DIM = 64
GLA_V_DIM = 128
GLA_WIDTH = GLA_HEADS * GLA_V_DIM
GLA_QK_WIDTH = GLA_HEADS * GLA_QK_DIM
GLA_GATE_RANK = 16
GLA_TAU = 16.0
MLP_HIDDEN = 4 * D_MODEL
AB_IN = 3072
CD_IN = 2576

LANES_V7X = 128
VMEM_LIMIT_BYTES_V7X = 56 * 1024 * 1024

CD_IN_PADDED = 2688
NEG_BIG = -1e30

TOKEN_TILE = 512
ATTN_BLOCK = 512
RET_BLOCK = 512
GLA_BLOCK = 256
S5_CHUNK = 64
MLP_HIDDEN_TILE = 1024


def _params(*semantics):
    return pltpu.CompilerParams(dimension_semantics=semantics, vmem_limit_bytes=VMEM_LIMIT_BYTES_V7X)


def _const_spec(shape):
    zeros = (0,) * len(shape)
    return pl.BlockSpec(shape, lambda *_: zeros, pipeline_mode=pl.Buffered(1))


def _rms(xf, gain):
    return xf * lax.rsqrt(jnp.mean(xf * xf, axis=-1, keepdims=True) + RMS_EPS) * gain


def _dot(a, b):
    return jnp.dot(a, b, preferred_element_type=F32)


def _dot_nt(a, b):
    return lax.dot_general(a, b, (((1,), (1,)), ((), ())), preferred_element_type=F32)


def _dot_tn(a, b):
    return lax.dot_general(a, b, (((0,), (0,)), ((), ())), preferred_element_type=F32)


def _swap_halves(x, group):
    n = x.shape[-1]
    half = group // 2
    lane = lax.broadcasted_iota(jnp.int32, x.shape, x.ndim - 1)
    from_right = pltpu.roll(x, n - half, axis=x.ndim - 1)
    from_left = pltpu.roll(x, half, axis=x.ndim - 1)
    return jnp.where((lane % group) < half, from_right, from_left)


def _ab_in_kernel(x_ref, g_ref, w_ref, wvt_ref, gq_ref, gk_ref, gsum_ref, cos_ref, sin_ref,
                  qa_ref, ka_ref, va_ref, qr_ref, kr_ref, vr_ref, gr_ref):
    h = _rms(x_ref[...], g_ref[...]).astype(BF16)

    def proj(lo, hi):
        return _dot(h, w_ref[:, lo:hi])

    def qk_norm(y, gain):
        ms = _dot((y * y).astype(BF16), gsum_ref[...])
        return (y * lax.rsqrt(ms + RMS_EPS) * gain).astype(BF16)

    def rotary(y):
        return y * cos_ref[...] + _swap_halves(y, RET_QK_DIM) * sin_ref[...]

    qa_ref[...] = qk_norm(proj(0, 512), gq_ref[...])
    ka_ref[...] = qk_norm(proj(512, 1024), gk_ref[...])
    va_ref[0] = _dot_nt(wvt_ref[...], h).astype(BF16)
    qr_ref[...] = rotary(proj(1536, 1792))
    kr_ref[...] = rotary(proj(1792, 2048)) * (RET_QK_DIM ** -0.5)
    vr_ref[...] = proj(2048, 2560).astype(BF16)
    gr_ref[...] = proj(2560, 3072)


def _ab_in(x2, g, w, wv_t, gq, gk, gsum, cos_t, sin_t, seq):
    t = x2.shape[0]
    tm = min(TOKEN_TILE, seq)
    per_seq = seq // tm
    row = lambda width: pl.BlockSpec((tm, width), lambda i: (i, 0))
    va_spec = pl.BlockSpec((1, DA_WIDTH, tm), lambda i: (i // per_seq, 0, i % per_seq))
    out_shapes = (
        jax.ShapeDtypeStruct((t, 512), BF16), jax.ShapeDtypeStruct((t, 512), BF16),
        jax.ShapeDtypeStruct((t // seq, DA_WIDTH, seq), BF16), jax.ShapeDtypeStruct((t, 256), F32),
        jax.ShapeDtypeStruct((t, 256), F32), jax.ShapeDtypeStruct((t, 512), BF16),
        jax.ShapeDtypeStruct((t, 512), F32))
    return pl.pallas_call(
        _ab_in_kernel,
        grid=(t // tm,),
        in_specs=[row(D_MODEL), _const_spec((1, D_MODEL)), _const_spec((D_MODEL, AB_IN)),
                  _const_spec((DA_WIDTH, D_MODEL)),
                  _const_spec((1, 512)), _const_spec((1, 512)), _const_spec((512, 512)),
                  row(256), row(256)],
        out_specs=(row(512), row(512), va_spec, row(256), row(256), row(512), row(512)),
        out_shape=out_shapes,
        compiler_params=_params("parallel"),
        name="ab_in_proj",
    )(x2, g, w, wv_t, gq, gk, gsum, cos_t, sin_t)


def _diff_attn_kernel(lq1_ref, lk1_ref, lq2_ref, lk2_ref, gout_ref, q_ref, k_ref, vt_ref, o_ref,
                      sa_ref, sb_ref, mxa_ref, mxb_ref, m_ref, l_ref, acc_ref, *, blk, lam_init):
    i = pl.program_id(2)
    q = q_ref[0]
    lane = lax.broadcasted_iota(jnp.int32, q.shape, 1)
    zero = jnp.zeros_like(q)
    qz = jnp.concatenate([jnp.where(lane < DA_QK_DIM, q, zero), jnp.where(lane >= DA_QK_DIM, q, zero)], axis=0)

    def score(t, s_ref, mx_ref):
        start = pl.multiple_of(t * blk, blk)
        st = _dot_nt(k_ref[0, pl.ds(start, blk), :], qz)
        s_ref[...] = st
        mx_ref[...] = jnp.max(st, axis=0, keepdims=True)

    def absorb(t, s_ref, mx_ref, diagonal):
        start = pl.multiple_of(t * blk, blk)
        st = s_ref[...]
        if diagonal:
            key = lax.broadcasted_iota(jnp.int32, st.shape, 0)
            qry = lax.broadcasted_iota(jnp.int32, st.shape, 1)
            st = jnp.where(key // CHUNK <= (qry % blk) // CHUNK, st, NEG_BIG)
            mx = jnp.max(st, axis=0, keepdims=True)
        else:
            mx = mx_ref[...]
        m = m_ref[...]
        m_new = jnp.maximum(m, mx)
        alpha = jnp.exp2(m - m_new)
        p = jnp.exp2(st - m_new)
        l_ref[...] = alpha * l_ref[...] + jnp.sum(p, axis=0, keepdims=True)
        acc_ref[...] = alpha * acc_ref[...] + _dot(vt_ref[0, :, pl.ds(start, blk)], p.astype(BF16))
        m_ref[...] = m_new

    m_ref[...] = jnp.full(m_ref.shape, NEG_BIG, F32)
    l_ref[...] = jnp.zeros(l_ref.shape, F32)
    acc_ref[...] = jnp.zeros(acc_ref.shape, F32)
    score(0, sa_ref, mxa_ref)

    def body(u, carry):
        t = 2 * u
        score(t + 1, sb_ref, mxb_ref)
        absorb(t, sa_ref, mxa_ref, False)
        score(t + 2, sa_ref, mxa_ref)
        absorb(t + 1, sb_ref, mxb_ref, False)
        return carry

    lax.fori_loop(0, lax.shift_right_logical(i, 1), body, 0)

    @pl.when((i & 1) == 0)
    def _():
        absorb(i, sa_ref, mxa_ref, True)

    @pl.when((i & 1) == 1)
    def _():
        score(i, sb_ref, mxb_ref)
        absorb(i - 1, sa_ref, mxa_ref, False)
        absorb(i, sb_ref, mxb_ref, True)

    l = l_ref[...]
    acc = acc_ref[...]
    lam =(jnp.exp(jnp.sum(lq1_ref[...] * lk1_ref[...], axis=-1, keepdims=True))
           - jnp.exp(jnp.sum(lq2_ref[...] * lk2_ref[...], axis=-1, keepdims=True)) + lam_init)
    o = acc / l
    o = o[:, :blk] - lam * o[:, blk:]
    o = o * lax.rsqrt(jnp.mean(o * o, axis=0, keepdims=True) + RMS_EPS)
    o_ref[0] = (o.T * (gout_ref[...] * (1.0 - lam_init))).astype(BF16)


def _diff_attn(qa, ka, va_t, lq1, lk1, lq2, lk2, gout, lam_init):
    b, s, _ = qa.shape
    blk = min(ATTN_BLOCK, s)
    vec = _const_spec((1, DA_QK_DIM))
    q_spec = pl.BlockSpec((1, blk, 128), lambda bi, h, i: (bi, i, h))
    k_spec = pl.BlockSpec((1, s, 128), lambda bi, h, i: (bi, 0, h))
    vt_spec = pl.BlockSpec((1, DA_V_DIM, s), lambda bi, h, i: (bi, h, 0))
    return pl.pallas_call(
        functools.partial(_diff_attn_kernel, blk=blk, lam_init=lam_init),
        grid=(b, DA_HEADS, s // blk),
        in_specs=[vec, vec, vec, vec, _const_spec((1, DA_V_DIM)), q_spec, k_spec, vt_spec],
        out_specs=q_spec,
        out_shape=jax.ShapeDtypeStruct((b, s, DA_WIDTH), BF16),
        scratch_shapes=[pltpu.VMEM((blk, 2 * blk), F32), pltpu.VMEM((blk, 2 * blk), F32),
                        pltpu.VMEM((1, 2 * blk), F32), pltpu.VMEM((1, 2 * blk), F32),
                        pltpu.VMEM((1, 2 * blk), F32), pltpu.VMEM((1, 2 * blk), F32),
                        pltpu.VMEM((DA_V_DIM, 2 * blk), F32)],
        compiler_params=_params("parallel", "parallel", "arbitrary"),
        name="diff_attention",
    )(lq1, lk1, lq2, lk2, gout, qa, ka, va_t)


def _retention_kernel(dec_ref, qdec_ref, kdec_ref, sdec_ref, smask_ref, gain_ref,
                      q_ref, k_ref, v_ref, g_ref, o_ref, state_ref):
    @pl.when(pl.program_id(1) == 0)
    def _():
        state_ref[...] = jnp.zeros_like(state_ref)

    q = q_ref[0]
    k = k_ref[0]
    v = v_ref[0]
    kb = k.astype(BF16)
    lane = lax.broadcasted_iota(jnp.int32, q.shape, 1)
    state = state_ref[...]
    o_cross = _dot((q * qdec_ref[...]).astype(BF16), state.astype(BF16))
    for h in range(RET_HEADS):
        qh = jnp.where(lane // RET_QK_DIM == h, q, 0.0).astype(BF16)
        w = (_dot_nt(qh, kb) * dec_ref[h]).astype(BF16)
        vs = slice(h * RET_V_DIM, (h + 1) * RET_V_DIM)
        o = _dot(w, v[:, vs]) + o_cross[:, vs]
        o = _rms(o, gain_ref[...])
        g = g_ref[0, :, vs]
        o_ref[0, :, vs] = (o * (g * jax.nn.sigmoid(g))).astype(BF16)
    kv = _dot_tn((k * kdec_ref[...]).astype(BF16), v)
    state_ref[...] = sdec_ref[...] * state + kv * smask_ref[...]


def _retention_tables(blk):
    heads = jnp.arange(RET_HEADS, dtype=F32)
    log_gamma = jnp.log(1.0 - 2.0 ** (-5.0 - heads))
    idx = jnp.arange(blk)
    diff = (idx[:, None] - idx[None, :]).astype(F32)
    same_chunk = (idx[:, None] // CHUNK) == (idx[None, :] // CHUNK)
    visible = (idx[None, :] <= idx[:, None]) | same_chunk
    dec = jnp.where(visible[None], jnp.exp(log_gamma[:, None, None] * jnp.abs(diff)[None]), 0.0)
    per_lane = jnp.repeat(log_gamma, RET_QK_DIM)
    pos = jnp.arange(blk, dtype=F32)
    qdec = jnp.exp((pos[:, None] + 1.0) * per_lane[None, :])
    kdec = jnp.exp((blk - 1.0 - pos)[:, None] * per_lane[None, :])
    sdec = jnp.exp(blk * per_lane)[:, None]
    smask = (jnp.arange(RET_HEADS * RET_QK_DIM)[:, None] // RET_QK_DIM
             == jnp.arange(RET_WIDTH)[None, :] // RET_V_DIM).astype(F32)
    return dec, qdec, kdec, sdec, smask


def _retention(qr, kr, vr, gr, gain):
    b, s, _ = qr.shape
    blk = min(RET_BLOCK, s)
    dec, qdec, kdec, sdec, smask = _retention_tables(blk)
    tok = lambda width: pl.BlockSpec((1, blk, width), lambda bi, i: (bi, i, 0))
    return pl.pallas_call(
        _retention_kernel,
        grid=(b, s // blk),
        in_specs=[_const_spec(dec.shape), _const_spec(qdec.shape), _const_spec(kdec.shape),
                  _const_spec(sdec.shape), _const_spec(smask.shape), _const_spec((1, RET_V_DIM)),
                  tok(256), tok(256), tok(512), tok(512)],
        out_specs=tok(RET_WIDTH),
        out_shape=jax.ShapeDtypeStruct((b, s, RET_WIDTH), BF16),
        scratch_shapes=[pltpu.VMEM((RET_HEADS * RET_QK_DIM, RET_WIDTH), F32)],
        compiler_params=_params("parallel", "arbitrary"),
        name="retention",
    )(dec, qdec, kdec, sdec, smask, gain, qr, kr, vr, gr)


def _mlp_tail(x1, g_ref, wup_ref, wdn_ref, o_ref):
    h = _rms(x1, g_ref[...]).astype(BF16)
    acc = x1
    for c in range(MLP_HIDDEN // MLP_HIDDEN_TILE):
        cs = slice(c * MLP_HIDDEN_TILE, (c + 1) * MLP_HIDDEN_TILE)
        u = jnp.maximum(_dot(h, wup_ref[:, cs]), 0.0)
        acc = acc + _dot((u * u).astype(BF16), wdn_ref[cs, :])
    o_ref[...] = acc


def _out_mlp_kernel(x_ref, a_ref, b_ref, wout_ref, g_ref, wup_ref, wdn_ref, o_ref):
    ka = a_ref.shape[1]
    mixed = _dot(a_ref[...], wout_ref[:ka, :]) + _dot(b_ref[...], wout_ref[ka:, :])
    _mlp_tail(x_ref[...] + mixed, g_ref, wup_ref, wdn_ref, o_ref)


def _s5_out_mlp_kernel(x_ref, y_ref, u_ref, d_ref, wglu_ref, b_ref, wout_ref, g_ref, wup_ref, wdn_ref, o_ref):
    y = y_ref[...] + d_ref[...] * u_ref[...]
    z = jax.nn.gelu(y)
    a = (z * jax.nn.sigmoid(_dot(z.astype(BF16), wglu_ref[...]))).astype(BF16)
    mixed = _dot(a, wout_ref[:S5_WIDTH, :]) + _dot(b_ref[...], wout_ref[S5_WIDTH:, :])
    _mlp_tail(x_ref[...] + mixed, g_ref, wup_ref, wdn_ref, o_ref)


def _mlp_specs():
    return [_const_spec((D_MODEL, D_MODEL)), _const_spec((1, D_MODEL)),
            _const_spec((D_MODEL, MLP_HIDDEN)), _const_spec((MLP_HIDDEN, D_MODEL))]


def _out_mlp(x2, a, bb, wout, g, wup, wdn):
    t = x2.shape[0]
    tm = TOKEN_TILE
    row = lambda width: pl.BlockSpec((tm, width), lambda i: (i, 0))
    return pl.pallas_call(
        _out_mlp_kernel,
        grid=(t // tm,),
        in_specs=[row(D_MODEL), row(a.shape[1]), row(bb.shape[1])] + _mlp_specs(),
        out_specs=row(D_MODEL),
        out_shape=jax.ShapeDtypeStruct((t, D_MODEL), F32),
        compiler_params=_params("parallel"),
        name="ab_out_mlp",
    )(x2, a, bb, wout, g, wup, wdn)


def _s5_out_mlp(x2, y, u, d, wglu, bb, wout, g, wup, wdn):
    t = x2.shape[0]
    tm = TOKEN_TILE
    row = lambda width: pl.BlockSpec((tm, width), lambda i: (i, 0))
    return pl.pallas_call(
        _s5_out_mlp_kernel,
        grid=(t // tm,),
        in_specs=[row(D_MODEL), row(S5_WIDTH), row(S5_WIDTH), _const_spec((1, S5_WIDTH)),
                  _const_spec((S5_WIDTH, S5_WIDTH)), row(GLA_WIDTH)] + _mlp_specs(),
        out_specs=row(D_MODEL),
        out_shape=jax.ShapeDtypeStruct((t, D_MODEL), F32),
        compiler_params=_params("parallel"),
        name="cd_out_mlp",
    )(x2, y, u, d, wglu, bb, wout, g, wup, wdn)


def _cd_in_kernel(x_ref, g_ref, w_ref, wa_ref, ba_ref, u_ref, q_ref, k_ref, v_ref, r_ref, la_ref):
    h = _rms(x_ref[...], g_ref[...]).astype(BF16)

    def proj(lo, hi):
        return _dot(h, w_ref[:, lo:hi])

    u_ref[...] = proj(0, 256)
    q_ref[...] = proj(256, 640) * (GLA_QK_DIM ** -0.5)
    k_ref[...] = proj(640, 1024)
    v_ref[...] = proj(1024, 1792).astype(BF16)
    r_ref[...] = proj(1792, 2560)
    a_lr = proj(2560, CD_IN_PADDED)
    pre = _dot(a_lr.astype(BF16), wa_ref[...]) + ba_ref[...]
    log_sig = jnp.minimum(pre, 0.0) - jnp.log1p(jnp.exp(-jnp.abs(pre)))
    la_ref[...] = log_sig / GLA_TAU


def _cd_in(x2, g, w, wa, ba):
    t = x2.shape[0]
    tm = TOKEN_TILE
    row = lambda width: pl.BlockSpec((tm, width), lambda i: (i, 0))
    out_shapes = (
        jax.ShapeDtypeStruct((t, S5_WIDTH), F32), jax.ShapeDtypeStruct((t, GLA_QK_WIDTH), F32),
        jax.ShapeDtypeStruct((t, GLA_QK_WIDTH), F32), jax.ShapeDtypeStruct((t, GLA_WIDTH), BF16),
        jax.ShapeDtypeStruct((t, GLA_WIDTH), F32), jax.ShapeDtypeStruct((t, GLA_QK_WIDTH), F32))
    return pl.pallas_call(
        _cd_in_kernel,
        grid=(t // tm,),
        in_specs=[row(D_MODEL), _const_spec((1, D_MODEL)), _const_spec((D_MODEL, CD_IN_PADDED)),
                  _const_spec((LANES_V7X, GLA_QK_WIDTH)), _const_spec((1, GLA_QK_WIDTH))],
        out_specs=(row(S5_WIDTH), row(GLA_QK_WIDTH), row(GLA_QK_WIDTH), row(GLA_WIDTH), row(GLA_WIDTH),
                   row(GLA_QK_WIDTH)),
        out_shape=out_shapes,
        compiler_params=_params("parallel"),
        name="cd_in_proj",
    )(x2, g, w, wa, ba)


def _split3(x):
    hi = x.astype(BF16)
    r = x - hi.astype(F32)
    mid = r.astype(BF16)
    lo = (r - mid.astype(F32)).astype(BF16)
    return hi, mid, lo


def _gla_kernel(tri_ref, gain_ref, q_ref, k_ref, v_ref, r_ref, la_ref, o_ref, state_ref, *, blk):
    @pl.when(pl.program_id(1) == 0)
    def _():
        state_ref[...] = jnp.zeros_like(state_ref)

    nc = blk // CHUNK
    q = q_ref[0]
    k = k_ref[0]
    tri = tri_ref[...]
    hi, mid, lo = _split3(la_ref[0])
    b = _dot(tri, hi) + _dot(tri, mid) + _dot(tri, lo)
    b3 = b.reshape(nc, CHUNK, GLA_QK_WIDTH)
    b_last = b3[:, CHUNK - 1:CHUNK, :]
    e_pos = jnp.exp(b)
    e_neg = jnp.exp(-b)
    k_tail = jnp.exp(b_last - b3).reshape(blk, GLA_QK_WIDTH)
    chunk_decay = jnp.exp(b_last)
    qp = (q * e_pos).astype(BF16)
    qn = (q * e_neg).astype(BF16)
    kp = (k * e_pos).astype(BF16)
    kn = (k * e_neg).astype(BF16)
    kw = (k * k_tail).astype(BF16)

    row = lax.broadcasted_iota(jnp.int32, (blk, blk), 0)
    col = lax.broadcasted_iota(jnp.int32, (blk, blk), 1)
    same_chunk = row // CHUNK == col // CHUNK
    causal = row >= col
    lane = lax.broadcasted_iota(jnp.int32, (blk, 2 * GLA_QK_DIM), 1)
    smask = (lax.broadcasted_iota(jnp.int32, (2 * GLA_V_DIM, 2 * GLA_QK_DIM), 0) // GLA_V_DIM
             == lax.broadcasted_iota(jnp.int32, (2 * GLA_V_DIM, 2 * GLA_QK_DIM), 1) // GLA_QK_DIM)

    for p in range(GLA_HEADS // 2):
        ks = slice(p * 2 * GLA_QK_DIM, (p + 1) * 2 * GLA_QK_DIM)
        vs = slice(p * 2 * GLA_V_DIM, (p + 1) * 2 * GLA_V_DIM)
        v = v_ref[0, :, vs]
        intra = []
        for half in range(2):
            sel = (lane // GLA_QK_DIM) == half
            fwd = _dot_nt(jnp.where(sel, qp[:, ks], 0).astype(BF16), kn[:, ks])
            bwd = _dot_nt(jnp.where(sel, qn[:, ks], 0).astype(BF16), kp[:, ks])
            sc = jnp.where(same_chunk, jnp.where(causal, fwd, bwd), 0.0).astype(BF16)
            intra.append(_dot(sc, v[:, half * GLA_V_DIM:(half + 1) * GLA_V_DIM]))
        o_intra = jnp.concatenate(intra, axis=1)

        state = state_ref[p]
        cross = []
        for c in range(nc):
            rs = slice(c * CHUNK, (c + 1) * CHUNK)
            cross.append(_dot_nt(qp[rs, ks], state.astype(BF16)))
            kv_t = _dot_tn(v[rs, :], kw[rs, ks])
            state = state * chunk_decay[c][:, ks] + jnp.where(smask, kv_t, 0.0)
        state_ref[p] = state
        o = o_intra + jnp.concatenate(cross, axis=0)
        for half in range(2):
            hs = slice(half * GLA_V_DIM, (half + 1) * GLA_V_DIM)
            os_ = slice(p * 2 * GLA_V_DIM + half * GLA_V_DIM, p * 2 * GLA_V_DIM + (half + 1) * GLA_V_DIM)
            g = r_ref[0, :, os_]
            o_ref[0, :, os_] = (_rms(o[:, hs], gain_ref[...]) * (g * jax.nn.sigmoid(g))).astype(BF16)


def _gla(qg, kg, vg, rg, la, gain):
    b, s, _ = qg.shape
    blk = min(GLA_BLOCK, s)
    idx = jnp.arange(blk)
    tri = (((idx[:, None] // CHUNK) == (idx[None, :] // CHUNK)) & (idx[None, :] <= idx[:, None])).astype(BF16)
    tok = lambda width: pl.BlockSpec((1, blk, width), lambda bi, i: (bi, i, 0))
    return pl.pallas_call(
        functools.partial(_gla_kernel, blk=blk),
        grid=(b, s // blk),
        in_specs=[_const_spec((blk, blk)), _const_spec((1, GLA_V_DIM)),
                  tok(GLA_QK_WIDTH), tok(GLA_QK_WIDTH), tok(GLA_WIDTH), tok(GLA_WIDTH), tok(GLA_QK_WIDTH)],
        out_specs=tok(GLA_WIDTH),
        out_shape=jax.ShapeDtypeStruct((b, s, GLA_WIDTH), BF16),
        scratch_shapes=[pltpu.VMEM((GLA_HEADS // 2, 2 * GLA_V_DIM, 2 * GLA_QK_DIM), F32)],
        compiler_params=_params("parallel", "arbitrary"),
        name="gla",
    )(tri, gain, qg, kg, vg, rg, la)


def _s5_kernel(u_ref, toep_ref, fre_ref, fim_ref, ere_ref, eim_ref, lre_ref, lim_ref, y_ref,
               vre_ref, vim_ref, hre_ref, him_ref, *, batch, chunks):
    u = u_ref[0]
    vre_ref[...] = _dot(u, fre_ref[0])
    vim_ref[...] = _dot(u, fim_ref[0])
    lre = lre_ref[0]
    lim = lim_ref[0]

    def step(n, carry):
        new = []
        for bi in range(batch):
            hr, hi = carry[2 * bi], carry[2 * bi + 1]
            r = bi * chunks + n
            hre_ref[pl.ds(r, 1), :] = hr
            him_ref[pl.ds(r, 1), :] = hi
            vr = vre_ref[pl.ds(r, 1), :]
            vi = vim_ref[pl.ds(r, 1), :]
            new += [lre * hr - lim * hi + vr, lre * hi + lim * hr + vi]
        return tuple(new)

    zero = jnp.zeros((1, LANES_V7X), F32)
    lax.fori_loop(0, chunks, step, (zero,) * (2 * batch))
    y_ref[0] = (_dot(u, toep_ref[0]) + _dot(hre_ref[...].astype(BF16), ere_ref[0])
                + _dot(him_ref[...].astype(BF16), eim_ref[0]))


def _s5_tables(a_re, a_im, log_step, b_re, b_im, c_re, c_im, length):
    hp = lax.Precision.HIGHEST
    g, p, c = S5_GROUPS, S5_STATE, S5_GROUP
    lam = lax.complex(a_re, a_im)
    delta = jnp.exp(log_step)[:, None]
    lam_bar = jnp.exp(lam * delta)
    b_bar = ((lam_bar - 1.0) / lam)[:, :, None] * lax.complex(b_re, b_im)
    cc = lax.complex(c_re, c_im)
    tau = jnp.arange(length + 1, dtype=F32)
    pw = jnp.exp((lam * delta)[:, None, :] * tau[None, :, None])
    cp = cc[:, None, :, :] * pw[:, :length, None, :]
    kern = (jnp.einsum('gtcp,gpd->gtcd', jnp.real(cp), jnp.real(b_bar), precision=hp)
            - jnp.einsum('gtcp,gpd->gtcd', jnp.imag(cp), jnp.imag(b_bar), precision=hp))
    s_idx = jnp.arange(length)
    lag = s_idx[None, :] - s_idx[:, None]
    toep = jnp.where((lag >= 0)[None, :, :, None, None], kern[:, jnp.clip(lag, 0, length - 1)], 0.0)
    toep = toep.transpose(0, 1, 4, 2, 3).reshape(g, length * c, length * c)
    f = pw[:, length - 1 - s_idx, :][:, :, None, :] * b_bar.transpose(0, 2, 1)[:, None, :, :]
    f = f.reshape(g, length * c, p)
    pad = lambda z: jnp.pad(z, ((0, 0), (0, 0), (0, LANES_V7X - p)))
    e = cc[:, None, :, :] * pw[:, 1:, None, :]
    e = e.transpose(0, 3, 1, 2).reshape(g, p, length * c)
    pad_rows = lambda z: jnp.pad(z, ((0, 0), (0, LANES_V7X - p), (0, 0)))
    lam_l = pw[:, length, :][:, None, :]
    return (toep.astype(BF16), pad(jnp.real(f)).astype(BF16), pad(jnp.imag(f)).astype(BF16),
            pad_rows(jnp.real(e)).astype(BF16), pad_rows(-jnp.imag(e)).astype(BF16),
            pad(jnp.real(lam_l)), pad(jnp.imag(lam_l)))


def _s5(u, tables):
    b, s, _ = u.shape
    length = min(S5_CHUNK, s)
    chunks = s // length
    rows = b * chunks
    width = length * S5_GROUP
    ug = u.astype(BF16).reshape(b, chunks, length, S5_GROUPS, S5_GROUP)
    ug = ug.transpose(3, 0, 1, 2, 4).reshape(S5_GROUPS, rows, width)
    toep, fre, fim, ere, eim, lre, lim = tables
    grp = lambda r, c: pl.BlockSpec((1, r, c), lambda gi: (gi, 0, 0))
    y = pl.pallas_call(
        functools.partial(_s5_kernel, batch=b, chunks=chunks),
        grid=(S5_GROUPS,),
        in_specs=[grp(rows, width), grp(width, width), grp(width, LANES_V7X), grp(width, LANES_V7X),
                  grp(LANES_V7X, width), grp(LANES_V7X, width), grp(1, LANES_V7X), grp(1, LANES_V7X)],
        out_specs=grp(rows, width),
        out_shape=jax.ShapeDtypeStruct((S5_GROUPS, rows, width), F32),
        scratch_shapes=[pltpu.VMEM((rows, LANES_V7X), F32)] * 4,
        compiler_params=_params("parallel"),
        name="s5",
    )(ug, toep, fre, fim, ere, eim, lre, lim)
    y = y.reshape(S5_GROUPS, b, chunks, length, S5_GROUP).transpose(1, 2, 3, 0, 4)
    return y.reshape(b, s, S5_WIDTH)


def _rotary_tables(seq):
    half = RET_QK_DIM // 2
    inv_freq = 1.0 / (ROPE_BASE ** jnp.linspace(0.0, 1.0, half, dtype=F32))
    ang = jnp.arange(seq, dtype=F32)[:, None] * inv_freq[None, :]
    cos, sin = jnp.cos(ang), jnp.sin(ang)
    cos_t = jnp.tile(jnp.concatenate([cos, cos], axis=1), (1, RET_HEADS))
    sin_t = jnp.tile(jnp.concatenate([-sin, sin], axis=1), (1, RET_HEADS))
    return cos_t, sin_t


def _row(v):
    return v.reshape(1, -1).astype(F32)


def kernel(x, norm_mix_g, norm_mlp_g, w_up, w_down, ab_w_in, ab_w_out, da_q_norm, da_k_norm,
           da_lam_q1, da_lam_k1, da_lam_q2, da_lam_k2, da_out_norm, ret_out_norm, cd_w_in, cd_w_out,
           s5_a_re, s5_a_im, s5_log_step, s5_b_re, s5_b_im, s5_c_re, s5_c_im, s5_d, s5_w_glu,
           gla_w_a2, gla_b_a2, gla_out_norm):
    bsz, seq, _ = x.shape
    t = bsz * seq
    x2 = x.reshape(t, D_MODEL)

    cos_t, sin_t = _rotary_tables(seq)
    cos_t = jnp.tile(cos_t, (bsz, 1))
    sin_t = jnp.tile(sin_t, (bsz, 1))
    gsum = ((jnp.arange(512)[:, None] // DA_QK_DIM) == (jnp.arange(512)[None, :] // DA_QK_DIM))
    gsum = (gsum.astype(F32) / DA_QK_DIM).astype(BF16)
    gq = _row(jnp.tile(da_q_norm[0], 2 * DA_HEADS)) * (DA_QK_DIM ** -0.5 * math.log2(math.e))
    gk = _row(jnp.tile(da_k_norm[0], 2 * DA_HEADS))
    w_ab = ab_w_in[0].astype(BF16)
    qa, ka, va_t, qr, kr, vr, gr = _ab_in(x2, _row(norm_mix_g[0]), w_ab, w_ab[:, 1024:1536].T, gq, gk, gsum,
                                          cos_t, sin_t, seq)
    lam_init = 0.8 - 0.6 * math.exp(-0.3 * 0)
    seq3 = lambda a: a.reshape(bsz, seq, a.shape[-1])
    o_a = _diff_attn(seq3(qa), seq3(ka), va_t, _row(da_lam_q1[0]), _row(da_lam_k1[0]),
                     _row(da_lam_q2[0]), _row(da_lam_k2[0]), _row(da_out_norm[0]), lam_init)
    o_r = _retention(seq3(qr), seq3(kr), seq3(vr), seq3(gr), _row(ret_out_norm[0]))
    x2 = _out_mlp(x2, o_a.reshape(t, DA_WIDTH), o_r.reshape(t, RET_WIDTH), ab_w_out[0].astype(BF16),
                  _row(norm_mlp_g[0]), w_up[0].astype(BF16), w_down[0].astype(BF16))

    w_cd = jnp.pad(cd_w_in[0], ((0, 0), (0, CD_IN_PADDED - CD_IN))).astype(BF16)
    wa = jnp.pad(gla_w_a2[0], ((0, LANES_V7X - GLA_GATE_RANK), (0, 0))).astype(BF16)
    u, qg, kg, vg, rg, la = _cd_in(x2, _row(norm_mix_g[1]), w_cd, wa, _row(gla_b_a2[0]))
    o_d = _gla(seq3(qg), seq3(kg), seq3(vg), seq3(rg), seq3(la), _row(gla_out_norm[0]))
    tables = _s5_tables(s5_a_re[0], s5_a_im[0], s5_log_step[0], s5_b_re[0], s5_b_im[0], s5_c_re[0],
                        s5_c_im[0], min(S5_CHUNK, seq))
    y = _s5(seq3(u), tables)
    x2 = _s5_out_mlp(x2, y.reshape(t, S5_WIDTH), u, _row(s5_d[0]), s5_w_glu[0].astype(BF16),
                     o_d.reshape(t, GLA_WIDTH), cd_w_out[0].astype(BF16), _row(norm_mlp_g[1]),
                     w_up[1].astype(BF16), w_down[1].astype(BF16))
    return x2.reshape(bsz, seq, D_MODEL)
```

```python
import functools
import math

import jax
import jax.numpy as jnp
from jax import lax
from jax.experimental import pallas as pl
from jax.experimental.pallas import tpu as pltpu

F32 = jnp.float32
BF16 = jnp.bfloat16

D_MODEL = 1024
CHUNK = 64
RMS_EPS = 1e-6
ROPE_BASE = 10000.0
DA_HEADS = 4
DA_QK_DIM = 64
DA_V_DIM = 128
DA_WIDTH = DA_HEADS * DA_V_DIM
RET_HEADS = 4
RET_QK_DIM = 64
RET_V_DIM = 128
RET_WIDTH = RET_HEADS * RET_V_DIM
S5_WIDTH = 256
S5_GROUP = 16
S5_GROUPS = S5_WIDTH // S5_GROUP
S5_STATE = 64
GLA_HEADS = 6
GLA_QK_DIM = 64
GLA_V_DIM = 128
GLA_WIDTH = GLA_HEADS * GLA_V_DIM
GLA_QK_WIDTH = GLA_HEADS * GLA_QK_DIM
GLA_GATE_RANK = 16
GLA_TAU = 16.0
MLP_HIDDEN = 4 * D_MODEL
AB_IN = 3072
CD_IN = 2576

LANES_V7X = 128
VMEM_LIMIT_BYTES_V7X = 56 * 1024 * 1024

CD_IN_PADDED = 2688
NEG_BIG = -1e30

TOKEN_TILE = 512
ATTN_BLOCK = 512
RET_BLOCK = 512
GLA_BLOCK = 256
S5_CHUNK = LANES_V7X
MLP_HIDDEN_TILE = 1024


def _params(*semantics):
    return pltpu.CompilerParams(dimension_semantics=semantics, vmem_limit_bytes=VMEM_LIMIT_BYTES_V7X)


def _const_spec(shape):
    zeros = (0,) * len(shape)
    return pl.BlockSpec(shape, lambda *_: zeros, pipeline_mode=pl.Buffered(1))


def _rms(xf, gain):
    return xf * lax.rsqrt(jnp.mean(xf * xf, axis=-1, keepdims=True) + RMS_EPS) * gain


def _dot(a, b):
    return jnp.dot(a, b, preferred_element_type=F32)


def _dot_nt(a, b):
    return lax.dot_general(a, b, (((1,), (1,)), ((), ())), preferred_element_type=F32)


def _dot_tn(a, b):
    return lax.dot_general(a, b, (((0,), (0,)), ((), ())), preferred_element_type=F32)


def _swap_halves(x, group):
    n = x.shape[-1]
    half = group // 2
    lane = lax.broadcasted_iota(jnp.int32, x.shape, x.ndim - 1)
    from_right = pltpu.roll(x, n - half, axis=x.ndim - 1)
    from_left = pltpu.roll(x, half, axis=x.ndim - 1)
    return jnp.where((lane % group) < half, from_right, from_left)


def _ab_in_kernel(x_ref, g_ref, w_ref, wvt_ref, gq_ref, gk_ref, gsum_ref, cos_ref, sin_ref,
                  qa_ref, ka_ref, va_ref, qr_ref, kr_ref, vr_ref, gr_ref):
    h = _rms(x_ref[...], g_ref[...]).astype(BF16)

    def proj(lo, hi):
        return _dot(h, w_ref[:, lo:hi])

    def qk_norm(y, gain):
        ms = _dot((y * y).astype(BF16), gsum_ref[...])
        return (y * lax.rsqrt(ms + RMS_EPS) * gain).astype(BF16)

    def rotary(y):
        return y * cos_ref[...] + _swap_halves(y, RET_QK_DIM) * sin_ref[...]

    qa_ref[...] = qk_norm(proj(0, 512), gq_ref[...])
    ka_ref[...] = qk_norm(proj(512, 1024), gk_ref[...])
    va_ref[0] = _dot_nt(wvt_ref[...], h).astype(BF16)
    qr_ref[...] = rotary(proj(1536, 1792))
    kr_ref[...] = rotary(proj(1792, 2048)) * (RET_QK_DIM ** -0.5)
    vr_ref[...] = proj(2048, 2560).astype(BF16)
    gr_ref[...] = proj(2560, 3072)


def _ab_in(x2, g, w, wv_t, gq, gk, gsum, cos_t, sin_t, seq):
    t = x2.shape[0]
    tm = min(TOKEN_TILE, seq)
    per_seq = seq // tm
    row = lambda width: pl.BlockSpec((tm, width), lambda i: (i, 0))
    va_spec = pl.BlockSpec((1, DA_WIDTH, tm), lambda i: (i // per_seq, 0, i % per_seq))
    out_shapes = (
        jax.ShapeDtypeStruct((t, 512), BF16), jax.ShapeDtypeStruct((t, 512), BF16),
        jax.ShapeDtypeStruct((t // seq, DA_WIDTH, seq), BF16), jax.ShapeDtypeStruct((t, 256), F32),
        jax.ShapeDtypeStruct((t, 256), F32), jax.ShapeDtypeStruct((t, 512), BF16),
        jax.ShapeDtypeStruct((t, 512), F32))
    return pl.pallas_call(
        _ab_in_kernel,
        grid=(t // tm,),
        in_specs=[row(D_MODEL), _const_spec((1, D_MODEL)), _const_spec((D_MODEL, AB_IN)),
                  _const_spec((DA_WIDTH, D_MODEL)),
                  _const_spec((1, 512)), _const_spec((1, 512)), _const_spec((512, 512)),
                  row(256), row(256)],
        out_specs=(row(512), row(512), va_spec, row(256), row(256), row(512), row(512)),
        out_shape=out_shapes,
        compiler_params=_params("parallel"),
        name="ab_in_proj",
    )(x2, g, w, wv_t, gq, gk, gsum, cos_t, sin_t)


def _diff_attn_kernel(lq1_ref, lk1_ref, lq2_ref, lk2_ref, gout_ref, q_ref, k_ref, vt_ref, o_ref,
                      sa_ref, sb_ref, mxa_ref, mxb_ref, m_ref, l_ref, acc_ref, *, blk, lam_init):
    i = pl.program_id(2)
    q = q_ref[0]
    lane = lax.broadcasted_iota(jnp.int32, q.shape, 1)
    zero = jnp.zeros_like(q)
    qz = jnp.concatenate([jnp.where(lane < DA_QK_DIM, q, zero), jnp.where(lane >= DA_QK_DIM, q, zero)], axis=0)

    def score(t, s_ref, mx_ref):
        start = pl.multiple_of(t * blk, blk)
        st = _dot_nt(k_ref[0, pl.ds(start, blk), :], qz)
        s_ref[...] = st
        mx_ref[...] = jnp.max(st, axis=0, keepdims=True)

    def absorb(t, s_ref, mx_ref, diagonal):
        start = pl.multiple_of(t * blk, blk)
        st = s_ref[...]
        if diagonal:
            key = lax.broadcasted_iota(jnp.int32, st.shape, 0)
            qry = lax.broadcasted_iota(jnp.int32, st.shape, 1)
            st = jnp.where(key // CHUNK <= (qry % blk) // CHUNK, st, NEG_BIG)
            mx = jnp.max(st, axis=0, keepdims=True)
        else:
            mx = mx_ref[...]
        m = m_ref[...]
        m_new = jnp.maximum(m, mx)
        alpha = jnp.exp2(m - m_new)
        p = jnp.exp2(st - m_new)
        l_ref[...] = alpha * l_ref[...] + jnp.sum(p, axis=0, keepdims=True)
        acc_ref[...] = alpha * acc_ref[...] + _dot(vt_ref[0, :, pl.ds(start, blk)], p.astype(BF16))
        m_ref[...] = m_new

    m_ref[...] = jnp.full(m_ref.shape, NEG_BIG, F32)
    l_ref[...] = jnp.zeros(l_ref.shape, F32)
    acc_ref[...] = jnp.zeros(acc_ref.shape, F32)
    score(0, sa_ref, mxa_ref)

    def body(u, carry):
        t = 2 * u
        score(t + 1, sb_ref, mxb_ref)
        absorb(t, sa_ref, mxa_ref, False)
        score(t + 2, sa_ref, mxa_ref)
        absorb(t + 1, sb_ref, mxb_ref, False)
        return carry

    lax.fori_loop(0, lax.shift_right_logical(i, 1), body, 0)

    @pl.when((i & 1) == 0)
    def _():
        absorb(i, sa_ref, mxa_ref, True)

    @pl.when((i & 1) == 1)
    def _():
        score(i, sb_ref, mxb_ref)
        absorb(i - 1, sa_ref, mxa_ref, False)
        absorb(i, sb_ref, mxb_ref, True)

    l = l_ref[...]
    acc = acc_ref[...]
    lam =(jnp.exp(jnp.sum(lq1_ref[...] * lk1_ref[...], axis=-1, keepdims=True))
           - jnp.exp(jnp.sum(lq2_ref[...] * lk2_ref[...], axis=-1, keepdims=True)) + lam_init)
    o = acc / l
    o = o[:, :blk] - lam * o[:, blk:]
    o = o * lax.rsqrt(jnp.mean(o * o, axis=0, keepdims=True) + RMS_EPS)
    o_ref[0] = (o.T * (gout_ref[...] * (1.0 - lam_init))).astype(BF16)


def _diff_attn(qa, ka, va_t, lq1, lk1, lq2, lk2, gout, lam_init):
    b, s, _ = qa.shape
    blk = min(ATTN_BLOCK, s)
    vec = _const_spec((1, DA_QK_DIM))
    q_spec = pl.BlockSpec((1, blk, 128), lambda bi, h, i: (bi, i, h))
    k_spec = pl.BlockSpec((1, s, 128), lambda bi, h, i: (bi, 0, h))
    vt_spec = pl.BlockSpec((1, DA_V_DIM, s), lambda bi, h, i: (bi, h, 0))
    return pl.pallas_call(
        functools.partial(_diff_attn_kernel, blk=blk, lam_init=lam_init),
        grid=(b, DA_HEADS, s // blk),
        in_specs=[vec, vec, vec, vec, _const_spec((1, DA_V_DIM)), q_spec, k_spec, vt_spec],
        out_specs=q_spec,
        out_shape=jax.ShapeDtypeStruct((b, s, DA_WIDTH), BF16),
        scratch_shapes=[pltpu.VMEM((blk, 2 * blk), F32), pltpu.VMEM((blk, 2 * blk), F32),
                        pltpu.VMEM((1, 2 * blk), F32), pltpu.VMEM((1, 2 * blk), F32),
                        pltpu.VMEM((1, 2 * blk), F32), pltpu.VMEM((1, 2 * blk), F32),
                        pltpu.VMEM((DA_V_DIM, 2 * blk), F32)],
        compiler_params=_params("parallel", "parallel", "arbitrary"),
        name="diff_attention",
    )(lq1, lk1, lq2, lk2, gout, qa, ka, va_t)


def _retention_kernel(dec_ref, qdec_ref, kdec_ref, sdec_ref, smask_ref, gain_ref,
                      q_ref, k_ref, v_ref, g_ref, o_ref, state_ref):
    @pl.when(pl.program_id(1) == 0)
    def _():
        state_ref[...] = jnp.zeros_like(state_ref)

    q = q_ref[0]
    k = k_ref[0]
    v = v_ref[0]
    kb = k.astype(BF16)
    lane = lax.broadcasted_iota(jnp.int32, q.shape, 1)
    state = state_ref[...]
    o_cross = _dot((q * qdec_ref[...]).astype(BF16), state.astype(BF16))
    for h in range(RET_HEADS):
        qh = jnp.where(lane // RET_QK_DIM == h, q, 0.0).astype(BF16)
        w = (_dot_nt(qh, kb) * dec_ref[h]).astype(BF16)
        vs = slice(h * RET_V_DIM, (h + 1) * RET_V_DIM)
        o = _dot(w, v[:, vs]) + o_cross[:, vs]
        o = _rms(o, gain_ref[...])
        g = g_ref[0, :, vs]
        o_ref[0, :, vs] = (o * (g * jax.nn.sigmoid(g))).astype(BF16)
    kv = _dot_tn((k * kdec_ref[...]).astype(BF16), v)
    state_ref[...] = sdec_ref[...] * state + kv * smask_ref[...]


def _retention_tables(blk):
    heads = jnp.arange(RET_HEADS, dtype=F32)
    log_gamma = jnp.log(1.0 - 2.0 ** (-5.0 - heads))
    idx = jnp.arange(blk)
    diff = (idx[:, None] - idx[None, :]).astype(F32)
    same_chunk = (idx[:, None] // CHUNK) == (idx[None, :] // CHUNK)
    visible = (idx[None, :] <= idx[:, None]) | same_chunk
    dec = jnp.where(visible[None], jnp.exp(log_gamma[:, None, None] * jnp.abs(diff)[None]), 0.0)
    per_lane = jnp.repeat(log_gamma, RET_QK_DIM)
    pos = jnp.arange(blk, dtype=F32)
    qdec = jnp.exp((pos[:, None] + 1.0) * per_lane[None, :])
    kdec = jnp.exp((blk - 1.0 - pos)[:, None] * per_lane[None, :])
    sdec = jnp.exp(blk * per_lane)[:, None]
    smask = (jnp.arange(RET_HEADS * RET_QK_DIM)[:, None] // RET_QK_DIM
             == jnp.arange(RET_WIDTH)[None, :] // RET_V_DIM).astype(F32)
    return dec, qdec, kdec, sdec, smask


def _retention(qr, kr, vr, gr, gain):
    b, s, _ = qr.shape
    blk = min(RET_BLOCK, s)
    dec, qdec, kdec, sdec, smask = _retention_tables(blk)
    tok = lambda width: pl.BlockSpec((1, blk, width), lambda bi, i: (bi, i, 0))
    return pl.pallas_call(
        _retention_kernel,
        grid=(b, s // blk),
        in_specs=[_const_spec(dec.shape), _const_spec(qdec.shape), _const_spec(kdec.shape),
                  _const_spec(sdec.shape), _const_spec(smask.shape), _const_spec((1, RET_V_DIM)),
                  tok(256), tok(256), tok(512), tok(512)],
        out_specs=tok(RET_WIDTH),
        out_shape=jax.ShapeDtypeStruct((b, s, RET_WIDTH), BF16),
        scratch_shapes=[pltpu.VMEM((RET_HEADS * RET_QK_DIM, RET_WIDTH), F32)],
        compiler_params=_params("parallel", "arbitrary"),
        name="retention",
    )(dec, qdec, kdec, sdec, smask, gain, qr, kr, vr, gr)


def _mlp_tail(x1, g_ref, wup_ref, wdn_ref, o_ref):
    h = _rms(x1, g_ref[...]).astype(BF16)
    acc = x1
    for c in range(MLP_HIDDEN // MLP_HIDDEN_TILE):
        cs = slice(c * MLP_HIDDEN_TILE, (c + 1) * MLP_HIDDEN_TILE)
        u = jnp.maximum(_dot(h, wup_ref[:, cs]), 0.0)
        acc = acc + _dot((u * u).astype(BF16), wdn_ref[cs, :])
    o_ref[...] = acc


def _out_mlp_kernel(x_ref, a_ref, b_ref, wout_ref, g_ref, wup_ref, wdn_ref, o_ref):
    ka = a_ref.shape[1]
    mixed = _dot(a_ref[...], wout_ref[:ka, :]) + _dot(b_ref[...], wout_ref[ka:, :])
    _mlp_tail(x_ref[...] + mixed, g_ref, wup_ref, wdn_ref, o_ref)


def _s5_out_mlp_kernel(x_ref, yt_ref, ut_ref, d_ref, wglut_ref, b_ref, wout_ref, g_ref, wup_ref, wdn_ref, o_ref):
    y = yt_ref[...] + d_ref[...] * ut_ref[...]
    z = jax.nn.gelu(y)
    a_t = (z * jax.nn.sigmoid(_dot(wglut_ref[...], z.astype(BF16)))).astype(BF16)
    mixed = _dot_tn(a_t, wout_ref[:S5_WIDTH, :]) + _dot(b_ref[...], wout_ref[S5_WIDTH:, :])
    _mlp_tail(x_ref[...] + mixed, g_ref, wup_ref, wdn_ref, o_ref)


def _mlp_specs():
    return [_const_spec((D_MODEL, D_MODEL)), _const_spec((1, D_MODEL)),
            _const_spec((D_MODEL, MLP_HIDDEN)), _const_spec((MLP_HIDDEN, D_MODEL))]


def _out_mlp(x2, a, bb, wout, g, wup, wdn):
    t = x2.shape[0]
    tm = TOKEN_TILE
    row = lambda width: pl.BlockSpec((tm, width), lambda i: (i, 0))
    return pl.pallas_call(
        _out_mlp_kernel,
        grid=(t // tm,),
        in_specs=[row(D_MODEL), row(a.shape[1]), row(bb.shape[1])] + _mlp_specs(),
        out_specs=row(D_MODEL),
        out_shape=jax.ShapeDtypeStruct((t, D_MODEL), F32),
        compiler_params=_params("parallel"),
        name="ab_out_mlp",
    )(x2, a, bb, wout, g, wup, wdn)


def _s5_out_mlp(x2, y_t, u_t, d_col, wglu_t, bb, wout, g, wup, wdn):
    t = x2.shape[0]
    tm = TOKEN_TILE
    row = lambda width: pl.BlockSpec((tm, width), lambda i: (i, 0))
    col = pl.BlockSpec((S5_WIDTH, tm), lambda i: (0, i))
    return pl.pallas_call(
        _s5_out_mlp_kernel,
        grid=(t // tm,),
        in_specs=[row(D_MODEL), col, col, _const_spec((S5_WIDTH, 1)),
                  _const_spec((S5_WIDTH, S5_WIDTH)), row(GLA_WIDTH)] + _mlp_specs(),
        out_specs=row(D_MODEL),
        out_shape=jax.ShapeDtypeStruct((t, D_MODEL), F32),
        compiler_params=_params("parallel"),
        name="cd_out_mlp",
    )(x2, y_t, u_t, d_col, wglu_t, bb, wout, g, wup, wdn)


def _cd_in_kernel(x_ref, g_ref, w_ref, wut_ref, wa_ref, ba_ref, ut_ref, q_ref, k_ref, v_ref, r_ref, la_ref):
    h = _rms(x_ref[...], g_ref[...]).astype(BF16)

    def proj(lo, hi):
        return _dot(h, w_ref[:, lo:hi])

    ut_ref[...] = _dot_nt(wut_ref[...], h)
    q_ref[...] = proj(256, 640) * (GLA_QK_DIM ** -0.5)
    k_ref[...] = proj(640, 1024)
    v_ref[...] = proj(1024, 1792).astype(BF16)
    r_ref[...] = proj(1792, 2560)
    a_lr = proj(2560, CD_IN_PADDED)
    pre = _dot(a_lr.astype(BF16), wa_ref[...]) + ba_ref[...]
    log_sig = jnp.minimum(pre, 0.0) - jnp.log1p(jnp.exp(-jnp.abs(pre)))
    la_ref[...] = log_sig / GLA_TAU


def _cd_in(x2, g, w, wu_t, wa, ba):
    t = x2.shape[0]
    tm = TOKEN_TILE
    row = lambda width: pl.BlockSpec((tm, width), lambda i: (i, 0))
    out_shapes = (
        jax.ShapeDtypeStruct((S5_WIDTH, t), F32), jax.ShapeDtypeStruct((t, GLA_QK_WIDTH), F32),
        jax.ShapeDtypeStruct((t, GLA_QK_WIDTH), F32), jax.ShapeDtypeStruct((t, GLA_WIDTH), BF16),
        jax.ShapeDtypeStruct((t, GLA_WIDTH), F32), jax.ShapeDtypeStruct((t, GLA_QK_WIDTH), F32))
    return pl.pallas_call(
        _cd_in_kernel,
        grid=(t // tm,),
        in_specs=[row(D_MODEL), _const_spec((1, D_MODEL)), _const_spec((D_MODEL, CD_IN_PADDED)),
                  _const_spec((S5_WIDTH, D_MODEL)),
                  _const_spec((LANES_V7X, GLA_QK_WIDTH)), _const_spec((1, GLA_QK_WIDTH))],
        out_specs=(pl.BlockSpec((S5_WIDTH, tm), lambda i: (0, i)), row(GLA_QK_WIDTH), row(GLA_QK_WIDTH),
                   row(GLA_WIDTH), row(GLA_WIDTH), row(GLA_QK_WIDTH)),
        out_shape=out_shapes,
        compiler_params=_params("parallel"),
        name="cd_in_proj",
    )(x2, g, w, wu_t, wa, ba)


def _split3(x):
    hi = x.astype(BF16)
    r = x - hi.astype(F32)
    mid = r.astype(BF16)
    lo = (r - mid.astype(F32)).astype(BF16)
    return hi, mid, lo


def _gla_kernel(tri_ref, gain_ref, q_ref, k_ref, v_ref, r_ref, la_ref, o_ref, state_ref, *, blk):
    @pl.when(pl.program_id(1) == 0)
    def _():
        state_ref[...] = jnp.zeros_like(state_ref)

    nc = blk // CHUNK
    q = q_ref[0]
    k = k_ref[0]
    tri = tri_ref[...]
    hi, mid, lo = _split3(la_ref[0])
    b = _dot(tri, hi) + _dot(tri, mid) + _dot(tri, lo)
    b3 = b.reshape(nc, CHUNK, GLA_QK_WIDTH)
    b_last = b3[:, CHUNK - 1:CHUNK, :]
    e_pos = jnp.exp(b)
    e_neg = jnp.exp(-b)
    k_tail = jnp.exp(b_last - b3).reshape(blk, GLA_QK_WIDTH)
    chunk_decay = jnp.exp(b_last)
    qp = (q * e_pos).astype(BF16)
    qn = (q * e_neg).astype(BF16)
    kp = (k * e_pos).astype(BF16)
    kn = (k * e_neg).astype(BF16)
    kw = (k * k_tail).astype(BF16)

    row = lax.broadcasted_iota(jnp.int32, (blk, blk), 0)
    col = lax.broadcasted_iota(jnp.int32, (blk, blk), 1)
    same_chunk = row // CHUNK == col // CHUNK
    causal = row >= col
    lane = lax.broadcasted_iota(jnp.int32, (blk, 2 * GLA_QK_DIM), 1)
    smask = (lax.broadcasted_iota(jnp.int32, (2 * GLA_V_DIM, 2 * GLA_QK_DIM), 0) // GLA_V_DIM
             == lax.broadcasted_iota(jnp.int32, (2 * GLA_V_DIM, 2 * GLA_QK_DIM), 1) // GLA_QK_DIM)

    for p in range(GLA_HEADS // 2):
        ks = slice(p * 2 * GLA_QK_DIM, (p + 1) * 2 * GLA_QK_DIM)
        vs = slice(p * 2 * GLA_V_DIM, (p + 1) * 2 * GLA_V_DIM)
        v = v_ref[0, :, vs]
        intra = []
        for half in range(2):
            sel = (lane // GLA_QK_DIM) == half
            fwd = _dot_nt(jnp.where(sel, qp[:, ks], 0).astype(BF16), kn[:, ks])
            bwd = _dot_nt(jnp.where(sel, qn[:, ks], 0).astype(BF16), kp[:, ks])
            sc = jnp.where(same_chunk, jnp.where(causal, fwd, bwd), 0.0).astype(BF16)
            intra.append(_dot(sc, v[:, half * GLA_V_DIM:(half + 1) * GLA_V_DIM]))
        o_intra = jnp.concatenate(intra, axis=1)

        state = state_ref[p]
        cross = []
        for c in range(nc):
            rs = slice(c * CHUNK, (c + 1) * CHUNK)
            cross.append(_dot_nt(qp[rs, ks], state.astype(BF16)))
            kv_t = _dot_tn(v[rs, :], kw[rs, ks])
            state = state * chunk_decay[c][:, ks] + jnp.where(smask, kv_t, 0.0)
        state_ref[p] = state
        o = o_intra + jnp.concatenate(cross, axis=0)
        for half in range(2):
            hs = slice(half * GLA_V_DIM, (half + 1) * GLA_V_DIM)
            os_ = slice(p * 2 * GLA_V_DIM + half * GLA_V_DIM, p * 2 * GLA_V_DIM + (half + 1) * GLA_V_DIM)
            g = r_ref[0, :, os_]
            o_ref[0, :, os_] = (_rms(o[:, hs], gain_ref[...]) * (g * jax.nn.sigmoid(g))).astype(BF16)


def _gla(qg, kg, vg, rg, la, gain):
    b, s, _ = qg.shape
    blk = min(GLA_BLOCK, s)
    idx = jnp.arange(blk)
    tri = (((idx[:, None] // CHUNK) == (idx[None, :] // CHUNK)) & (idx[None, :] <= idx[:, None])).astype(BF16)
    tok = lambda width: pl.BlockSpec((1, blk, width), lambda bi, i: (bi, i, 0))
    return pl.pallas_call(
        functools.partial(_gla_kernel, blk=blk),
        grid=(b, s // blk),
        in_specs=[_const_spec((blk, blk)), _const_spec((1, GLA_V_DIM)),
                  tok(GLA_QK_WIDTH), tok(GLA_QK_WIDTH), tok(GLA_WIDTH), tok(GLA_WIDTH), tok(GLA_QK_WIDTH)],
        out_specs=tok(GLA_WIDTH),
        out_shape=jax.ShapeDtypeStruct((b, s, GLA_WIDTH), BF16),
        scratch_shapes=[pltpu.VMEM((GLA_HEADS // 2, 2 * GLA_V_DIM, 2 * GLA_QK_DIM), F32)],
        compiler_params=_params("parallel", "arbitrary"),
        name="gla",
    )(tri, gain, qg, kg, vg, rg, la)


def _s5_kernel(u_ref, kern_ref, fre_ref, fim_ref, ere_ref, eim_ref, lre_ref, lim_ref, y_ref,
               toep_ref, vre_ref, vim_ref, hre_ref, him_ref, *, batch, chunks):
    length = S5_CHUNK
    src = lax.broadcasted_iota(jnp.int32, (length, length), 0)
    dst = lax.broadcasted_iota(jnp.int32, (length, length), 1)

    def build(ci, carry):
        for co in range(S5_GROUP):
            lags = kern_ref[0, pl.ds(ci * S5_GROUP + co, 1), :]
            blk = pltpu.roll(jnp.broadcast_to(lags, (length, length)), 0, 1, stride=1, stride_axis=0)
            blk = jnp.where(dst >= src, blk, 0.0)
            toep_ref[pl.ds(pl.multiple_of(ci * length, length), length), co * length:(co + 1) * length] = (
                blk.astype(BF16))
        return carry

    lax.fori_loop(0, S5_GROUP, build, 0)

    u = jnp.concatenate([u_ref[0, c] for c in range(S5_GROUP)], axis=1).astype(BF16)
    vre_ref[...] = _dot(u, fre_ref[0])
    vim_ref[...] = _dot(u, fim_ref[0])
    lre = lre_ref[0]
    lim = lim_ref[0]

    def step(n, carry):
        new = []
        for bi in range(batch):
            hr, hi = carry[2 * bi], carry[2 * bi + 1]
            r = bi * chunks + n
            hre_ref[pl.ds(r, 1), :] = hr
            him_ref[pl.ds(r, 1), :] = hi
            vr = vre_ref[pl.ds(r, 1), :]
            vi = vim_ref[pl.ds(r, 1), :]
            new += [lre * hr - lim * hi + vr, lre * hi + lim * hr + vi]
        return tuple(new)

    zero = jnp.zeros((1, LANES_V7X), F32)
    lax.fori_loop(0, chunks, step, (zero,) * (2 * batch))
    y = (_dot(u, toep_ref[...]) + _dot(hre_ref[...].astype(BF16), ere_ref[0])
         + _dot(him_ref[...].astype(BF16), eim_ref[0]))
    for c in range(S5_GROUP):
        y_ref[0, c] = y[:, c * length:(c + 1) * length]


def _s5_tables(a_re, a_im, log_step, b_re, b_im, c_re, c_im):
    hp = lax.Precision.HIGHEST
    g, p, c, length = S5_GROUPS, S5_STATE, S5_GROUP, S5_CHUNK
    a_re, a_im = a_re.astype(F32), a_im.astype(F32)
    delta = jnp.exp(log_step.astype(F32))[:, None]
    tau = jnp.arange(length + 1, dtype=F32)[None, :, None]
    mag = jnp.exp((a_re * delta)[:, None, :] * tau)
    ang = (a_im * delta)[:, None, :] * tau
    pw_re, pw_im = mag * jnp.cos(ang), mag * jnp.sin(ang)
    n_re, n_im = pw_re[:, 1, :] - 1.0, pw_im[:, 1, :]
    den = a_re * a_re + a_im * a_im
    q_re, q_im = (n_re * a_re + n_im * a_im) / den, (n_im * a_re - n_re * a_im) / den
    bb_re = q_re[:, :, None] * b_re - q_im[:, :, None] * b_im
    bb_im = q_re[:, :, None] * b_im + q_im[:, :, None] * b_re
    first_re, first_im = pw_re[:, :length, None, :], pw_im[:, :length, None, :]
    cp_re = c_re[:, None] * first_re - c_im[:, None] * first_im
    cp_im = c_re[:, None] * first_im + c_im[:, None] * first_re
    kern = (jnp.einsum('gtcp,gpd->gdct', cp_re, bb_re, precision=hp)
            - jnp.einsum('gtcp,gpd->gdct', cp_im, bb_im, precision=hp)).reshape(g, c * c, length)
    rev_re, rev_im = pw_re[:, length - 1::-1, :][:, None], pw_im[:, length - 1::-1, :][:, None]
    bt_re, bt_im = bb_re.transpose(0, 2, 1)[:, :, None, :], bb_im.transpose(0, 2, 1)[:, :, None, :]
    f_re = (rev_re * bt_re - rev_im * bt_im).reshape(g, c * length, p)
    f_im = (rev_re * bt_im + rev_im * bt_re).reshape(g, c * length, p)
    nx_re = pw_re[:, 1:, :].transpose(0, 2, 1)[:, :, None, :]
    nx_im = pw_im[:, 1:, :].transpose(0, 2, 1)[:, :, None, :]
    ct_re, ct_im = c_re.transpose(0, 2, 1)[:, :, :, None], c_im.transpose(0, 2, 1)[:, :, :, None]
    e_re = (ct_re * nx_re - ct_im * nx_im).reshape(g, p, c * length)
    e_im = (ct_re * nx_im + ct_im * nx_re).reshape(g, p, c * length)
    pad = lambda z: jnp.pad(z, ((0, 0), (0, 0), (0, LANES_V7X - p)))
    pad_rows = lambda z: jnp.pad(z, ((0, 0), (0, LANES_V7X - p), (0, 0)))
    return (kern, pad(f_re).astype(BF16), pad(f_im).astype(BF16), pad_rows(e_re).astype(BF16),
            pad_rows(-e_im).astype(BF16), pad(pw_re[:, length:, :]), pad(pw_im[:, length:, :]))


def _s5(u_t, tables, batch):
    t = u_t.shape[1]
    length = S5_CHUNK
    rows = t // length
    width = length * S5_GROUP
    kern, fre, fim, ere, eim, lre, lim = tables
    grp = lambda r, c: pl.BlockSpec((1, r, c), lambda gi: (gi, 0, 0))
    io_spec = pl.BlockSpec((1, S5_GROUP, rows, length), lambda gi: (gi, 0, 0, 0))
    y = pl.pallas_call(
        functools.partial(_s5_kernel, batch=batch, chunks=rows // batch),
        grid=(S5_GROUPS,),
        in_specs=[io_spec, grp(S5_GROUP * S5_GROUP, length), grp(width, LANES_V7X), grp(width, LANES_V7X),
                  grp(LANES_V7X, width), grp(LANES_V7X, width), grp(1, LANES_V7X), grp(1, LANES_V7X)],
        out_specs=io_spec,
        out_shape=jax.ShapeDtypeStruct((S5_GROUPS, S5_GROUP, rows, length), F32),
        scratch_shapes=[pltpu.VMEM((width, width), BF16)] + [pltpu.VMEM((rows, LANES_V7X), F32)] * 4,
        compiler_params=_params("parallel"),
        name="s5",
    )(u_t.reshape(S5_GROUPS, S5_GROUP, rows, length), kern, fre, fim, ere, eim, lre, lim)
    return y.reshape(S5_WIDTH, t)


def _rotary_tables(seq):
    half = RET_QK_DIM // 2
    inv_freq = 1.0 / (ROPE_BASE ** jnp.linspace(0.0, 1.0, half, dtype=F32))
    ang = jnp.arange(seq, dtype=F32)[:, None] * inv_freq[None, :]
    cos, sin = jnp.cos(ang), jnp.sin(ang)
    cos_t = jnp.tile(jnp.concatenate([cos, cos], axis=1), (1, RET_HEADS))
    sin_t = jnp.tile(jnp.concatenate([-sin, sin], axis=1), (1, RET_HEADS))
    return cos_t, sin_t


def _row(v):
    return v.reshape(1, -1).astype(F32)


def kernel(x, norm_mix_g, norm_mlp_g, w_up, w_down, ab_w_in, ab_w_out, da_q_norm, da_k_norm,
           da_lam_q1, da_lam_k1, da_lam_q2, da_lam_k2, da_out_norm, ret_out_norm, cd_w_in, cd_w_out,
           s5_a_re, s5_a_im, s5_log_step, s5_b_re, s5_b_im, s5_c_re, s5_c_im, s5_d, s5_w_glu,
           gla_w_a2, gla_b_a2, gla_out_norm):
    bsz, seq, _ = x.shape
    t = bsz * seq
    x2 = x.reshape(t, D_MODEL)

    cos_t, sin_t = _rotary_tables(seq)
    cos_t = jnp.tile(cos_t, (bsz, 1))
    sin_t = jnp.tile(sin_t, (bsz, 1))
    gsum = ((jnp.arange(512)[:, None] // DA_QK_DIM) == (jnp.arange(512)[None, :] // DA_QK_DIM))
    gsum = (gsum.astype(F32) / DA_QK_DIM).astype(BF16)
    gq = _row(jnp.tile(da_q_norm[0], 2 * DA_HEADS)) * (DA_QK_DIM ** -0.5 * math.log2(math.e))
    gk = _row(jnp.tile(da_k_norm[0], 2 * DA_HEADS))
    w_ab = ab_w_in[0].astype(BF16)
    qa, ka, va_t, qr, kr, vr, gr = _ab_in(x2, _row(norm_mix_g[0]), w_ab, w_ab[:, 1024:1536].T, gq, gk, gsum,
                                          cos_t, sin_t, seq)
    lam_init = 0.8 - 0.6 * math.exp(-0.3 * 0)
    seq3 = lambda a: a.reshape(bsz, seq, a.shape[-1])
    o_a = _diff_attn(seq3(qa), seq3(ka), va_t, _row(da_lam_q1[0]), _row(da_lam_k1[0]),
                     _row(da_lam_q2[0]), _row(da_lam_k2[0]), _row(da_out_norm[0]), lam_init)
    o_r = _retention(seq3(qr), seq3(kr), seq3(vr), seq3(gr), _row(ret_out_norm[0]))
    x2 = _out_mlp(x2, o_a.reshape(t, DA_WIDTH), o_r.reshape(t, RET_WIDTH), ab_w_out[0].astype(BF16),
                  _row(norm_mlp_g[0]), w_up[0].astype(BF16), w_down[0].astype(BF16))

    w_cd = jnp.pad(cd_w_in[0], ((0, 0), (0, CD_IN_PADDED - CD_IN))).astype(BF16)
    wa = jnp.pad(gla_w_a2[0], ((0, LANES_V7X - GLA_GATE_RANK), (0, 0))).astype(BF16)
    u_t, qg, kg, vg, rg, la = _cd_in(x2, _row(norm_mix_g[1]), w_cd, w_cd[:, :S5_WIDTH].T, wa, _row(gla_b_a2[0]))
    o_d = _gla(seq3(qg), seq3(kg), seq3(vg), seq3(rg), seq3(la), _row(gla_out_norm[0]))
    tables = _s5_tables(s5_a_re[0], s5_a_im[0], s5_log_step[0], s5_b_re[0], s5_b_im[0], s5_c_re[0], s5_c_im[0])
    y_t = _s5(u_t, tables, bsz)
    x2 = _s5_out_mlp(x2, y_t, u_t, s5_d[0].reshape(S5_WIDTH, 1).astype(F32), s5_w_glu[0].T.astype(BF16),
                     o_d.reshape(t, GLA_WIDTH), cd_w_out[0].astype(BF16), _row(norm_mlp_g[1]),
                     w_up[1].astype(BF16), w_down[1].astype(BF16))
    return x2.reshape(bsz, seq, D_MODEL)
```

```python
import functools
import math

import jax
import jax.numpy as jnp
from jax import lax
from jax.experimental import pallas as pl
from jax.experimental.pallas import tpu as pltpu

F32 = jnp.float32
BF16 = jnp.bfloat16

D_MODEL = 1024
CHUNK = 64
RMS_EPS = 1e-6
ROPE_BASE = 10000.0
DA_HEADS = 4
DA_QK_DIM = 64
DA_V_DIM = 128
DA_WIDTH = DA_HEADS * DA_V_DIM
RET_HEADS = 4
RET_QK_DIM = 64
RET_V_DIM = 128
RET_WIDTH = RET_HEADS * RET_V_DIM
S5_WIDTH = 256
S5_GROUP = 16
S5_GROUPS = S5_WIDTH // S5_GROUP
S5_STATE = 64
GLA_HEADS = 6
GLA_QK_DIM = 64
GLA_V_DIM = 128
GLA_WIDTH = GLA_HEADS * GLA_V_DIM
GLA_QK_WIDTH = GLA_HEADS * GLA_QK_DIM
GLA_GATE_RANK = 16
GLA_TAU = 16.0
MLP_HIDDEN = 4 * D_MODEL
AB_IN = 3072
CD_IN = 2576

LANES_V7X = 128
VMEM_LIMIT_BYTES_V7X = 56 * 1024 * 1024

CD_IN_PADDED = 2688
NEG_BIG = -1e30

TOKEN_TILE = 512
ATTN_BLOCK = 512
ATTN_TILE = 256
RET_BLOCK = 512
GLA_BLOCK = 256
S5_CHUNK = LANES_V7X
MLP_HIDDEN_TILE = 1024


def _params(*semantics):
    return pltpu.CompilerParams(dimension_semantics=semantics, vmem_limit_bytes=VMEM_LIMIT_BYTES_V7X)


def _const_spec(shape):
    zeros = (0,) * len(shape)
    return pl.BlockSpec(shape, lambda *_: zeros, pipeline_mode=pl.Buffered(1))


def _rms(xf, gain):
    return xf * lax.rsqrt(jnp.mean(xf * xf, axis=-1, keepdims=True) + RMS_EPS) * gain


def _dot(a, b):
    return jnp.dot(a, b, preferred_element_type=F32)


def _dot_nt(a, b):
    return lax.dot_general(a, b, (((1,), (1,)), ((), ())), preferred_element_type=F32)


def _dot_tn(a, b):
    return lax.dot_general(a, b, (((0,), (0,)), ((), ())), preferred_element_type=F32)


def _swap_halves(x, group):
    n = x.shape[-1]
    half = group // 2
    lane = lax.broadcasted_iota(jnp.int32, x.shape, x.ndim - 1)
    from_right = pltpu.roll(x, n - half, axis=x.ndim - 1)
    from_left = pltpu.roll(x, half, axis=x.ndim - 1)
    return jnp.where((lane % group) < half, from_right, from_left)


def _ab_in_kernel(x_ref, g_ref, w_ref, wvt_ref, gq_ref, gk_ref, gsum_ref, cos_ref, sin_ref,
                  qa_ref, ka_ref, va_ref, qr_ref, kr_ref, vr_ref, gr_ref):
    h = _rms(x_ref[...], g_ref[...]).astype(BF16)

    def proj(lo, hi):
        return _dot(h, w_ref[:, lo:hi])

    def qk_norm(y, gain):
        ms = _dot((y * y).astype(BF16), gsum_ref[...])
        return (y * lax.rsqrt(ms + RMS_EPS) * gain).astype(BF16)

    def rotary(y):
        return y * cos_ref[...] + _swap_halves(y, RET_QK_DIM) * sin_ref[...]

    qa_ref[...] = qk_norm(proj(0, 512), gq_ref[...])
    ka_ref[...] = qk_norm(proj(512, 1024), gk_ref[...])
    va_ref[0] = _dot_nt(wvt_ref[...], h).astype(BF16)
    qr_ref[...] = rotary(proj(1536, 1792))
    kr_ref[...] = rotary(proj(1792, 2048)) * (RET_QK_DIM ** -0.5)
    vr_ref[...] = proj(2048, 2560).astype(BF16)
    gr_ref[...] = proj(2560, 3072)


def _ab_in(x2, g, w, wv_t, gq, gk, gsum, cos_t, sin_t, seq):
    t = x2.shape[0]
    tm = min(TOKEN_TILE, seq)
    per_seq = seq // tm
    row = lambda width: pl.BlockSpec((tm, width), lambda i: (i, 0))
    va_spec = pl.BlockSpec((1, DA_WIDTH, tm), lambda i: (i // per_seq, 0, i % per_seq))
    out_shapes = (
        jax.ShapeDtypeStruct((t, 512), BF16), jax.ShapeDtypeStruct((t, 512), BF16),
        jax.ShapeDtypeStruct((t // seq, DA_WIDTH, seq), BF16), jax.ShapeDtypeStruct((t, 256), F32),
        jax.ShapeDtypeStruct((t, 256), F32), jax.ShapeDtypeStruct((t, 512), BF16),
        jax.ShapeDtypeStruct((t, 512), F32))
    return pl.pallas_call(
        _ab_in_kernel,
        grid=(t // tm,),
        in_specs=[row(D_MODEL), _const_spec((1, D_MODEL)), _const_spec((D_MODEL, AB_IN)),
                  _const_spec((DA_WIDTH, D_MODEL)),
                  _const_spec((1, 512)), _const_spec((1, 512)), _const_spec((512, 512)),
                  row(256), row(256)],
        out_specs=(row(512), row(512), va_spec, row(256), row(256), row(512), row(512)),
        out_shape=out_shapes,
        compiler_params=_params("parallel"),
        name="ab_in_proj",
    )(x2, g, w, wv_t, gq, gk, gsum, cos_t, sin_t)


def _diff_attn_kernel(lq1_ref, lk1_ref, lq2_ref, lk2_ref, gout_ref, q_ref, k_ref, vt_ref, o_ref,
                      sa_ref, sb_ref, mxa_ref, mxb_ref, m_ref, l_ref, acc_ref, *, blk, lam_init):
    i = pl.program_id(2)
    qt = q_ref[0].astype(F32).T
    dim = lax.broadcasted_iota(jnp.int32, qt.shape, 0)
    qz_t = jnp.concatenate([jnp.where(dim < DA_QK_DIM, qt, 0.0), jnp.where(dim >= DA_QK_DIM, qt, 0.0)],
                           axis=1).astype(BF16)

    def score(t, s_ref, mx_ref):
        start = pl.multiple_of(t * blk, blk)
        st = _dot(k_ref[0, pl.ds(start, blk), :], qz_t)
        s_ref[...] = st
        mx_ref[...] = jnp.max(st, axis=0, keepdims=True)

    def absorb(t, s_ref, mx_ref, diagonal):
        start = pl.multiple_of(t * blk, blk)
        tile = ATTN_TILE
        for qg in range(2 * blk // tile):
            cs = slice(qg * tile, (qg + 1) * tile)

            def scores(kh):
                st = s_ref[kh * tile:(kh + 1) * tile, cs]
                if diagonal:
                    key = lax.broadcasted_iota(jnp.int32, st.shape, 0) + kh * tile
                    qry = lax.broadcasted_iota(jnp.int32, st.shape, 1) + (qg * tile) % blk
                    st = jnp.where(key // CHUNK <= qry // CHUNK, st, NEG_BIG)
                return st

            if diagonal:
                mx = functools.reduce(jnp.maximum, [jnp.max(scores(kh), axis=0, keepdims=True)
                                                    for kh in range(blk // tile)])
            else:
                mx = mx_ref[:, cs]
            m = m_ref[:, cs]
            m_new = jnp.maximum(m, mx)
            alpha = jnp.exp2(m - m_new)
            l = alpha * l_ref[:, cs]
            acc = alpha * acc_ref[:, cs]
            for kh in range(blk // tile):
                p = jnp.exp2(scores(kh) - m_new)
                l = l + jnp.sum(p, axis=0, keepdims=True)
                acc = acc + _dot(vt_ref[0, :, pl.ds(start + kh * tile, tile)], p.astype(BF16))
            m_ref[:, cs] = m_new
            l_ref[:, cs] = l
            acc_ref[:, cs] = acc

    m_ref[...] = jnp.full(m_ref.shape, NEG_BIG, F32)
    l_ref[...] = jnp.zeros(l_ref.shape, F32)
    acc_ref[...] = jnp.zeros(acc_ref.shape, F32)
    score(0, sa_ref, mxa_ref)

    def body(u, carry):
        t = 2 * u
        score(t + 1, sb_ref, mxb_ref)
        absorb(t, sa_ref, mxa_ref, False)
        score(t + 2, sa_ref, mxa_ref)
        absorb(t + 1, sb_ref, mxb_ref, False)
        return carry

    lax.fori_loop(0, lax.shift_right_logical(i, 1), body, 0)

    @pl.when((i & 1) == 0)
    def _():
        absorb(i, sa_ref, mxa_ref, True)

    @pl.when((i & 1) == 1)
    def _():
        score(i, sb_ref, mxb_ref)
        absorb(i - 1, sa_ref, mxa_ref, False)
        absorb(i, sb_ref, mxb_ref, True)

    l = l_ref[...]
    acc = acc_ref[...]
    lam =(jnp.exp(jnp.sum(lq1_ref[...] * lk1_ref[...], axis=-1, keepdims=True))
           - jnp.exp(jnp.sum(lq2_ref[...] * lk2_ref[...], axis=-1, keepdims=True)) + lam_init)
    o = acc / l
    o = o[:, :blk] - lam * o[:, blk:]
    o = o * lax.rsqrt(jnp.mean(o * o, axis=0, keepdims=True) + RMS_EPS)
    o_ref[0] = (o.T * (gout_ref[...] * (1.0 - lam_init))).astype(BF16)


def _diff_attn(qa, ka, va_t, lq1, lk1, lq2, lk2, gout, lam_init):
    b, s, _ = qa.shape
    blk = min(ATTN_BLOCK, s)
    vec = _const_spec((1, DA_QK_DIM))
    q_spec = pl.BlockSpec((1, blk, 128), lambda bi, h, i: (bi, i, h))
    k_spec = pl.BlockSpec((1, s, 128), lambda bi, h, i: (bi, 0, h))
    vt_spec = pl.BlockSpec((1, DA_V_DIM, s), lambda bi, h, i: (bi, h, 0))
    return pl.pallas_call(
        functools.partial(_diff_attn_kernel, blk=blk, lam_init=lam_init),
        grid=(b, DA_HEADS, s // blk),
        in_specs=[vec, vec, vec, vec, _const_spec((1, DA_V_DIM)), q_spec, k_spec, vt_spec],
        out_specs=q_spec,
        out_shape=jax.ShapeDtypeStruct((b, s, DA_WIDTH), BF16),
        scratch_shapes=[pltpu.VMEM((blk, 2 * blk), F32), pltpu.VMEM((blk, 2 * blk), F32),
                        pltpu.VMEM((1, 2 * blk), F32), pltpu.VMEM((1, 2 * blk), F32),
                        pltpu.VMEM((1, 2 * blk), F32), pltpu.VMEM((1, 2 * blk), F32),
                        pltpu.VMEM((DA_V_DIM, 2 * blk), F32)],
        compiler_params=_params("parallel", "parallel", "arbitrary"),
        name="diff_attention",
    )(lq1, lk1, lq2, lk2, gout, qa, ka, va_t)


def _retention_kernel(dec_ref, qdec_ref, kdec_ref, sdec_ref, smask_ref, gain_ref,
                      q_ref, k_ref, v_ref, g_ref, o_ref, state_ref):
    @pl.when(pl.program_id(1) == 0)
    def _():
        state_ref[...] = jnp.zeros_like(state_ref)

    q = q_ref[0]
    k = k_ref[0]
    v = v_ref[0]
    kb = k.astype(BF16)
    lane = lax.broadcasted_iota(jnp.int32, q.shape, 1)
    state = state_ref[...]
    o_cross = _dot((q * qdec_ref[...]).astype(BF16), state.astype(BF16))
    for h in range(RET_HEADS):
        qh = jnp.where(lane // RET_QK_DIM == h, q, 0.0).astype(BF16)
        w = (_dot_nt(qh, kb) * dec_ref[h]).astype(BF16)
        vs = slice(h * RET_V_DIM, (h + 1) * RET_V_DIM)
        o = _dot(w, v[:, vs]) + o_cross[:, vs]
        o = _rms(o, gain_ref[...])
        g = g_ref[0, :, vs]
        o_ref[0, :, vs] = (o * (g * jax.nn.sigmoid(g))).astype(BF16)
    kv = _dot_tn((k * kdec_ref[...]).astype(BF16), v)
    state_ref[...] = sdec_ref[...] * state + kv * smask_ref[...]


def _retention_tables(blk):
    heads = jnp.arange(RET_HEADS, dtype=F32)
    log_gamma = jnp.log(1.0 - 2.0 ** (-5.0 - heads))
    idx = jnp.arange(blk)
    diff = (idx[:, None] - idx[None, :]).astype(F32)
    same_chunk = (idx[:, None] // CHUNK) == (idx[None, :] // CHUNK)
    visible = (idx[None, :] <= idx[:, None]) | same_chunk
    dec = jnp.where(visible[None], jnp.exp(log_gamma[:, None, None] * jnp.abs(diff)[None]), 0.0)
    per_lane = jnp.repeat(log_gamma, RET_QK_DIM)
    pos = jnp.arange(blk, dtype=F32)
    qdec = jnp.exp((pos[:, None] + 1.0) * per_lane[None, :])
    kdec = jnp.exp((blk - 1.0 - pos)[:, None] * per_lane[None, :])
    sdec = jnp.exp(blk * per_lane)[:, None]
    smask = (jnp.arange(RET_HEADS * RET_QK_DIM)[:, None] // RET_QK_DIM
             == jnp.arange(RET_WIDTH)[None, :] // RET_V_DIM).astype(F32)
    return dec, qdec, kdec, sdec, smask


def _retention(qr, kr, vr, gr, gain):
    b, s, _ = qr.shape
    blk = min(RET_BLOCK, s)
    dec, qdec, kdec, sdec, smask = _retention_tables(blk)
    tok = lambda width: pl.BlockSpec((1, blk, width), lambda bi, i: (bi, i, 0))
    return pl.pallas_call(
        _retention_kernel,
        grid=(b, s // blk),
        in_specs=[_const_spec(dec.shape), _const_spec(qdec.shape), _const_spec(kdec.shape),
                  _const_spec(sdec.shape), _const_spec(smask.shape), _const_spec((1, RET_V_DIM)),
                  tok(256), tok(256), tok(512), tok(512)],
        out_specs=tok(RET_WIDTH),
        out_shape=jax.ShapeDtypeStruct((b, s, RET_WIDTH), BF16),
        scratch_shapes=[pltpu.VMEM((RET_HEADS * RET_QK_DIM, RET_WIDTH), F32)],
        compiler_params=_params("parallel", "arbitrary"),
        name="retention",
    )(dec, qdec, kdec, sdec, smask, gain, qr, kr, vr, gr)


def _mlp_tail(x1, g_ref, wup_ref, wdn_ref, o_ref):
    h = _rms(x1, g_ref[...]).astype(BF16)
    acc = x1
    for c in range(MLP_HIDDEN // MLP_HIDDEN_TILE):
        cs = slice(c * MLP_HIDDEN_TILE, (c + 1) * MLP_HIDDEN_TILE)
        u = jnp.maximum(_dot(h, wup_ref[:, cs]), 0.0)
        acc = acc + _dot((u * u).astype(BF16), wdn_ref[cs, :])
    o_ref[...] = acc


def _out_mlp_kernel(x_ref, a_ref, b_ref, wout_ref, g_ref, wup_ref, wdn_ref, o_ref):
    ka = a_ref.shape[1]
    mixed = _dot(a_ref[...], wout_ref[:ka, :]) + _dot(b_ref[...], wout_ref[ka:, :])
    _mlp_tail(x_ref[...] + mixed, g_ref, wup_ref, wdn_ref, o_ref)


def _s5_out_mlp_kernel(x_ref, yt_ref, ut_ref, d_ref, wglut_ref, b_ref, wout_ref, g_ref, wup_ref, wdn_ref, o_ref):
    y = yt_ref[...] + d_ref[...] * ut_ref[...]
    z = jax.nn.gelu(y)
    a_t = (z * jax.nn.sigmoid(_dot(wglut_ref[...], z.astype(BF16)))).astype(BF16)
    mixed = _dot_tn(a_t, wout_ref[:S5_WIDTH, :]) + _dot(b_ref[...], wout_ref[S5_WIDTH:, :])
    _mlp_tail(x_ref[...] + mixed, g_ref, wup_ref, wdn_ref, o_ref)


def _mlp_specs():
    return [_const_spec((D_MODEL, D_MODEL)), _const_spec((1, D_MODEL)),
            _const_spec((D_MODEL, MLP_HIDDEN)), _const_spec((MLP_HIDDEN, D_MODEL))]


def _out_mlp(x2, a, bb, wout, g, wup, wdn):
    t = x2.shape[0]
    tm = TOKEN_TILE
    row = lambda width: pl.BlockSpec((tm, width), lambda i: (i, 0))
    return pl.pallas_call(
        _out_mlp_kernel,
        grid=(t // tm,),
        in_specs=[row(D_MODEL), row(a.shape[1]), row(bb.shape[1])] + _mlp_specs(),
        out_specs=row(D_MODEL),
        out_shape=jax.ShapeDtypeStruct((t, D_MODEL), F32),
        compiler_params=_params("parallel"),
        name="ab_out_mlp",
    )(x2, a, bb, wout, g, wup, wdn)


def _s5_out_mlp(x2, y_t, u_t, d_col, wglu_t, bb, wout, g, wup, wdn):
    t = x2.shape[0]
    tm = TOKEN_TILE
    row = lambda width: pl.BlockSpec((tm, width), lambda i: (i, 0))
    col = pl.BlockSpec((S5_WIDTH, tm), lambda i: (0, i))
    return pl.pallas_call(
        _s5_out_mlp_kernel,
        grid=(t // tm,),
        in_specs=[row(D_MODEL), col, col, _const_spec((S5_WIDTH, 1)),
                  _const_spec((S5_WIDTH, S5_WIDTH)), row(GLA_WIDTH)] + _mlp_specs(),
        out_specs=row(D_MODEL),
        out_shape=jax.ShapeDtypeStruct((t, D_MODEL), F32),
        compiler_params=_params("parallel"),
        name="cd_out_mlp",
    )(x2, y_t, u_t, d_col, wglu_t, bb, wout, g, wup, wdn)


def _cd_in_kernel(x_ref, g_ref, w_ref, wut_ref, wa_ref, ba_ref, ut_ref, q_ref, k_ref, v_ref, r_ref, la_ref):
    h = _rms(x_ref[...], g_ref[...]).astype(BF16)

    def proj(lo, hi):
        return _dot(h, w_ref[:, lo:hi])

    ut_ref[...] = _dot_nt(wut_ref[...], h)
    q_ref[...] = proj(256, 640) * (GLA_QK_DIM ** -0.5)
    k_ref[...] = proj(640, 1024)
    v_ref[...] = proj(1024, 1792).astype(BF16)
    r_ref[...] = proj(1792, 2560)
    a_lr = proj(2560, CD_IN_PADDED)
    pre = _dot(a_lr.astype(BF16), wa_ref[...]) + ba_ref[...]
    log_sig = jnp.minimum(pre, 0.0) - jnp.log1p(jnp.exp(-jnp.abs(pre)))
    la_ref[...] = log_sig / GLA_TAU


def _cd_in(x2, g, w, wu_t, wa, ba):
    t = x2.shape[0]
    tm = TOKEN_TILE
    row = lambda width: pl.BlockSpec((tm, width), lambda i: (i, 0))
    out_shapes = (
        jax.ShapeDtypeStruct((S5_WIDTH, t), F32), jax.ShapeDtypeStruct((t, GLA_QK_WIDTH), F32),
        jax.ShapeDtypeStruct((t, GLA_QK_WIDTH), F32), jax.ShapeDtypeStruct((t, GLA_WIDTH), BF16),
        jax.ShapeDtypeStruct((t, GLA_WIDTH), F32), jax.ShapeDtypeStruct((t, GLA_QK_WIDTH), F32))
    return pl.pallas_call(
        _cd_in_kernel,
        grid=(t // tm,),
        in_specs=[row(D_MODEL), _const_spec((1, D_MODEL)), _const_spec((D_MODEL, CD_IN_PADDED)),
                  _const_spec((S5_WIDTH, D_MODEL)),
                  _const_spec((LANES_V7X, GLA_QK_WIDTH)), _const_spec((1, GLA_QK_WIDTH))],
        out_specs=(pl.BlockSpec((S5_WIDTH, tm), lambda i: (0, i)), row(GLA_QK_WIDTH), row(GLA_QK_WIDTH),
                   row(GLA_WIDTH), row(GLA_WIDTH), row(GLA_QK_WIDTH)),
        out_shape=out_shapes,
        compiler_params=_params("parallel"),
        name="cd_in_proj",
    )(x2, g, w, wu_t, wa, ba)


def _split3(x):
    hi = x.astype(BF16)
    r = x - hi.astype(F32)
    mid = r.astype(BF16)
    lo = (r - mid.astype(F32)).astype(BF16)
    return hi, mid, lo


def _gla_kernel(tri_ref, gain_ref, q_ref, k_ref, v_ref, r_ref, la_ref, o_ref, state_ref, *, blk):
    @pl.when(pl.program_id(1) == 0)
    def _():
        state_ref[...] = jnp.zeros_like(state_ref)

    nc = blk // CHUNK
    q = q_ref[0]
    k = k_ref[0]
    tri = tri_ref[...]
    hi, mid, lo = _split3(la_ref[0])
    b = _dot(tri, hi) + _dot(tri, mid) + _dot(tri, lo)
    b3 = b.reshape(nc, CHUNK, GLA_QK_WIDTH)
    b_last = b3[:, CHUNK - 1:CHUNK, :]
    e_pos = jnp.exp(b)
    e_neg = jnp.exp(-b)
    k_tail = jnp.exp(b_last - b3).reshape(blk, GLA_QK_WIDTH)
    chunk_decay = jnp.exp(b_last)
    qp = (q * e_pos).astype(BF16)
    qn = (q * e_neg).astype(BF16)
    kp = (k * e_pos).astype(BF16)
    kn = (k * e_neg).astype(BF16)
    kw = (k * k_tail).astype(BF16)

    row = lax.broadcasted_iota(jnp.int32, (blk, blk), 0)
    col = lax.broadcasted_iota(jnp.int32, (blk, blk), 1)
    same_chunk = row // CHUNK == col // CHUNK
    causal = row >= col
    lane = lax.broadcasted_iota(jnp.int32, (blk, 2 * GLA_QK_DIM), 1)
    smask = (lax.broadcasted_iota(jnp.int32, (2 * GLA_V_DIM, 2 * GLA_QK_DIM), 0) // GLA_V_DIM
             == lax.broadcasted_iota(jnp.int32, (2 * GLA_V_DIM, 2 * GLA_QK_DIM), 1) // GLA_QK_DIM)

    for p in range(GLA_HEADS // 2):
        ks = slice(p * 2 * GLA_QK_DIM, (p + 1) * 2 * GLA_QK_DIM)
        vs = slice(p * 2 * GLA_V_DIM, (p + 1) * 2 * GLA_V_DIM)
        v = v_ref[0, :, vs]
        intra = []
        for half in range(2):
            sel = (lane // GLA_QK_DIM) == half
            fwd = _dot_nt(jnp.where(sel, qp[:, ks], 0).astype(BF16), kn[:, ks])
            bwd = _dot_nt(jnp.where(sel, qn[:, ks], 0).astype(BF16), kp[:, ks])
            sc = jnp.where(same_chunk, jnp.where(causal, fwd, bwd), 0.0).astype(BF16)
            intra.append(_dot(sc, v[:, half * GLA_V_DIM:(half + 1) * GLA_V_DIM]))
        o_intra = jnp.concatenate(intra, axis=1)

        state = state_ref[p]
        cross = []
        for c in range(nc):
            rs = slice(c * CHUNK, (c + 1) * CHUNK)
            cross.append(_dot_nt(qp[rs, ks], state.astype(BF16)))
            kv_t = _dot_tn(v[rs, :], kw[rs, ks])
            state = state * chunk_decay[c][:, ks] + jnp.where(smask, kv_t, 0.0)
        state_ref[p] = state
        o = o_intra + jnp.concatenate(cross, axis=0)
        for half in range(2):
            hs = slice(half * GLA_V_DIM, (half + 1) * GLA_V_DIM)
            os_ = slice(p * 2 * GLA_V_DIM + half * GLA_V_DIM, p * 2 * GLA_V_DIM + (half + 1) * GLA_V_DIM)
            g = r_ref[0, :, os_]
            o_ref[0, :, os_] = (_rms(o[:, hs], gain_ref[...]) * (g * jax.nn.sigmoid(g))).astype(BF16)


def _gla(qg, kg, vg, rg, la, gain):
    b, s, _ = qg.shape
    blk = min(GLA_BLOCK, s)
    idx = jnp.arange(blk)
    tri = (((idx[:, None] // CHUNK) == (idx[None, :] // CHUNK)) & (idx[None, :] <= idx[:, None])).astype(BF16)
    tok = lambda width: pl.BlockSpec((1, blk, width), lambda bi, i: (bi, i, 0))
    return pl.pallas_call(
        functools.partial(_gla_kernel, blk=blk),
        grid=(b, s // blk),
        in_specs=[_const_spec((blk, blk)), _const_spec((1, GLA_V_DIM)),
                  tok(GLA_QK_WIDTH), tok(GLA_QK_WIDTH), tok(GLA_WIDTH), tok(GLA_WIDTH), tok(GLA_QK_WIDTH)],
        out_specs=tok(GLA_WIDTH),
        out_shape=jax.ShapeDtypeStruct((b, s, GLA_WIDTH), BF16),
        scratch_shapes=[pltpu.VMEM((GLA_HEADS // 2, 2 * GLA_V_DIM, 2 * GLA_QK_DIM), F32)],
        compiler_params=_params("parallel", "arbitrary"),
        name="gla",
    )(tri, gain, qg, kg, vg, rg, la)


def _s5_kernel(u_ref, kern_ref, fre_ref, fim_ref, ere_ref, eim_ref, lre_ref, lim_ref, y_ref,
               toep_ref, vre_ref, vim_ref, hre_ref, him_ref, *, batch, chunks):
    length = S5_CHUNK
    src = lax.broadcasted_iota(jnp.int32, (length, length), 0)
    dst = lax.broadcasted_iota(jnp.int32, (length, length), 1)

    def build(ci, carry):
        for co in range(S5_GROUP):
            lags = kern_ref[0, pl.ds(ci * S5_GROUP + co, 1), :]
            blk = pltpu.roll(jnp.broadcast_to(lags, (length, length)), 0, 1, stride=1, stride_axis=0)
            blk = jnp.where(dst >= src, blk, 0.0)
            toep_ref[pl.ds(pl.multiple_of(ci * length, length), length), co * length:(co + 1) * length] = (
                blk.astype(BF16))
        return carry

    lax.fori_loop(0, S5_GROUP, build, 0)

    u = jnp.concatenate([u_ref[0, c] for c in range(S5_GROUP)], axis=1).astype(BF16)
    vre_ref[...] = _dot(u, fre_ref[0])
    vim_ref[...] = _dot(u, fim_ref[0])
    lre = lre_ref[0]
    lim = lim_ref[0]

    def step(n, carry):
        new = []
        for bi in range(batch):
            hr, hi = carry[2 * bi], carry[2 * bi + 1]
            r = bi * chunks + n
            hre_ref[pl.ds(r, 1), :] = hr
            him_ref[pl.ds(r, 1), :] = hi
            vr = vre_ref[pl.ds(r, 1), :]
            vi = vim_ref[pl.ds(r, 1), :]
            new += [lre * hr - lim * hi + vr, lre * hi + lim * hr + vi]
        return tuple(new)

    zero = jnp.zeros((1, LANES_V7X), F32)
    lax.fori_loop(0, chunks, step, (zero,) * (2 * batch))
    y = (_dot(u, toep_ref[...]) + _dot(hre_ref[...].astype(BF16), ere_ref[0])
         + _dot(him_ref[...].astype(BF16), eim_ref[0]))
    for c in range(S5_GROUP):
        y_ref[0, c] = y[:, c * length:(c + 1) * length]


def _s5_tables(a_re, a_im, log_step, b_re, b_im, c_re, c_im):
    hp = lax.Precision.HIGHEST
    g, p, c, length = S5_GROUPS, S5_STATE, S5_GROUP, S5_CHUNK
    a_re, a_im = a_re.astype(F32), a_im.astype(F32)
    delta = jnp.exp(log_step.astype(F32))[:, None]
    tau = jnp.arange(length + 1, dtype=F32)[None, :, None]
    mag = jnp.exp((a_re * delta)[:, None, :] * tau)
    ang = (a_im * delta)[:, None, :] * tau
    pw_re, pw_im = mag * jnp.cos(ang), mag * jnp.sin(ang)
    n_re, n_im = pw_re[:, 1, :] - 1.0, pw_im[:, 1, :]
    den = a_re * a_re + a_im * a_im
    q_re, q_im = (n_re * a_re + n_im * a_im) / den, (n_im * a_re - n_re * a_im) / den
    bb_re = q_re[:, :, None] * b_re - q_im[:, :, None] * b_im
    bb_im = q_re[:, :, None] * b_im + q_im[:, :, None] * b_re
    first_re, first_im = pw_re[:, :length, None, :], pw_im[:, :length, None, :]
    cp_re = c_re[:, None] * first_re - c_im[:, None] * first_im
    cp_im = c_re[:, None] * first_im + c_im[:, None] * first_re
    kern = (jnp.einsum('gtcp,gpd->gdct', cp_re, bb_re, precision=hp)
            - jnp.einsum('gtcp,gpd->gdct', cp_im, bb_im, precision=hp)).reshape(g, c * c, length)
    rev_re, rev_im = pw_re[:, length - 1::-1, :][:, None], pw_im[:, length - 1::-1, :][:, None]
    bt_re, bt_im = bb_re.transpose(0, 2, 1)[:, :, None, :], bb_im.transpose(0, 2, 1)[:, :, None, :]
    f_re = (rev_re * bt_re - rev_im * bt_im).reshape(g, c * length, p)
    f_im = (rev_re * bt_im + rev_im * bt_re).reshape(g, c * length, p)
    nx_re = pw_re[:, 1:, :].transpose(0, 2, 1)[:, :, None, :]
    nx_im = pw_im[:, 1:, :].transpose(0, 2, 1)[:, :, None, :]
    ct_re, ct_im = c_re.transpose(0, 2, 1)[:, :, :, None], c_im.transpose(0, 2, 1)[:, :, :, None]
    e_re = (ct_re * nx_re - ct_im * nx_im).reshape(g, p, c * length)
    e_im = (ct_re * nx_im + ct_im * nx_re).reshape(g, p, c * length)
    pad = lambda z: jnp.pad(z, ((0, 0), (0, 0), (0, LANES_V7X - p)))
    pad_rows = lambda z: jnp.pad(z, ((0, 0), (0, LANES_V7X - p), (0, 0)))
    return (kern, pad(f_re).astype(BF16), pad(f_im).astype(BF16), pad_rows(e_re).astype(BF16),
            pad_rows(-e_im).astype(BF16), pad(pw_re[:, length:, :]), pad(pw_im[:, length:, :]))


def _s5(u_t, tables, batch):
    t = u_t.shape[1]
    length = S5_CHUNK
    rows = t // length
    width = length * S5_GROUP
    kern, fre, fim, ere, eim, lre, lim = tables
    grp = lambda r, c: pl.BlockSpec((1, r, c), lambda gi: (gi, 0, 0))
    io_spec = pl.BlockSpec((1, S5_GROUP, rows, length), lambda gi: (gi, 0, 0, 0))
    y = pl.pallas_call(
        functools.partial(_s5_kernel, batch=batch, chunks=rows // batch),
        grid=(S5_GROUPS,),
        in_specs=[io_spec, grp(S5_GROUP * S5_GROUP, length), grp(width, LANES_V7X), grp(width, LANES_V7X),
                  grp(LANES_V7X, width), grp(LANES_V7X, width), grp(1, LANES_V7X), grp(1, LANES_V7X)],
        out_specs=io_spec,
        out_shape=jax.ShapeDtypeStruct((S5_GROUPS, S5_GROUP, rows, length), F32),
        scratch_shapes=[pltpu.VMEM((width, width), BF16)] + [pltpu.VMEM((rows, LANES_V7X), F32)] * 4,
        compiler_params=_params("parallel"),
        name="s5",
    )(u_t.reshape(S5_GROUPS, S5_GROUP, rows, length), kern, fre, fim, ere, eim, lre, lim)
    return y.reshape(S5_WIDTH, t)


def _rotary_tables(seq):
    half = RET_QK_DIM // 2
    inv_freq = 1.0 / (ROPE_BASE ** jnp.linspace(0.0, 1.0, half, dtype=F32))
    ang = jnp.arange(seq, dtype=F32)[:, None] * inv_freq[None, :]
    cos, sin = jnp.cos(ang), jnp.sin(ang)
    cos_t = jnp.tile(jnp.concatenate([cos, cos], axis=1), (1, RET_HEADS))
    sin_t = jnp.tile(jnp.concatenate([-sin, sin], axis=1), (1, RET_HEADS))
    return cos_t, sin_t


def _row(v):
    return v.reshape(1, -1).astype(F32)


def kernel(x, norm_mix_g, norm_mlp_g, w_up, w_down, ab_w_in, ab_w_out, da_q_norm, da_k_norm,
           da_lam_q1, da_lam_k1, da_lam_q2, da_lam_k2, da_out_norm, ret_out_norm, cd_w_in, cd_w_out,
           s5_a_re, s5_a_im, s5_log_step, s5_b_re, s5_b_im, s5_c_re, s5_c_im, s5_d, s5_w_glu,
           gla_w_a2, gla_b_a2, gla_out_norm):
    bsz, seq, _ = x.shape
    t = bsz * seq
    x2 = x.reshape(t, D_MODEL)

    cos_t, sin_t = _rotary_tables(seq)
    cos_t = jnp.tile(cos_t, (bsz, 1))
    sin_t = jnp.tile(sin_t, (bsz, 1))
    gsum = ((jnp.arange(512)[:, None] // DA_QK_DIM) == (jnp.arange(512)[None, :] // DA_QK_DIM))
    gsum = (gsum.astype(F32) / DA_QK_DIM).astype(BF16)
    gq = _row(jnp.tile(da_q_norm[0], 2 * DA_HEADS)) * (DA_QK_DIM ** -0.5 * math.log2(math.e))
    gk = _row(jnp.tile(da_k_norm[0], 2 * DA_HEADS))
    w_ab = ab_w_in[0].astype(BF16)
    qa, ka, va_t, qr, kr, vr, gr = _ab_in(x2, _row(norm_mix_g[0]), w_ab, w_ab[:, 1024:1536].T, gq, gk, gsum,
                                          cos_t, sin_t, seq)
    lam_init = 0.8 - 0.6 * math.exp(-0.3 * 0)
    seq3 = lambda a: a.reshape(bsz, seq, a.shape[-1])
    o_a = _diff_attn(seq3(qa), seq3(ka), va_t, _row(da_lam_q1[0]), _row(da_lam_k1[0]),
                     _row(da_lam_q2[0]), _row(da_lam_k2[0]), _row(da_out_norm[0]), lam_init)
    o_r = _retention(seq3(qr), seq3(kr), seq3(vr), seq3(gr), _row(ret_out_norm[0]))
    x2 = _out_mlp(x2, o_a.reshape(t, DA_WIDTH), o_r.reshape(t, RET_WIDTH), ab_w_out[0].astype(BF16),
                  _row(norm_mlp_g[0]), w_up[0].astype(BF16), w_down[0].astype(BF16))

    w_cd = jnp.pad(cd_w_in[0], ((0, 0), (0, CD_IN_PADDED - CD_IN))).astype(BF16)
    wa = jnp.pad(gla_w_a2[0], ((0, LANES_V7X - GLA_GATE_RANK), (0, 0))).astype(BF16)
    u_t, qg, kg, vg, rg, la = _cd_in(x2, _row(norm_mix_g[1]), w_cd, w_cd[:, :S5_WIDTH].T, wa, _row(gla_b_a2[0]))
    o_d = _gla(seq3(qg), seq3(kg), seq3(vg), seq3(rg), seq3(la), _row(gla_out_norm[0]))
    tables = _s5_tables(s5_a_re[0], s5_a_im[0], s5_log_step[0], s5_b_re[0], s5_b_im[0], s5_c_re[0], s5_c_im[0])
    y_t = _s5(u_t, tables, bsz)
    x2 = _s5_out_mlp(x2, y_t, u_t, s5_d[0].reshape(S5_WIDTH, 1).astype(F32), s5_w_glu[0].T.astype(BF16),
                     o_d.reshape(t, GLA_WIDTH), cd_w_out[0].astype(BF16), _row(norm_mlp_g[1]),
                     w_up[1].astype(BF16), w_down[1].astype(BF16))
    return x2.reshape(bsz, seq, D_MODEL)
```

```python
import functools
import math

import jax
import jax.numpy as jnp
from jax import lax
from jax.experimental import pallas as pl
from jax.experimental.pallas import tpu as pltpu

F32 = jnp.float32
BF16 = jnp.bfloat16

D_MODEL = 1024
CHUNK = 64
RMS_EPS = 1e-6
ROPE_BASE = 10000.0
DA_HEADS = 4
DA_QK_DIM = 64
DA_V_DIM = 128
DA_WIDTH = DA_HEADS * DA_V_DIM
RET_HEADS = 4
RET_QK_DIM = 64
RET_V_DIM = 128
RET_WIDTH = RET_HEADS * RET_V_DIM
S5_WIDTH = 256
S5_GROUP = 16
S5_GROUPS = S5_WIDTH // S5_GROUP
S5_STATE = 64
GLA_HEADS = 6
GLA_QK_DIM = 64
GLA_V_DIM = 128
GLA_WIDTH = GLA_HEADS * GLA_V_DIM
GLA_QK_WIDTH = GLA_HEADS * GLA_QK_DIM
GLA_GATE_RANK = 16
GLA_TAU = 16.0
MLP_HIDDEN = 4 * D_MODEL
AB_IN = 3072
CD_IN = 2576

LANES_V7X = 128
VMEM_LIMIT_BYTES_V7X = 56 * 1024 * 1024

CD_IN_PADDED = 2688
NEG_BIG = -1e30

TOKEN_TILE = 512
ATTN_BLOCK = 512
ATTN_TILE = 256
RET_BLOCK = 512
GLA_BLOCK = 256
S5_CHUNK = LANES_V7X
MLP_HIDDEN_TILE = 1024


def _params(*semantics):
    return pltpu.CompilerParams(dimension_semantics=semantics, vmem_limit_bytes=VMEM_LIMIT_BYTES_V7X)


def _const_spec(shape):
    zeros = (0,) * len(shape)
    return pl.BlockSpec(shape, lambda *_: zeros, pipeline_mode=pl.Buffered(1))


def _rms(xf, gain):
    return xf * lax.rsqrt(jnp.mean(xf * xf, axis=-1, keepdims=True) + RMS_EPS) * gain


def _dot(a, b):
    return jnp.dot(a, b, preferred_element_type=F32)


def _dot_nt(a, b):
    return lax.dot_general(a, b, (((1,), (1,)), ((), ())), preferred_element_type=F32)


def _dot_tn(a, b):
    return lax.dot_general(a, b, (((0,), (0,)), ((), ())), preferred_element_type=F32)


def _swap_halves(x, group):
    n = x.shape[-1]
    half = group // 2
    lane = lax.broadcasted_iota(jnp.int32, x.shape, x.ndim - 1)
    from_right = pltpu.roll(x, n - half, axis=x.ndim - 1)
    from_left = pltpu.roll(x, half, axis=x.ndim - 1)
    return jnp.where((lane % group) < half, from_right, from_left)


def _ab_in_kernel(x_ref, g_ref, w_ref, wvt_ref, gq_ref, gk_ref, gsum_ref, cos_ref, sin_ref,
                  qa_ref, ka_ref, va_ref, qr_ref, kr_ref, vr_ref, gr_ref):
    h = _rms(x_ref[...], g_ref[...]).astype(BF16)

    def proj(lo, hi):
        return _dot(h, w_ref[:, lo:hi])

    def qk_norm(y, gain):
        ms = _dot((y * y).astype(BF16), gsum_ref[...])
        return (y * lax.rsqrt(ms + RMS_EPS) * gain).astype(BF16)

    def rotary(y):
        return y * cos_ref[...] + _swap_halves(y, RET_QK_DIM) * sin_ref[...]

    qa_ref[...] = qk_norm(proj(0, 512), gq_ref[...])
    ka_ref[...] = qk_norm(proj(512, 1024), gk_ref[...])
    va_ref[0] = _dot_nt(wvt_ref[...], h).astype(BF16)
    qr_ref[...] = rotary(proj(1536, 1792))
    kr_ref[...] = rotary(proj(1792, 2048)) * (RET_QK_DIM ** -0.5)
    vr_ref[...] = proj(2048, 2560).astype(BF16)
    gr_ref[...] = proj(2560, 3072)


def _ab_in(x2, g, w, wv_t, gq, gk, gsum, cos_t, sin_t, seq):
    t = x2.shape[0]
    tm = min(TOKEN_TILE, seq)
    per_seq = seq // tm
    row = lambda width: pl.BlockSpec((tm, width), lambda i: (i, 0))
    va_spec = pl.BlockSpec((1, DA_WIDTH, tm), lambda i: (i // per_seq, 0, i % per_seq))
    rot_spec = pl.BlockSpec((tm, 256), lambda i: (i % per_seq, 0))
    out_shapes = (
        jax.ShapeDtypeStruct((t, 512), BF16), jax.ShapeDtypeStruct((t, 512), BF16),
        jax.ShapeDtypeStruct((t // seq, DA_WIDTH, seq), BF16), jax.ShapeDtypeStruct((t, 256), F32),
        jax.ShapeDtypeStruct((t, 256), F32), jax.ShapeDtypeStruct((t, 512), BF16),
        jax.ShapeDtypeStruct((t, 512), F32))
    return pl.pallas_call(
        _ab_in_kernel,
        grid=(t // tm,),
        in_specs=[row(D_MODEL), _const_spec((1, D_MODEL)), _const_spec((D_MODEL, AB_IN)),
                  _const_spec((DA_WIDTH, D_MODEL)),
                  _const_spec((1, 512)), _const_spec((1, 512)), _const_spec((512, 512)),
                  rot_spec, rot_spec],
        out_specs=(row(512), row(512), va_spec, row(256), row(256), row(512), row(512)),
        out_shape=out_shapes,
        compiler_params=_params("parallel"),
        name="ab_in_proj",
    )(x2, g, w, wv_t, gq, gk, gsum, cos_t, sin_t)


def _diff_attn_kernel(lq1_ref, lk1_ref, lq2_ref, lk2_ref, gout_ref, q_ref, k_ref, vt_ref, o_ref,
                      sa_ref, sb_ref, sc_ref, mxa_ref, mxb_ref, mxc_ref, qza_ref, qzb_ref, m_ref, l_ref, acc_ref,
                      *, blk, nblk, lam_init):
    lam = (jnp.exp(jnp.sum(lq1_ref[...] * lk1_ref[...], axis=-1, keepdims=True))
           - jnp.exp(jnp.sum(lq2_ref[...] * lk2_ref[...], axis=-1, keepdims=True)) + lam_init)
    gain = gout_ref[...] * (1.0 - lam_init)
    buf_a, buf_b, buf_c = (sa_ref, mxa_ref), (sb_ref, mxb_ref), (sc_ref, mxc_ref)

    def load_queries(qb, qz_ref):
        start = pl.multiple_of(qb * blk, blk)
        qt = q_ref[0, pl.ds(start, blk), :].astype(F32).T
        dim = lax.broadcasted_iota(jnp.int32, qt.shape, 0)
        qz_ref[...] = jnp.concatenate([jnp.where(dim < DA_QK_DIM, qt, 0.0), jnp.where(dim >= DA_QK_DIM, qt, 0.0)],
                                      axis=1).astype(BF16)

    def score(qz_ref, t, buf):
        s_ref, mx_ref = buf
        start = pl.multiple_of(t * blk, blk)
        st = _dot(k_ref[0, pl.ds(start, blk), :], qz_ref[...])
        s_ref[...] = st
        mx_ref[...] = jnp.max(st, axis=0, keepdims=True)

    def absorb(t, buf, diagonal=False):
        s_ref, mx_ref = buf
        start = pl.multiple_of(t * blk, blk)
        tile = ATTN_TILE
        for qg in range(2 * blk // tile):
            cs = slice(qg * tile, (qg + 1) * tile)

            def scores(kh):
                st = s_ref[kh * tile:(kh + 1) * tile, cs]
                if diagonal:
                    key = lax.broadcasted_iota(jnp.int32, st.shape, 0) + kh * tile
                    qry = lax.broadcasted_iota(jnp.int32, st.shape, 1) + (qg * tile) % blk
                    st = jnp.where(key // CHUNK <= qry // CHUNK, st, NEG_BIG)
                return st

            if diagonal:
                mx = functools.reduce(jnp.maximum, [jnp.max(scores(kh), axis=0, keepdims=True)
                                                    for kh in range(blk // tile)])
            else:
                mx = mx_ref[:, cs]
            m = m_ref[:, cs]
            m_new = jnp.maximum(m, mx)
            alpha = jnp.exp2(m - m_new)
            l = alpha * l_ref[:, cs]
            acc = alpha * acc_ref[:, cs]
            for kh in range(blk // tile):
                p = jnp.exp2(scores(kh) - m_new)
                l = l + jnp.sum(p, axis=0, keepdims=True)
                acc = acc + _dot(vt_ref[0, :, pl.ds(start + kh * tile, tile)], p.astype(BF16))
            m_ref[:, cs] = m_new
            l_ref[:, cs] = l
            acc_ref[:, cs] = acc

    def reset():
        m_ref[...] = jnp.full(m_ref.shape, NEG_BIG, F32)
        l_ref[...] = jnp.zeros(l_ref.shape, F32)
        acc_ref[...] = jnp.zeros(acc_ref.shape, F32)

    def finish(qb):
        o = acc_ref[...] / l_ref[...]
        o = o[:, :blk] - lam * o[:, blk:]
        o = o * lax.rsqrt(jnp.mean(o * o, axis=0, keepdims=True) + RMS_EPS)
        o_ref[0, pl.ds(pl.multiple_of(qb * blk, blk), blk), :] = (o.T * gain).astype(BF16)
        reset()

    reset()
    load_queries(0, qza_ref)
    load_queries(1, qzb_ref)
    score(qza_ref, 0, buf_a)
    score(qzb_ref, 0, buf_b)
    absorb(0, buf_a, diagonal=True)
    finish(0)
    score(qzb_ref, 1, buf_a)
    absorb(0, buf_b)
    load_queries(min(2, nblk - 1), qza_ref)
    score(qza_ref, 0, buf_c)
    absorb(1, buf_a, diagonal=True)
    finish(1)

    def pair(g, carry):
        a = 2 * g
        load_queries(a + 1, qzb_ref)
        score(qza_ref, 1, buf_b)
        absorb(0, buf_c)
        score(qza_ref, 2, buf_a)
        absorb(1, buf_b)

        def stages_a(u, c):
            score(qza_ref, 2 * u + 1, buf_b)
            absorb(2 * u, buf_a)
            score(qza_ref, 2 * u + 2, buf_a)
            absorb(2 * u + 1, buf_b)
            return c

        lax.fori_loop(1, g, stages_a, 0)
        score(qzb_ref, 0, buf_b)
        absorb(a, buf_a, diagonal=True)
        finish(a)
        score(qzb_ref, 1, buf_a)
        absorb(0, buf_b)

        def stages_b(u, c):
            score(qzb_ref, 2 * u + 2, buf_b)
            absorb(2 * u + 1, buf_a)
            score(qzb_ref, 2 * u + 3, buf_a)
            absorb(2 * u + 2, buf_b)
            return c

        lax.fori_loop(0, g, stages_b, 0)
        load_queries(jnp.minimum(a + 2, nblk - 1), qza_ref)
        score(qza_ref, 0, buf_c)
        absorb(a + 1, buf_a, diagonal=True)
        finish(a + 1)
        return carry

    lax.fori_loop(1, nblk // 2, pair, 0)


def _diff_attn(qa, ka, va_t, lq1, lk1, lq2, lk2, gout, lam_init):
    b, s, _ = qa.shape
    blk = min(ATTN_BLOCK, s // 2)
    nblk = s // blk
    assert nblk % 2 == 0 and blk % ATTN_TILE == 0, (s, blk)
    vec = _const_spec((1, DA_QK_DIM))
    tok_spec = pl.BlockSpec((1, s, 128), lambda bi, h: (bi, 0, h))
    vt_spec = pl.BlockSpec((1, DA_V_DIM, s), lambda bi, h: (bi, h, 0))
    scores = pltpu.VMEM((blk, 2 * blk), F32)
    stats = pltpu.VMEM((1, 2 * blk), F32)
    queries = pltpu.VMEM((2 * DA_QK_DIM, 2 * blk), BF16)
    return pl.pallas_call(
        functools.partial(_diff_attn_kernel, blk=blk, nblk=nblk, lam_init=lam_init),
        grid=(b, DA_HEADS),
        in_specs=[vec, vec, vec, vec, _const_spec((1, DA_V_DIM)), tok_spec, tok_spec, vt_spec],
        out_specs=tok_spec,
        out_shape=jax.ShapeDtypeStruct((b, s, DA_WIDTH), BF16),
        scratch_shapes=[scores, scores, scores, stats, stats, stats, queries, queries, stats, stats,
                        pltpu.VMEM((DA_V_DIM, 2 * blk), F32)],
        compiler_params=_params("parallel", "parallel"),
        name="diff_attention",
    )(lq1, lk1, lq2, lk2, gout, qa, ka, va_t)


def _retention_kernel(dec_ref, qdec_ref, kdec_ref, sdec_ref, smask_ref, gain_ref,
                      q_ref, k_ref, v_ref, g_ref, o_ref, state_ref):
    @pl.when(pl.program_id(1) == 0)
    def _():
        state_ref[...] = jnp.zeros_like(state_ref)

    q = q_ref[0]
    k = k_ref[0]
    v = v_ref[0]
    kb = k.astype(BF16)
    lane = lax.broadcasted_iota(jnp.int32, q.shape, 1)
    state = state_ref[...]
    o_cross = _dot((q * qdec_ref[...]).astype(BF16), state.astype(BF16))
    for h in range(RET_HEADS):
        qh = jnp.where(lane // RET_QK_DIM == h, q, 0.0).astype(BF16)
        w = (_dot_nt(qh, kb) * dec_ref[h]).astype(BF16)
        vs = slice(h * RET_V_DIM, (h + 1) * RET_V_DIM)
        o = _dot(w, v[:, vs]) + o_cross[:, vs]
        o = _rms(o, gain_ref[...])
        g = g_ref[0, :, vs]
        o_ref[0, :, vs] = (o * (g * jax.nn.sigmoid(g))).astype(BF16)
    kv = _dot_tn((k * kdec_ref[...]).astype(BF16), v)
    state_ref[...] = sdec_ref[...] * state + kv * smask_ref[...]


def _retention_tables(blk):
    heads = jnp.arange(RET_HEADS, dtype=F32)
    log_gamma = jnp.log(1.0 - 2.0 ** (-5.0 - heads))
    idx = jnp.arange(blk)
    diff = (idx[:, None] - idx[None, :]).astype(F32)
    same_chunk = (idx[:, None] // CHUNK) == (idx[None, :] // CHUNK)
    visible = (idx[None, :] <= idx[:, None]) | same_chunk
    dec = jnp.where(visible[None], jnp.exp(log_gamma[:, None, None] * jnp.abs(diff)[None]), 0.0)
    per_lane = jnp.repeat(log_gamma, RET_QK_DIM)
    pos = jnp.arange(blk, dtype=F32)
    qdec = jnp.exp((pos[:, None] + 1.0) * per_lane[None, :])
    kdec = jnp.exp((blk - 1.0 - pos)[:, None] * per_lane[None, :])
    sdec = jnp.exp(blk * per_lane)[:, None]
    smask = (jnp.arange(RET_HEADS * RET_QK_DIM)[:, None] // RET_QK_DIM
             == jnp.arange(RET_WIDTH)[None, :] // RET_V_DIM).astype(F32)
    return dec, qdec, kdec, sdec, smask


def _retention(qr, kr, vr, gr, gain):
    b, s, _ = qr.shape
    blk = min(RET_BLOCK, s)
    dec, qdec, kdec, sdec, smask = _retention_tables(blk)
    tok = lambda width: pl.BlockSpec((1, blk, width), lambda bi, i: (bi, i, 0))
    return pl.pallas_call(
        _retention_kernel,
        grid=(b, s // blk),
        in_specs=[_const_spec(dec.shape), _const_spec(qdec.shape), _const_spec(kdec.shape),
                  _const_spec(sdec.shape), _const_spec(smask.shape), _const_spec((1, RET_V_DIM)),
                  tok(256), tok(256), tok(512), tok(512)],
        out_specs=tok(RET_WIDTH),
        out_shape=jax.ShapeDtypeStruct((b, s, RET_WIDTH), BF16),
        scratch_shapes=[pltpu.VMEM((RET_HEADS * RET_QK_DIM, RET_WIDTH), F32)],
        compiler_params=_params("parallel", "arbitrary"),
        name="retention",
    )(dec, qdec, kdec, sdec, smask, gain, qr, kr, vr, gr)


def _mlp_tail(x1, g_ref, wup_ref, wdn_ref, o_ref):
    h = _rms(x1, g_ref[...]).astype(BF16)
    acc = x1
    for c in range(MLP_HIDDEN // MLP_HIDDEN_TILE):
        cs = slice(c * MLP_HIDDEN_TILE, (c + 1) * MLP_HIDDEN_TILE)
        u = jnp.maximum(_dot(h, wup_ref[:, cs]), 0.0)
        acc = acc + _dot((u * u).astype(BF16), wdn_ref[cs, :])
    o_ref[...] = acc


def _out_mlp_kernel(x_ref, a_ref, b_ref, wout_ref, g_ref, wup_ref, wdn_ref, o_ref):
    ka = a_ref.shape[1]
    mixed = _dot(a_ref[...], wout_ref[:ka, :]) + _dot(b_ref[...], wout_ref[ka:, :])
    _mlp_tail(x_ref[...] + mixed, g_ref, wup_ref, wdn_ref, o_ref)


def _s5_out_mlp_kernel(x_ref, yt_ref, ut_ref, d_ref, wglut_ref, b_ref, wout_ref, g_ref, wup_ref, wdn_ref, o_ref):
    y = yt_ref[...] + d_ref[...] * ut_ref[...]
    z = jax.nn.gelu(y)
    a_t = (z * jax.nn.sigmoid(_dot(wglut_ref[...], z.astype(BF16)))).astype(BF16)
    mixed = _dot_tn(a_t, wout_ref[:S5_WIDTH, :]) + _dot(b_ref[...], wout_ref[S5_WIDTH:, :])
    _mlp_tail(x_ref[...] + mixed, g_ref, wup_ref, wdn_ref, o_ref)


def _mlp_specs():
    return [_const_spec((D_MODEL, D_MODEL)), _const_spec((1, D_MODEL)),
            _const_spec((D_MODEL, MLP_HIDDEN)), _const_spec((MLP_HIDDEN, D_MODEL))]


def _out_mlp(x2, a, bb, wout, g, wup, wdn):
    t = x2.shape[0]
    tm = TOKEN_TILE
    row = lambda width: pl.BlockSpec((tm, width), lambda i: (i, 0))
    return pl.pallas_call(
        _out_mlp_kernel,
        grid=(t // tm,),
        in_specs=[row(D_MODEL), row(a.shape[1]), row(bb.shape[1])] + _mlp_specs(),
        out_specs=row(D_MODEL),
        out_shape=jax.ShapeDtypeStruct((t, D_MODEL), F32),
        compiler_params=_params("parallel"),
        name="ab_out_mlp",
    )(x2, a, bb, wout, g, wup, wdn)


def _s5_out_mlp(x2, y_t, u_t, d_col, wglu_t, bb, wout, g, wup, wdn):
    t = x2.shape[0]
    tm = TOKEN_TILE
    row = lambda width: pl.BlockSpec((tm, width), lambda i: (i, 0))
    col = pl.BlockSpec((S5_WIDTH, tm), lambda i: (0, i))
    return pl.pallas_call(
        _s5_out_mlp_kernel,
        grid=(t // tm,),
        in_specs=[row(D_MODEL), col, col, _const_spec((S5_WIDTH, 1)),
                  _const_spec((S5_WIDTH, S5_WIDTH)), row(GLA_WIDTH)] + _mlp_specs(),
        out_specs=row(D_MODEL),
        out_shape=jax.ShapeDtypeStruct((t, D_MODEL), F32),
        compiler_params=_params("parallel"),
        name="cd_out_mlp",
    )(x2, y_t, u_t, d_col, wglu_t, bb, wout, g, wup, wdn)


def _cd_in_kernel(x_ref, g_ref, w_ref, wut_ref, wa_ref, ba_ref, ut_ref, q_ref, k_ref, v_ref, r_ref, la_ref):
    h = _rms(x_ref[...], g_ref[...]).astype(BF16)

    def proj(lo, hi):
        return _dot(h, w_ref[:, lo:hi])

    ut_ref[...] = _dot_nt(wut_ref[...], h)
    q_ref[...] = proj(256, 640) * (GLA_QK_DIM ** -0.5)
    k_ref[...] = proj(640, 1024)
    v_ref[...] = proj(1024, 1792).astype(BF16)
    r_ref[...] = proj(1792, 2560)
    a_lr = proj(2560, CD_IN_PADDED)
    pre = _dot(a_lr.astype(BF16), wa_ref[...]) + ba_ref[...]
    log_sig = jnp.minimum(pre, 0.0) - jnp.log1p(jnp.exp(-jnp.abs(pre)))
    la_ref[...] = log_sig / GLA_TAU


def _cd_in(x2, g, w, wu_t, wa, ba):
    t = x2.shape[0]
    tm = TOKEN_TILE
    row = lambda width: pl.BlockSpec((tm, width), lambda i: (i, 0))
    out_shapes = (
        jax.ShapeDtypeStruct((S5_WIDTH, t), F32), jax.ShapeDtypeStruct((t, GLA_QK_WIDTH), F32),
        jax.ShapeDtypeStruct((t, GLA_QK_WIDTH), F32), jax.ShapeDtypeStruct((t, GLA_WIDTH), BF16),
        jax.ShapeDtypeStruct((t, GLA_WIDTH), F32), jax.ShapeDtypeStruct((t, GLA_QK_WIDTH), F32))
    return pl.pallas_call(
        _cd_in_kernel,
        grid=(t // tm,),
        in_specs=[row(D_MODEL), _const_spec((1, D_MODEL)), _const_spec((D_MODEL, CD_IN_PADDED)),
                  _const_spec((S5_WIDTH, D_MODEL)),
                  _const_spec((LANES_V7X, GLA_QK_WIDTH)), _const_spec((1, GLA_QK_WIDTH))],
        out_specs=(pl.BlockSpec((S5_WIDTH, tm), lambda i: (0, i)), row(GLA_QK_WIDTH), row(GLA_QK_WIDTH),
                   row(GLA_WIDTH), row(GLA_WIDTH), row(GLA_QK_WIDTH)),
        out_shape=out_shapes,
        compiler_params=_params("parallel"),
        name="cd_in_proj",
    )(x2, g, w, wu_t, wa, ba)


def _split3(x):
    hi = x.astype(BF16)
    r = x - hi.astype(F32)
    mid = r.astype(BF16)
    lo = (r - mid.astype(F32)).astype(BF16)
    return hi, mid, lo


def _gla_kernel(tri_ref, gain_ref, q_ref, k_ref, v_ref, r_ref, la_ref, o_ref, state_ref, *, blk):
    @pl.when(pl.program_id(1) == 0)
    def _():
        state_ref[...] = jnp.zeros_like(state_ref)

    nc = blk // CHUNK
    q = q_ref[0]
    k = k_ref[0]
    tri = tri_ref[...]
    hi, mid, lo = _split3(la_ref[0])
    b = _dot(tri, hi) + _dot(tri, mid) + _dot(tri, lo)
    b3 = b.reshape(nc, CHUNK, GLA_QK_WIDTH)
    b_last = b3[:, CHUNK - 1:CHUNK, :]
    e_pos = jnp.exp(b)
    e_neg = jnp.exp(-b)
    k_tail = jnp.exp(b_last - b3).reshape(blk, GLA_QK_WIDTH)
    chunk_decay = jnp.exp(b_last)
    qp = (q * e_pos).astype(BF16)
    qn = (q * e_neg).astype(BF16)
    kp = (k * e_pos).astype(BF16)
    kn = (k * e_neg).astype(BF16)
    kw = (k * k_tail).astype(BF16)

    row = lax.broadcasted_iota(jnp.int32, (blk, blk), 0)
    col = lax.broadcasted_iota(jnp.int32, (blk, blk), 1)
    same_chunk = row // CHUNK == col // CHUNK
    causal = row >= col
    lane = lax.broadcasted_iota(jnp.int32, (blk, 2 * GLA_QK_DIM), 1)
    smask = (lax.broadcasted_iota(jnp.int32, (2 * GLA_V_DIM, 2 * GLA_QK_DIM), 0) // GLA_V_DIM
             == lax.broadcasted_iota(jnp.int32, (2 * GLA_V_DIM, 2 * GLA_QK_DIM), 1) // GLA_QK_DIM)

    for p in range(GLA_HEADS // 2):
        ks = slice(p * 2 * GLA_QK_DIM, (p + 1) * 2 * GLA_QK_DIM)
        vs = slice(p * 2 * GLA_V_DIM, (p + 1) * 2 * GLA_V_DIM)
        v = v_ref[0, :, vs]
        intra = []
        for half in range(2):
            sel = (lane // GLA_QK_DIM) == half
            fwd = _dot_nt(jnp.where(sel, qp[:, ks], 0).astype(BF16), kn[:, ks])
            bwd = _dot_nt(jnp.where(sel, qn[:, ks], 0).astype(BF16), kp[:, ks])
            sc = jnp.where(same_chunk, jnp.where(causal, fwd, bwd), 0.0).astype(BF16)
            intra.append(_dot(sc, v[:, half * GLA_V_DIM:(half + 1) * GLA_V_DIM]))
        o_intra = jnp.concatenate(intra, axis=1)

        state = state_ref[p]
        cross = []
        for c in range(nc):
            rs = slice(c * CHUNK, (c + 1) * CHUNK)
            cross.append(_dot_nt(qp[rs, ks], state.astype(BF16)))
            kv_t = _dot_tn(v[rs, :], kw[rs, ks])
            state = state * chunk_decay[c][:, ks] + jnp.where(smask, kv_t, 0.0)
        state_ref[p] = state
        o = o_intra + jnp.concatenate(cross, axis=0)
        for half in range(2):
            hs = slice(half * GLA_V_DIM, (half + 1) * GLA_V_DIM)
            os_ = slice(p * 2 * GLA_V_DIM + half * GLA_V_DIM, p * 2 * GLA_V_DIM + (half + 1) * GLA_V_DIM)
            g = r_ref[0, :, os_]
            o_ref[0, :, os_] = (_rms(o[:, hs], gain_ref[...]) * (g * jax.nn.sigmoid(g))).astype(BF16)


def _gla(qg, kg, vg, rg, la, gain):
    b, s, _ = qg.shape
    blk = min(GLA_BLOCK, s)
    idx = jnp.arange(blk)
    tri = (((idx[:, None] // CHUNK) == (idx[None, :] // CHUNK)) & (idx[None, :] <= idx[:, None])).astype(BF16)
    tok = lambda width: pl.BlockSpec((1, blk, width), lambda bi, i: (bi, i, 0))
    return pl.pallas_call(
        functools.partial(_gla_kernel, blk=blk),
        grid=(b, s // blk),
        in_specs=[_const_spec((blk, blk)), _const_spec((1, GLA_V_DIM)),
                  tok(GLA_QK_WIDTH), tok(GLA_QK_WIDTH), tok(GLA_WIDTH), tok(GLA_WIDTH), tok(GLA_QK_WIDTH)],
        out_specs=tok(GLA_WIDTH),
        out_shape=jax.ShapeDtypeStruct((b, s, GLA_WIDTH), BF16),
        scratch_shapes=[pltpu.VMEM((GLA_HEADS // 2, 2 * GLA_V_DIM, 2 * GLA_QK_DIM), F32)],
        compiler_params=_params("parallel", "arbitrary"),
        name="gla",
    )(tri, gain, qg, kg, vg, rg, la)


def _s5_kernel(u_ref, kern_ref, fre_ref, fim_ref, ere_ref, eim_ref, lre_ref, lim_ref, y_ref,
               toep_ref, vre_ref, vim_ref, hre_ref, him_ref, *, batch, chunks):
    length = S5_CHUNK
    src = lax.broadcasted_iota(jnp.int32, (length, length), 0)
    dst = lax.broadcasted_iota(jnp.int32, (length, length), 1)

    def build(ci, carry):
        for co in range(S5_GROUP):
            lags = kern_ref[0, pl.ds(ci * S5_GROUP + co, 1), :]
            blk = pltpu.roll(jnp.broadcast_to(lags, (length, length)), 0, 1, stride=1, stride_axis=0)
            blk = jnp.where(dst >= src, blk, 0.0)
            toep_ref[pl.ds(pl.multiple_of(ci * length, length), length), co * length:(co + 1) * length] = (
                blk.astype(BF16))
        return carry

    lax.fori_loop(0, S5_GROUP, build, 0)

    u = jnp.concatenate([u_ref[0, c] for c in range(S5_GROUP)], axis=1).astype(BF16)
    vre_ref[...] = _dot(u, fre_ref[0])
    vim_ref[...] = _dot(u, fim_ref[0])
    lre = lre_ref[0]
    lim = lim_ref[0]

    def step(n, carry):
        new = []
        for bi in range(batch):
            hr, hi = carry[2 * bi], carry[2 * bi + 1]
            r = bi * chunks + n
            hre_ref[pl.ds(r, 1), :] = hr
            him_ref[pl.ds(r, 1), :] = hi
            vr = vre_ref[pl.ds(r, 1), :]
            vi = vim_ref[pl.ds(r, 1), :]
            new += [lre * hr - lim * hi + vr, lre * hi + lim * hr + vi]
        return tuple(new)

    zero = jnp.zeros((1, LANES_V7X), F32)
    lax.fori_loop(0, chunks, step, (zero,) * (2 * batch))
    y = (_dot(u, toep_ref[...]) + _dot(hre_ref[...].astype(BF16), ere_ref[0])
         + _dot(him_ref[...].astype(BF16), eim_ref[0]))
    for c in range(S5_GROUP):
        y_ref[0, c] = y[:, c * length:(c + 1) * length]


def _s5_tables(a_re, a_im, log_step, b_re, b_im, c_re, c_im):
    hp = lax.Precision.HIGHEST
    g, p, c, length = S5_GROUPS, S5_STATE, S5_GROUP, S5_CHUNK
    a_re, a_im = a_re.astype(F32), a_im.astype(F32)
    delta = jnp.exp(log_step.astype(F32))[:, None]
    tau = jnp.arange(length + 1, dtype=F32)[None, :, None]
    mag = jnp.exp((a_re * delta)[:, None, :] * tau)
    ang = (a_im * delta)[:, None, :] * tau
    pw_re, pw_im = mag * jnp.cos(ang), mag * jnp.sin(ang)
    n_re, n_im = pw_re[:, 1, :] - 1.0, pw_im[:, 1, :]
    den = a_re * a_re + a_im * a_im
    q_re, q_im = (n_re * a_re + n_im * a_im) / den, (n_im * a_re - n_re * a_im) / den
    bb_re = q_re[:, :, None] * b_re - q_im[:, :, None] * b_im
    bb_im = q_re[:, :, None] * b_im + q_im[:, :, None] * b_re
    first_re, first_im = pw_re[:, :length, None, :], pw_im[:, :length, None, :]
    cp_re = c_re[:, None] * first_re - c_im[:, None] * first_im
    cp_im = c_re[:, None] * first_im + c_im[:, None] * first_re
    kern = (jnp.einsum('gtcp,gpd->gdct', cp_re, bb_re, precision=hp)
            - jnp.einsum('gtcp,gpd->gdct', cp_im, bb_im, precision=hp)).reshape(g, c * c, length)
    rev_re, rev_im = pw_re[:, length - 1::-1, :][:, None], pw_im[:, length - 1::-1, :][:, None]
    bt_re, bt_im = bb_re.transpose(0, 2, 1)[:, :, None, :], bb_im.transpose(0, 2, 1)[:, :, None, :]
    f_re = (rev_re * bt_re - rev_im * bt_im).reshape(g, c * length, p)
    f_im = (rev_re * bt_im + rev_im * bt_re).reshape(g, c * length, p)
    nx_re = pw_re[:, 1:, :].transpose(0, 2, 1)[:, :, None, :]
    nx_im = pw_im[:, 1:, :].transpose(0, 2, 1)[:, :, None, :]
    ct_re, ct_im = c_re.transpose(0, 2, 1)[:, :, :, None], c_im.transpose(0, 2, 1)[:, :, :, None]
    e_re = (ct_re * nx_re - ct_im * nx_im).reshape(g, p, c * length)
    e_im = (ct_re * nx_im + ct_im * nx_re).reshape(g, p, c * length)
    pad = lambda z: jnp.pad(z, ((0, 0), (0, 0), (0, LANES_V7X - p)))
    pad_rows = lambda z: jnp.pad(z, ((0, 0), (0, LANES_V7X - p), (0, 0)))
    return (kern, pad(f_re).astype(BF16), pad(f_im).astype(BF16), pad_rows(e_re).astype(BF16),
            pad_rows(-e_im).astype(BF16), pad(pw_re[:, length:, :]), pad(pw_im[:, length:, :]))


def _s5(u_t, tables, batch):
    t = u_t.shape[1]
    length = S5_CHUNK
    rows = t // length
    width = length * S5_GROUP
    kern, fre, fim, ere, eim, lre, lim = tables
    grp = lambda r, c: pl.BlockSpec((1, r, c), lambda gi: (gi, 0, 0))
    io_spec = pl.BlockSpec((1, S5_GROUP, rows, length), lambda gi: (gi, 0, 0, 0))
    y = pl.pallas_call(
        functools.partial(_s5_kernel, batch=batch, chunks=rows // batch),
        grid=(S5_GROUPS,),
        in_specs=[io_spec, grp(S5_GROUP * S5_GROUP, length), grp(width, LANES_V7X), grp(width, LANES_V7X),
                  grp(LANES_V7X, width), grp(LANES_V7X, width), grp(1, LANES_V7X), grp(1, LANES_V7X)],
        out_specs=io_spec,
        out_shape=jax.ShapeDtypeStruct((S5_GROUPS, S5_GROUP, rows, length), F32),
        scratch_shapes=[pltpu.VMEM((width, width), BF16)] + [pltpu.VMEM((rows, LANES_V7X), F32)] * 4,
        compiler_params=_params("parallel"),
        name="s5",
    )(u_t.reshape(S5_GROUPS, S5_GROUP, rows, length), kern, fre, fim, ere, eim, lre, lim)
    return y.reshape(S5_WIDTH, t)


def _rotary_tables(seq):
    half = RET_QK_DIM // 2
    inv_freq = 1.0 / (ROPE_BASE ** jnp.linspace(0.0, 1.0, half, dtype=F32))
    ang = jnp.arange(seq, dtype=F32)[:, None] * inv_freq[None, :]
    cos, sin = jnp.cos(ang), jnp.sin(ang)
    cos_t = jnp.tile(jnp.concatenate([cos, cos], axis=1), (1, RET_HEADS))
    sin_t = jnp.tile(jnp.concatenate([-sin, sin], axis=1), (1, RET_HEADS))
    return cos_t, sin_t


def _row(v):
    return v.reshape(1, -1).astype(F32)


def kernel(x, norm_mix_g, norm_mlp_g, w_up, w_down, ab_w_in, ab_w_out, da_q_norm, da_k_norm,
           da_lam_q1, da_lam_k1, da_lam_q2, da_lam_k2, da_out_norm, ret_out_norm, cd_w_in, cd_w_out,
           s5_a_re, s5_a_im, s5_log_step, s5_b_re, s5_b_im, s5_c_re, s5_c_im, s5_d, s5_w_glu,
           gla_w_a2, gla_b_a2, gla_out_norm):
    bsz, seq, _ = x.shape
    t = bsz * seq
    x2 = x.reshape(t, D_MODEL)

    cos_t, sin_t = _rotary_tables(seq)
    gsum = ((jnp.arange(512)[:, None] // DA_QK_DIM) == (jnp.arange(512)[None, :] // DA_QK_DIM))
    gsum = (gsum.astype(F32) / DA_QK_DIM).astype(BF16)
    gq = _row(jnp.tile(da_q_norm[0], 2 * DA_HEADS)) * (DA_QK_DIM ** -0.5 * math.log2(math.e))
    gk = _row(jnp.tile(da_k_norm[0], 2 * DA_HEADS))
    w_ab = ab_w_in[0].astype(BF16)
    qa, ka, va_t, qr, kr, vr, gr = _ab_in(x2, _row(norm_mix_g[0]), w_ab, w_ab[:, 1024:1536].T, gq, gk, gsum,
                                          cos_t, sin_t, seq)
    lam_init = 0.8 - 0.6 * math.exp(-0.3 * 0)
    seq3 = lambda a: a.reshape(bsz, seq, a.shape[-1])
    o_a = _diff_attn(seq3(qa), seq3(ka), va_t, _row(da_lam_q1[0]), _row(da_lam_k1[0]),
                     _row(da_lam_q2[0]), _row(da_lam_k2[0]), _row(da_out_norm[0]), lam_init)
    o_r = _retention(seq3(qr), seq3(kr), seq3(vr), seq3(gr), _row(ret_out_norm[0]))
    x2 = _out_mlp(x2, o_a.reshape(t, DA_WIDTH), o_r.reshape(t, RET_WIDTH), ab_w_out[0].astype(BF16),
                  _row(norm_mlp_g[0]), w_up[0].astype(BF16), w_down[0].astype(BF16))

    w_cd = jnp.pad(cd_w_in[0], ((0, 0), (0, CD_IN_PADDED - CD_IN))).astype(BF16)
    wa = jnp.pad(gla_w_a2[0], ((0, LANES_V7X - GLA_GATE_RANK), (0, 0))).astype(BF16)
    u_t, qg, kg, vg, rg, la = _cd_in(x2, _row(norm_mix_g[1]), w_cd, w_cd[:, :S5_WIDTH].T, wa, _row(gla_b_a2[0]))
    o_d = _gla(seq3(qg), seq3(kg), seq3(vg), seq3(rg), seq3(la), _row(gla_out_norm[0]))
    tables = _s5_tables(s5_a_re[0], s5_a_im[0], s5_log_step[0], s5_b_re[0], s5_b_im[0], s5_c_re[0], s5_c_im[0])
    y_t = _s5(u_t, tables, bsz)
    x2 = _s5_out_mlp(x2, y_t, u_t, s5_d[0].reshape(S5_WIDTH, 1).astype(F32), s5_w_glu[0].T.astype(BF16),
                     o_d.reshape(t, GLA_WIDTH), cd_w_out[0].astype(BF16), _row(norm_mlp_g[1]),
                     w_up[1].astype(BF16), w_down[1].astype(BF16))
    return x2.reshape(bsz, seq, D_MODEL)
```

```python
import functools
import math

import jax
import jax.numpy as jnp
from jax import lax
from jax.experimental import pallas as pl
from jax.experimental.pallas import tpu as pltpu

F32 = jnp.float32
BF16 = jnp.bfloat16

D_MODEL = 1024
CHUNK = 64
RMS_EPS = 1e-6
ROPE_BASE = 10000.0
DA_HEADS = 4
DA_QK_DIM = 64
DA_V_DIM = 128
DA_WIDTH = DA_HEADS * DA_V_DIM
RET_HEADS = 4
RET_QK_DIM = 64
RET_V_DIM = 128
RET_WIDTH = RET_HEADS * RET_V_DIM
S5_WIDTH = 256
S5_GROUP = 16
S5_GROUPS = S5_WIDTH // S5_GROUP
S5_STATE = 64
GLA_HEADS = 6
GLA_QK_DIM = 64
GLA_V_DIM = 128
GLA_WIDTH = GLA_HEADS * GLA_V_DIM
GLA_QK_WIDTH = GLA_HEADS * GLA_QK_DIM
GLA_GATE_RANK = 16
GLA_TAU = 16.0
MLP_HIDDEN = 4 * D_MODEL
AB_IN = 3072
CD_IN = 2576

LANES_V7X = 128
VMEM_LIMIT_BYTES_V7X = 56 * 1024 * 1024

CD_IN_PADDED = 2688
NEG_BIG = -1e30

TOKEN_TILE = 512
ATTN_BLOCK = 512
ATTN_TILE = 256
RET_BLOCK = 512
GLA_BLOCK = 256
S5_CHUNK = LANES_V7X
MLP_HIDDEN_TILE = 1024


def _params(*semantics):
    return pltpu.CompilerParams(dimension_semantics=semantics, vmem_limit_bytes=VMEM_LIMIT_BYTES_V7X)


def _const_spec(shape):
    zeros = (0,) * len(shape)
    return pl.BlockSpec(shape, lambda *_: zeros, pipeline_mode=pl.Buffered(1))


def _rms(xf, gain):
    return xf * lax.rsqrt(jnp.mean(xf * xf, axis=-1, keepdims=True) + RMS_EPS) * gain


def _dot(a, b):
    return jnp.dot(a, b, preferred_element_type=F32)


def _dot_nt(a, b):
    return lax.dot_general(a, b, (((1,), (1,)), ((), ())), preferred_element_type=F32)


def _dot_tn(a, b):
    return lax.dot_general(a, b, (((0,), (0,)), ((), ())), preferred_element_type=F32)


def _swap_halves(x, group):
    n = x.shape[-1]
    half = group // 2
    lane = lax.broadcasted_iota(jnp.int32, x.shape, x.ndim - 1)
    from_right = pltpu.roll(x, n - half, axis=x.ndim - 1)
    from_left = pltpu.roll(x, half, axis=x.ndim - 1)
    return jnp.where((lane % group) < half, from_right, from_left)


def _ab_in_kernel(x_ref, g_ref, w_ref, wvt_ref, gq_ref, gk_ref, gsum_ref, cos_ref, sin_ref,
                  qa_ref, ka_ref, va_ref, qr_ref, kr_ref, vr_ref, gr_ref):
    h = _rms(x_ref[...], g_ref[...]).astype(BF16)

    def proj(lo, hi):
        return _dot(h, w_ref[:, lo:hi])

    def qk_norm(y, gain):
        ms = _dot((y * y).astype(BF16), gsum_ref[...])
        return (y * lax.rsqrt(ms + RMS_EPS) * gain).astype(BF16)

    def rotary(y):
        return y * cos_ref[...] + _swap_halves(y, RET_QK_DIM) * sin_ref[...]

    qa_ref[...] = qk_norm(proj(0, 512), gq_ref[...])
    ka_ref[...] = qk_norm(proj(512, 1024), gk_ref[...])
    va_ref[0] = _dot_nt(wvt_ref[...], h).astype(BF16)
    qr_ref[...] = rotary(proj(1536, 1792))
    kr_ref[...] = rotary(proj(1792, 2048)) * (RET_QK_DIM ** -0.5)
    vr_ref[...] = proj(2048, 2560).astype(BF16)
    gr_ref[...] = proj(2560, 3072)


def _ab_in(x2, g, w, wv_t, gq, gk, gsum, cos_t, sin_t, seq):
    t = x2.shape[0]
    tm = min(TOKEN_TILE, seq)
    per_seq = seq // tm
    row = lambda width: pl.BlockSpec((tm, width), lambda i: (i, 0))
    va_spec = pl.BlockSpec((1, DA_WIDTH, tm), lambda i: (i // per_seq, 0, i % per_seq))
    rot_spec = pl.BlockSpec((tm, 256), lambda i: (i % per_seq, 0))
    out_shapes = (
        jax.ShapeDtypeStruct((t, 512), BF16), jax.ShapeDtypeStruct((t, 512), BF16),
        jax.ShapeDtypeStruct((t // seq, DA_WIDTH, seq), BF16), jax.ShapeDtypeStruct((t, 256), F32),
        jax.ShapeDtypeStruct((t, 256), F32), jax.ShapeDtypeStruct((t, 512), BF16),
        jax.ShapeDtypeStruct((t, 512), F32))
    return pl.pallas_call(
        _ab_in_kernel,
        grid=(t // tm,),
        in_specs=[row(D_MODEL), _const_spec((1, D_MODEL)), _const_spec((D_MODEL, AB_IN)),
                  _const_spec((DA_WIDTH, D_MODEL)),
                  _const_spec((1, 512)), _const_spec((1, 512)), _const_spec((512, 512)),
                  rot_spec, rot_spec],
        out_specs=(row(512), row(512), va_spec, row(256), row(256), row(512), row(512)),
        out_shape=out_shapes,
        compiler_params=_params("parallel"),
        name="ab_in_proj",
    )(x2, g, w, wv_t, gq, gk, gsum, cos_t, sin_t)


def _diff_attn_kernel(lq1_ref, lk1_ref, lq2_ref, lk2_ref, gout_ref, q_ref, k_ref, vt_ref, o_ref,
                      sa_ref, sb_ref, sc_ref, mxa_ref, mxb_ref, mxc_ref, qza_ref, qzb_ref, m_ref, l_ref, acc_ref,
                      *, blk, nblk, lam_init):
    lam = (jnp.exp(jnp.sum(lq1_ref[...] * lk1_ref[...], axis=-1, keepdims=True))
           - jnp.exp(jnp.sum(lq2_ref[...] * lk2_ref[...], axis=-1, keepdims=True)) + lam_init)
    gain = gout_ref[...] * (1.0 - lam_init)
    buf_a, buf_b, buf_c = (sa_ref, mxa_ref), (sb_ref, mxb_ref), (sc_ref, mxc_ref)

    def load_queries(qb, qz_ref):
        start = pl.multiple_of(qb * blk, blk)
        qt = q_ref[0, pl.ds(start, blk), :].astype(F32).T
        dim = lax.broadcasted_iota(jnp.int32, qt.shape, 0)
        qz_ref[...] = jnp.concatenate([jnp.where(dim < DA_QK_DIM, qt, 0.0), jnp.where(dim >= DA_QK_DIM, qt, 0.0)],
                                      axis=1).astype(BF16)

    def score(qz_ref, t, buf):
        s_ref, mx_ref = buf
        start = pl.multiple_of(t * blk, blk)
        st = _dot(k_ref[0, pl.ds(start, blk), :], qz_ref[...])
        s_ref[...] = st
        mx_ref[...] = jnp.max(st, axis=0, keepdims=True)

    def absorb(t, buf, diagonal=False):
        s_ref, mx_ref = buf
        start = pl.multiple_of(t * blk, blk)
        tile = ATTN_TILE
        for qg in range(2 * blk // tile):
            cs = slice(qg * tile, (qg + 1) * tile)

            def scores(kh):
                st = s_ref[kh * tile:(kh + 1) * tile, cs]
                if diagonal:
                    key = lax.broadcasted_iota(jnp.int32, st.shape, 0) + kh * tile
                    qry = lax.broadcasted_iota(jnp.int32, st.shape, 1) + (qg * tile) % blk
                    st = jnp.where(key // CHUNK <= qry // CHUNK, st, NEG_BIG)
                return st

            if diagonal:
                mx = functools.reduce(jnp.maximum, [jnp.max(scores(kh), axis=0, keepdims=True)
                                                    for kh in range(blk // tile)])
            else:
                mx = mx_ref[:, cs]
            m = m_ref[:, cs]
            m_new = jnp.maximum(m, mx)
            alpha = jnp.exp2(m - m_new)
            l = alpha * l_ref[:, cs]
            acc = alpha * acc_ref[:, cs]
            for kh in range(blk // tile):
                p = jnp.exp2(scores(kh) - m_new)
                l = l + jnp.sum(p, axis=0, keepdims=True)
                acc = acc + _dot(vt_ref[0, :, pl.ds(start + kh * tile, tile)], p.astype(BF16))
            m_ref[:, cs] = m_new
            l_ref[:, cs] = l
            acc_ref[:, cs] = acc

    def repeat(first, count, two_stages):
        def twice(v, c):
            two_stages(first + 2 * v)
            two_stages(first + 2 * v + 1)
            return c

        lax.fori_loop(0, lax.shift_right_logical(count, 1), twice, 0)

        @pl.when((count & 1) == 1)
        def _():
            two_stages(first + count - 1)

    def reset():
        m_ref[...] = jnp.full(m_ref.shape, NEG_BIG, F32)
        l_ref[...] = jnp.zeros(l_ref.shape, F32)
        acc_ref[...] = jnp.zeros(acc_ref.shape, F32)

    def finish(qb):
        o = acc_ref[...] / l_ref[...]
        o = o[:, :blk] - lam * o[:, blk:]
        o = o * lax.rsqrt(jnp.mean(o * o, axis=0, keepdims=True) + RMS_EPS)
        o_ref[0, pl.ds(pl.multiple_of(qb * blk, blk), blk), :] = (o.T * gain).astype(BF16)
        reset()

    reset()
    load_queries(0, qza_ref)
    load_queries(1, qzb_ref)
    score(qza_ref, 0, buf_a)
    score(qzb_ref, 0, buf_b)
    absorb(0, buf_a, diagonal=True)
    finish(0)
    score(qzb_ref, 1, buf_a)
    absorb(0, buf_b)
    load_queries(min(2, nblk - 1), qza_ref)
    score(qza_ref, 0, buf_c)
    absorb(1, buf_a, diagonal=True)
    finish(1)

    def pair(g, carry):
        a = 2 * g
        load_queries(a + 1, qzb_ref)
        score(qza_ref, 1, buf_b)
        absorb(0, buf_c)
        score(qza_ref, 2, buf_a)
        absorb(1, buf_b)

        def stages_a(u):
            score(qza_ref, 2 * u + 1, buf_b)
            absorb(2 * u, buf_a)
            score(qza_ref, 2 * u + 2, buf_a)
            absorb(2 * u + 1, buf_b)

        repeat(1, g - 1, stages_a)
        score(qzb_ref, 0, buf_b)
        absorb(a, buf_a, diagonal=True)
        finish(a)
        score(qzb_ref, 1, buf_a)
        absorb(0, buf_b)

        def stages_b(u):
            score(qzb_ref, 2 * u + 2, buf_b)
            absorb(2 * u + 1, buf_a)
            score(qzb_ref, 2 * u + 3, buf_a)
            absorb(2 * u + 2, buf_b)

        repeat(0, g, stages_b)
        load_queries(jnp.minimum(a + 2, nblk - 1), qza_ref)
        score(qza_ref, 0, buf_c)
        absorb(a + 1, buf_a, diagonal=True)
        finish(a + 1)
        return carry

    lax.fori_loop(1, nblk // 2, pair, 0)


def _diff_attn(qa, ka, va_t, lq1, lk1, lq2, lk2, gout, lam_init):
    b, s, _ = qa.shape
    blk = min(ATTN_BLOCK, s // 2)
    nblk = s // blk
    assert nblk % 2 == 0 and blk % ATTN_TILE == 0, (s, blk)
    vec = _const_spec((1, DA_QK_DIM))
    tok_spec = pl.BlockSpec((1, s, 128), lambda bi, h: (bi, 0, h))
    vt_spec = pl.BlockSpec((1, DA_V_DIM, s), lambda bi, h: (bi, h, 0))
    scores = pltpu.VMEM((blk, 2 * blk), F32)
    stats = pltpu.VMEM((1, 2 * blk), F32)
    queries = pltpu.VMEM((2 * DA_QK_DIM, 2 * blk), BF16)
    return pl.pallas_call(
        functools.partial(_diff_attn_kernel, blk=blk, nblk=nblk, lam_init=lam_init),
        grid=(b, DA_HEADS),
        in_specs=[vec, vec, vec, vec, _const_spec((1, DA_V_DIM)), tok_spec, tok_spec, vt_spec],
        out_specs=tok_spec,
        out_shape=jax.ShapeDtypeStruct((b, s, DA_WIDTH), BF16),
        scratch_shapes=[scores, scores, scores, stats, stats, stats, queries, queries, stats, stats,
                        pltpu.VMEM((DA_V_DIM, 2 * blk), F32)],
        compiler_params=_params("parallel", "parallel"),
        name="diff_attention",
    )(lq1, lk1, lq2, lk2, gout, qa, ka, va_t)


def _retention_kernel(dec_ref, qdec_ref, kdec_ref, sdec_ref, smask_ref, gain_ref,
                      q_ref, k_ref, v_ref, g_ref, o_ref, state_ref):
    @pl.when(pl.program_id(1) == 0)
    def _():
        state_ref[...] = jnp.zeros_like(state_ref)

    q = q_ref[0]
    k = k_ref[0]
    v = v_ref[0]
    kb = k.astype(BF16)
    lane = lax.broadcasted_iota(jnp.int32, q.shape, 1)
    state = state_ref[...]
    o_cross = _dot((q * qdec_ref[...]).astype(BF16), state.astype(BF16))
    for h in range(RET_HEADS):
        qh = jnp.where(lane // RET_QK_DIM == h, q, 0.0).astype(BF16)
        w = (_dot_nt(qh, kb) * dec_ref[h]).astype(BF16)
        vs = slice(h * RET_V_DIM, (h + 1) * RET_V_DIM)
        o = _dot(w, v[:, vs]) + o_cross[:, vs]
        o = _rms(o, gain_ref[...])
        g = g_ref[0, :, vs]
        o_ref[0, :, vs] = (o * (g * jax.nn.sigmoid(g))).astype(BF16)
    kv = _dot_tn((k * kdec_ref[...]).astype(BF16), v)
    state_ref[...] = sdec_ref[...] * state + kv * smask_ref[...]


def _retention_tables(blk):
    heads = jnp.arange(RET_HEADS, dtype=F32)
    log_gamma = jnp.log(1.0 - 2.0 ** (-5.0 - heads))
    idx = jnp.arange(blk)
    diff = (idx[:, None] - idx[None, :]).astype(F32)
    same_chunk = (idx[:, None] // CHUNK) == (idx[None, :] // CHUNK)
    visible = (idx[None, :] <= idx[:, None]) | same_chunk
    dec = jnp.where(visible[None], jnp.exp(log_gamma[:, None, None] * jnp.abs(diff)[None]), 0.0)
    per_lane = jnp.repeat(log_gamma, RET_QK_DIM)
    pos = jnp.arange(blk, dtype=F32)
    qdec = jnp.exp((pos[:, None] + 1.0) * per_lane[None, :])
    kdec = jnp.exp((blk - 1.0 - pos)[:, None] * per_lane[None, :])
    sdec = jnp.exp(blk * per_lane)[:, None]
    smask = (jnp.arange(RET_HEADS * RET_QK_DIM)[:, None] // RET_QK_DIM
             == jnp.arange(RET_WIDTH)[None, :] // RET_V_DIM).astype(F32)
    return dec, qdec, kdec, sdec, smask


def _retention(qr, kr, vr, gr, gain):
    b, s, _ = qr.shape
    blk = min(RET_BLOCK, s)
    dec, qdec, kdec, sdec, smask = _retention_tables(blk)
    tok = lambda width: pl.BlockSpec((1, blk, width), lambda bi, i: (bi, i, 0))
    return pl.pallas_call(
        _retention_kernel,
        grid=(b, s // blk),
        in_specs=[_const_spec(dec.shape), _const_spec(qdec.shape), _const_spec(kdec.shape),
                  _const_spec(sdec.shape), _const_spec(smask.shape), _const_spec((1, RET_V_DIM)),
                  tok(256), tok(256), tok(512), tok(512)],
        out_specs=tok(RET_WIDTH),
        out_shape=jax.ShapeDtypeStruct((b, s, RET_WIDTH), BF16),
        scratch_shapes=[pltpu.VMEM((RET_HEADS * RET_QK_DIM, RET_WIDTH), F32)],
        compiler_params=_params("parallel", "arbitrary"),
        name="retention",
    )(dec, qdec, kdec, sdec, smask, gain, qr, kr, vr, gr)


def _mlp_tail(x1, g_ref, wup_ref, wdn_ref, o_ref):
    h = _rms(x1, g_ref[...]).astype(BF16)
    acc = x1
    for c in range(MLP_HIDDEN // MLP_HIDDEN_TILE):
        cs = slice(c * MLP_HIDDEN_TILE, (c + 1) * MLP_HIDDEN_TILE)
        u = jnp.maximum(_dot(h, wup_ref[:, cs]), 0.0)
        acc = acc + _dot((u * u).astype(BF16), wdn_ref[cs, :])
    o_ref[...] = acc


def _out_mlp_kernel(x_ref, a_ref, b_ref, wout_ref, g_ref, wup_ref, wdn_ref, o_ref):
    ka = a_ref.shape[1]
    mixed = _dot(a_ref[...], wout_ref[:ka, :]) + _dot(b_ref[...], wout_ref[ka:, :])
    _mlp_tail(x_ref[...] + mixed, g_ref, wup_ref, wdn_ref, o_ref)


def _s5_out_mlp_kernel(x_ref, yt_ref, ut_ref, d_ref, wglut_ref, b_ref, wout_ref, g_ref, wup_ref, wdn_ref, o_ref):
    y = yt_ref[...] + d_ref[...] * ut_ref[...]
    z = jax.nn.gelu(y)
    a_t = (z * jax.nn.sigmoid(_dot(wglut_ref[...], z.astype(BF16)))).astype(BF16)
    mixed = _dot_tn(a_t, wout_ref[:S5_WIDTH, :]) + _dot(b_ref[...], wout_ref[S5_WIDTH:, :])
    _mlp_tail(x_ref[...] + mixed, g_ref, wup_ref, wdn_ref, o_ref)


def _mlp_specs():
    return [_const_spec((D_MODEL, D_MODEL)), _const_spec((1, D_MODEL)),
            _const_spec((D_MODEL, MLP_HIDDEN)), _const_spec((MLP_HIDDEN, D_MODEL))]


def _out_mlp(x2, a, bb, wout, g, wup, wdn):
    t = x2.shape[0]
    tm = TOKEN_TILE
    row = lambda width: pl.BlockSpec((tm, width), lambda i: (i, 0))
    return pl.pallas_call(
        _out_mlp_kernel,
        grid=(t // tm,),
        in_specs=[row(D_MODEL), row(a.shape[1]), row(bb.shape[1])] + _mlp_specs(),
        out_specs=row(D_MODEL),
        out_shape=jax.ShapeDtypeStruct((t, D_MODEL), F32),
        compiler_params=_params("parallel"),
        name="ab_out_mlp",
    )(x2, a, bb, wout, g, wup, wdn)


def _s5_out_mlp(x2, y_t, u_t, d_col, wglu_t, bb, wout, g, wup, wdn):
    t = x2.shape[0]
    tm = TOKEN_TILE
    row = lambda width: pl.BlockSpec((tm, width), lambda i: (i, 0))
    col = pl.BlockSpec((S5_WIDTH, tm), lambda i: (0, i))
    return pl.pallas_call(
        _s5_out_mlp_kernel,
        grid=(t // tm,),
        in_specs=[row(D_MODEL), col, col, _const_spec((S5_WIDTH, 1)),
                  _const_spec((S5_WIDTH, S5_WIDTH)), row(GLA_WIDTH)] + _mlp_specs(),
        out_specs=row(D_MODEL),
        out_shape=jax.ShapeDtypeStruct((t, D_MODEL), F32),
        compiler_params=_params("parallel"),
        name="cd_out_mlp",
    )(x2, y_t, u_t, d_col, wglu_t, bb, wout, g, wup, wdn)


def _cd_in_kernel(x_ref, g_ref, w_ref, wut_ref, wa_ref, ba_ref, ut_ref, q_ref, k_ref, v_ref, r_ref, la_ref):
    h = _rms(x_ref[...], g_ref[...]).astype(BF16)

    def proj(lo, hi):
        return _dot(h, w_ref[:, lo:hi])

    ut_ref[...] = _dot_nt(wut_ref[...], h)
    q_ref[...] = proj(256, 640) * (GLA_QK_DIM ** -0.5)
    k_ref[...] = proj(640, 1024)
    v_ref[...] = proj(1024, 1792).astype(BF16)
    r_ref[...] = proj(1792, 2560)
    a_lr = proj(2560, CD_IN_PADDED)
    pre = _dot(a_lr.astype(BF16), wa_ref[...]) + ba_ref[...]
    log_sig = jnp.minimum(pre, 0.0) - jnp.log1p(jnp.exp(-jnp.abs(pre)))
    la_ref[...] = log_sig / GLA_TAU


def _cd_in(x2, g, w, wu_t, wa, ba):
    t = x2.shape[0]
    tm = TOKEN_TILE
    row = lambda width: pl.BlockSpec((tm, width), lambda i: (i, 0))
    out_shapes = (
        jax.ShapeDtypeStruct((S5_WIDTH, t), F32), jax.ShapeDtypeStruct((t, GLA_QK_WIDTH), F32),
        jax.ShapeDtypeStruct((t, GLA_QK_WIDTH), F32), jax.ShapeDtypeStruct((t, GLA_WIDTH), BF16),
        jax.ShapeDtypeStruct((t, GLA_WIDTH), F32), jax.ShapeDtypeStruct((t, GLA_QK_WIDTH), F32))
    return pl.pallas_call(
        _cd_in_kernel,
        grid=(t // tm,),
        in_specs=[row(D_MODEL), _const_spec((1, D_MODEL)), _const_spec((D_MODEL, CD_IN_PADDED)),
                  _const_spec((S5_WIDTH, D_MODEL)),
                  _const_spec((LANES_V7X, GLA_QK_WIDTH)), _const_spec((1, GLA_QK_WIDTH))],
        out_specs=(pl.BlockSpec((S5_WIDTH, tm), lambda i: (0, i)), row(GLA_QK_WIDTH), row(GLA_QK_WIDTH),
                   row(GLA_WIDTH), row(GLA_WIDTH), row(GLA_QK_WIDTH)),
        out_shape=out_shapes,
        compiler_params=_params("parallel"),
        name="cd_in_proj",
    )(x2, g, w, wu_t, wa, ba)


def _split3(x):
    hi = x.astype(BF16)
    r = x - hi.astype(F32)
    mid = r.astype(BF16)
    lo = (r - mid.astype(F32)).astype(BF16)
    return hi, mid, lo


def _gla_kernel(tri_ref, gain_ref, q_ref, k_ref, v_ref, r_ref, la_ref, o_ref, state_ref, *, blk, batch):
    @pl.when(pl.program_id(0) == 0)
    def _():
        state_ref[...] = jnp.zeros_like(state_ref)

    for bi in range(batch):
        _gla_block(bi, tri_ref, gain_ref, q_ref, k_ref, v_ref, r_ref, la_ref, o_ref, state_ref, blk)


def _gla_block(bi, tri_ref, gain_ref, q_ref, k_ref, v_ref, r_ref, la_ref, o_ref, state_ref, blk):
    nc = blk // CHUNK
    pairs = GLA_HEADS // 2
    q = q_ref[bi]
    k = k_ref[bi]
    tri = tri_ref[...]
    hi, mid, lo = _split3(la_ref[bi])
    b = _dot(tri, hi) + _dot(tri, mid) + _dot(tri, lo)
    b3 = b.reshape(nc, CHUNK, GLA_QK_WIDTH)
    b_last = b3[:, CHUNK - 1:CHUNK, :]
    e_pos = jnp.exp(b)
    e_neg = jnp.exp(-b)
    k_tail = jnp.exp(b_last - b3).reshape(blk, GLA_QK_WIDTH)
    chunk_decay = jnp.exp(b_last)
    qp = (q * e_pos).astype(BF16)
    qn = (q * e_neg).astype(BF16)
    kp = (k * e_pos).astype(BF16)
    kn = (k * e_neg).astype(BF16)
    kw = (k * k_tail).astype(BF16)

    row = lax.broadcasted_iota(jnp.int32, (blk, blk), 0)
    col = lax.broadcasted_iota(jnp.int32, (blk, blk), 1)
    same_chunk = row // CHUNK == col // CHUNK
    causal = row >= col
    lane = lax.broadcasted_iota(jnp.int32, (blk, 2 * GLA_QK_DIM), 1)
    smask = (lax.broadcasted_iota(jnp.int32, (2 * GLA_V_DIM, 2 * GLA_QK_DIM), 0) // GLA_V_DIM
             == lax.broadcasted_iota(jnp.int32, (2 * GLA_V_DIM, 2 * GLA_QK_DIM), 1) // GLA_QK_DIM)

    for p in range(pairs):
        ks = slice(p * 2 * GLA_QK_DIM, (p + 1) * 2 * GLA_QK_DIM)
        vs = slice(p * 2 * GLA_V_DIM, (p + 1) * 2 * GLA_V_DIM)
        v = v_ref[bi, :, vs]
        intra = []
        for half in range(2):
            sel = (lane // GLA_QK_DIM) == half
            fwd = _dot_nt(jnp.where(sel, qp[:, ks], 0).astype(BF16), kn[:, ks])
            bwd = _dot_nt(jnp.where(sel, qn[:, ks], 0).astype(BF16), kp[:, ks])
            sc = jnp.where(same_chunk, jnp.where(causal, fwd, bwd), 0.0).astype(BF16)
            intra.append(_dot(sc, v[:, half * GLA_V_DIM:(half + 1) * GLA_V_DIM]))
        o_intra = jnp.concatenate(intra, axis=1)

        state = state_ref[bi * pairs + p]
        cross = []
        for c in range(nc):
            rs = slice(c * CHUNK, (c + 1) * CHUNK)
            cross.append(_dot_nt(qp[rs, ks], state.astype(BF16)))
            kv_t = _dot_tn(v[rs, :], kw[rs, ks])
            state = state * chunk_decay[c][:, ks] + jnp.where(smask, kv_t, 0.0)
        state_ref[bi * pairs + p] = state
        o = o_intra + jnp.concatenate(cross, axis=0)
        for half in range(2):
            hs = slice(half * GLA_V_DIM, (half + 1) * GLA_V_DIM)
            os_ = slice(p * 2 * GLA_V_DIM + half * GLA_V_DIM, p * 2 * GLA_V_DIM + (half + 1) * GLA_V_DIM)
            g = r_ref[bi, :, os_]
            o_ref[bi, :, os_] = (_rms(o[:, hs], gain_ref[...]) * (g * jax.nn.sigmoid(g))).astype(BF16)


def _gla(qg, kg, vg, rg, la, gain):
    b, s, _ = qg.shape
    blk = min(GLA_BLOCK, s)
    idx = jnp.arange(blk)
    tri = (((idx[:, None] // CHUNK) == (idx[None, :] // CHUNK)) & (idx[None, :] <= idx[:, None])).astype(BF16)
    tok = lambda width: pl.BlockSpec((b, blk, width), lambda i: (0, i, 0))
    return pl.pallas_call(
        functools.partial(_gla_kernel, blk=blk, batch=b),
        grid=(s // blk,),
        in_specs=[_const_spec((blk, blk)), _const_spec((1, GLA_V_DIM)),
                  tok(GLA_QK_WIDTH), tok(GLA_QK_WIDTH), tok(GLA_WIDTH), tok(GLA_WIDTH), tok(GLA_QK_WIDTH)],
        out_specs=tok(GLA_WIDTH),
        out_shape=jax.ShapeDtypeStruct((b, s, GLA_WIDTH), BF16),
        scratch_shapes=[pltpu.VMEM((b * (GLA_HEADS // 2), 2 * GLA_V_DIM, 2 * GLA_QK_DIM), F32)],
        compiler_params=_params("arbitrary"),
        name="gla",
    )(tri, gain, qg, kg, vg, rg, la)


def _s5_kernel(u_ref, kern_ref, fre_ref, fim_ref, ere_ref, eim_ref, lre_ref, lim_ref, y_ref,
               toep_ref, vre_ref, vim_ref, hre_ref, him_ref, *, batch, chunks):
    length = S5_CHUNK
    src = lax.broadcasted_iota(jnp.int32, (length, length), 0)
    dst = lax.broadcasted_iota(jnp.int32, (length, length), 1)

    def build(ci, carry):
        for co in range(S5_GROUP):
            lags = kern_ref[0, pl.ds(ci * S5_GROUP + co, 1), :]
            blk = pltpu.roll(jnp.broadcast_to(lags, (length, length)), 0, 1, stride=1, stride_axis=0)
            blk = jnp.where(dst >= src, blk, 0.0)
            toep_ref[pl.ds(pl.multiple_of(ci * length, length), length), co * length:(co + 1) * length] = (
                blk.astype(BF16))
        return carry

    lax.fori_loop(0, S5_GROUP, build, 0)

    u = jnp.concatenate([u_ref[0, c] for c in range(S5_GROUP)], axis=1).astype(BF16)
    vre_ref[...] = _dot(u, fre_ref[0])
    vim_ref[...] = _dot(u, fim_ref[0])
    lre = lre_ref[0]
    lim = lim_ref[0]

    def step(n, carry):
        new = []
        for bi in range(batch):
            hr, hi = carry[2 * bi], carry[2 * bi + 1]
            r = bi * chunks + n
            hre_ref[pl.ds(r, 1), :] = hr
            him_ref[pl.ds(r, 1), :] = hi
            vr = vre_ref[pl.ds(r, 1), :]
            vi = vim_ref[pl.ds(r, 1), :]
            new += [lre * hr - lim * hi + vr, lre * hi + lim * hr + vi]
        return tuple(new)

    zero = jnp.zeros((1, LANES_V7X), F32)
    lax.fori_loop(0, chunks, step, (zero,) * (2 * batch))
    y = (_dot(u, toep_ref[...]) + _dot(hre_ref[...].astype(BF16), ere_ref[0])
         + _dot(him_ref[...].astype(BF16), eim_ref[0]))
    for c in range(S5_GROUP):
        y_ref[0, c] = y[:, c * length:(c + 1) * length]


def _s5_tables(a_re, a_im, log_step, b_re, b_im, c_re, c_im):
    hp = lax.Precision.HIGHEST
    g, p, c, length = S5_GROUPS, S5_STATE, S5_GROUP, S5_CHUNK
    a_re, a_im = a_re.astype(F32), a_im.astype(F32)
    delta = jnp.exp(log_step.astype(F32))[:, None]
    tau = jnp.arange(length + 1, dtype=F32)[None, :, None]
    mag = jnp.exp((a_re * delta)[:, None, :] * tau)
    ang = (a_im * delta)[:, None, :] * tau
    pw_re, pw_im = mag * jnp.cos(ang), mag * jnp.sin(ang)
    n_re, n_im = pw_re[:, 1, :] - 1.0, pw_im[:, 1, :]
    den = a_re * a_re + a_im * a_im
    q_re, q_im = (n_re * a_re + n_im * a_im) / den, (n_im * a_re - n_re * a_im) / den
    bb_re = q_re[:, :, None] * b_re - q_im[:, :, None] * b_im
    bb_im = q_re[:, :, None] * b_im + q_im[:, :, None] * b_re
    first_re, first_im = pw_re[:, :length, None, :], pw_im[:, :length, None, :]
    cp_re = c_re[:, None] * first_re - c_im[:, None] * first_im
    cp_im = c_re[:, None] * first_im + c_im[:, None] * first_re
    kern = (jnp.einsum('gtcp,gpd->gdct', cp_re, bb_re, precision=hp)
            - jnp.einsum('gtcp,gpd->gdct', cp_im, bb_im, precision=hp)).reshape(g, c * c, length)
    rev_re, rev_im = pw_re[:, length - 1::-1, :][:, None], pw_im[:, length - 1::-1, :][:, None]
    bt_re, bt_im = bb_re.transpose(0, 2, 1)[:, :, None, :], bb_im.transpose(0, 2, 1)[:, :, None, :]
    f_re = (rev_re * bt_re - rev_im * bt_im).reshape(g, c * length, p)
    f_im = (rev_re * bt_im + rev_im * bt_re).reshape(g, c * length, p)
    nx_re = pw_re[:, 1:, :].transpose(0, 2, 1)[:, :, None, :]
    nx_im = pw_im[:, 1:, :].transpose(0, 2, 1)[:, :, None, :]
    ct_re, ct_im = c_re.transpose(0, 2, 1)[:, :, :, None], c_im.transpose(0, 2, 1)[:, :, :, None]
    e_re = (ct_re * nx_re - ct_im * nx_im).reshape(g, p, c * length)
    e_im = (ct_re * nx_im + ct_im * nx_re).reshape(g, p, c * length)
    pad = lambda z: jnp.pad(z, ((0, 0), (0, 0), (0, LANES_V7X - p)))
    pad_rows = lambda z: jnp.pad(z, ((0, 0), (0, LANES_V7X - p), (0, 0)))
    return (kern, pad(f_re).astype(BF16), pad(f_im).astype(BF16), pad_rows(e_re).astype(BF16),
            pad_rows(-e_im).astype(BF16), pad(pw_re[:, length:, :]), pad(pw_im[:, length:, :]))


def _s5(u_t, tables, batch):
    t = u_t.shape[1]
    length = S5_CHUNK
    rows = t // length
    width = length * S5_GROUP
    kern, fre, fim, ere, eim, lre, lim = tables
    grp = lambda r, c: pl.BlockSpec((1, r, c), lambda gi: (gi, 0, 0))
    io_spec = pl.BlockSpec((1, S5_GROUP, rows, length), lambda gi: (gi, 0, 0, 0))
    y = pl.pallas_call(
        functools.partial(_s5_kernel, batch=batch, chunks=rows // batch),
        grid=(S5_GROUPS,),
        in_specs=[io_spec, grp(S5_GROUP * S5_GROUP, length), grp(width, LANES_V7X), grp(width, LANES_V7X),
                  grp(LANES_V7X, width), grp(LANES_V7X, width), grp(1, LANES_V7X), grp(1, LANES_V7X)],
        out_specs=io_spec,
        out_shape=jax.ShapeDtypeStruct((S5_GROUPS, S5_GROUP, rows, length), F32),
        scratch_shapes=[pltpu.VMEM((width, width), BF16)] + [pltpu.VMEM((rows, LANES_V7X), F32)] * 4,
        compiler_params=_params("parallel"),
        name="s5",
    )(u_t.reshape(S5_GROUPS, S5_GROUP, rows, length), kern, fre, fim, ere, eim, lre, lim)
    return y.reshape(S5_WIDTH, t)


def _rotary_tables(seq):
    half = RET_QK_DIM // 2
    inv_freq = 1.0 / (ROPE_BASE ** jnp.linspace(0.0, 1.0, half, dtype=F32))
    ang = jnp.arange(seq, dtype=F32)[:, None] * inv_freq[None, :]
    cos, sin = jnp.cos(ang), jnp.sin(ang)
    cos_t = jnp.tile(jnp.concatenate([cos, cos], axis=1), (1, RET_HEADS))
    sin_t = jnp.tile(jnp.concatenate([-sin, sin], axis=1), (1, RET_HEADS))
    return cos_t, sin_t


def _row(v):
    return v.reshape(1, -1).astype(F32)


def kernel(x, norm_mix_g, norm_mlp_g, w_up, w_down, ab_w_in, ab_w_out, da_q_norm, da_k_norm,
           da_lam_q1, da_lam_k1, da_lam_q2, da_lam_k2, da_out_norm, ret_out_norm, cd_w_in, cd_w_out,
           s5_a_re, s5_a_im, s5_log_step, s5_b_re, s5_b_im, s5_c_re, s5_c_im, s5_d, s5_w_glu,
           gla_w_a2, gla_b_a2, gla_out_norm):
    bsz, seq, _ = x.shape
    t = bsz * seq
    x2 = x.reshape(t, D_MODEL)

    cos_t, sin_t = _rotary_tables(seq)
    gsum = ((jnp.arange(512)[:, None] // DA_QK_DIM) == (jnp.arange(512)[None, :] // DA_QK_DIM))
    gsum = (gsum.astype(F32) / DA_QK_DIM).astype(BF16)
    gq = _row(jnp.tile(da_q_norm[0], 2 * DA_HEADS)) * (DA_QK_DIM ** -0.5 * math.log2(math.e))
    gk = _row(jnp.tile(da_k_norm[0], 2 * DA_HEADS))
    w_ab = ab_w_in[0].astype(BF16)
    qa, ka, va_t, qr, kr, vr, gr = _ab_in(x2, _row(norm_mix_g[0]), w_ab, w_ab[:, 1024:1536].T, gq, gk, gsum,
                                          cos_t, sin_t, seq)
    lam_init = 0.8 - 0.6 * math.exp(-0.3 * 0)
    seq3 = lambda a: a.reshape(bsz, seq, a.shape[-1])
    o_a = _diff_attn(seq3(qa), seq3(ka), va_t, _row(da_lam_q1[0]), _row(da_lam_k1[0]),
                     _row(da_lam_q2[0]), _row(da_lam_k2[0]), _row(da_out_norm[0]), lam_init)
    o_r = _retention(seq3(qr), seq3(kr), seq3(vr), seq3(gr), _row(ret_out_norm[0]))
    x2 = _out_mlp(x2, o_a.reshape(t, DA_WIDTH), o_r.reshape(t, RET_WIDTH), ab_w_out[0].astype(BF16),
                  _row(norm_mlp_g[0]), w_up[0].astype(BF16), w_down[0].astype(BF16))

    w_cd = jnp.pad(cd_w_in[0], ((0, 0), (0, CD_IN_PADDED - CD_IN))).astype(BF16)
    wa = jnp.pad(gla_w_a2[0], ((0, LANES_V7X - GLA_GATE_RANK), (0, 0))).astype(BF16)
    u_t, qg, kg, vg, rg, la = _cd_in(x2, _row(norm_mix_g[1]), w_cd, w_cd[:, :S5_WIDTH].T, wa, _row(gla_b_a2[0]))
    o_d = _gla(seq3(qg), seq3(kg), seq3(vg), seq3(rg), seq3(la), _row(gla_out_norm[0]))
    tables = _s5_tables(s5_a_re[0], s5_a_im[0], s5_log_step[0], s5_b_re[0], s5_b_im[0], s5_c_re[0], s5_c_im[0])
    y_t = _s5(u_t, tables, bsz)
    x2 = _s5_out_mlp(x2, y_t, u_t, s5_d[0].reshape(S5_WIDTH, 1).astype(F32), s5_w_glu[0].T.astype(BF16),
                     o_d.reshape(t, GLA_WIDTH), cd_w_out[0].astype(BF16), _row(norm_mlp_g[1]),
                     w_up[1].astype(BF16), w_down[1].astype(BF16))
    return x2.reshape(bsz, seq, D_MODEL)
```

```python
import functools
import math

import jax
import jax.numpy as jnp
from jax import lax
from jax.experimental import pallas as pl
from jax.experimental.pallas import tpu as pltpu

F32 = jnp.float32
BF16 = jnp.bfloat16

D_MODEL = 1024
CHUNK = 64
RMS_EPS = 1e-6
ROPE_BASE = 10000.0
DA_HEADS = 4
DA_QK_DIM = 64
DA_V_DIM = 128
DA_WIDTH = DA_HEADS * DA_V_DIM
RET_HEADS = 4
RET_QK_DIM = 64
RET_V_DIM = 128
RET_WIDTH = RET_HEADS * RET_V_DIM
S5_WIDTH = 256
S5_GROUP = 16
S5_GROUPS = S5_WIDTH // S5_GROUP
S5_STATE = 64
GLA_HEADS = 6
GLA_QK_DIM = 64
GLA_V_DIM = 128
GLA_WIDTH = GLA_HEADS * GLA_V_DIM
GLA_QK_WIDTH = GLA_HEADS * GLA_QK_DIM
GLA_GATE_RANK = 16
GLA_TAU = 16.0
MLP_HIDDEN = 4 * D_MODEL
AB_IN = 3072
CD_IN = 2576

LANES_V7X = 128
VMEM_LIMIT_BYTES_V7X = 56 * 1024 * 1024

CD_IN_MAIN = CD_IN - GLA_GATE_RANK
GATE_ROWS = 32
NEG_BIG = -1e30

TOKEN_TILE = 512
ATTN_BLOCK = 512
ATTN_TILE = 256
RET_BLOCK = 512
GLA_BLOCK = 256
S5_CHUNK = LANES_V7X
MLP_HIDDEN_TILE = 1024


def _params(*semantics):
    return pltpu.CompilerParams(dimension_semantics=semantics, vmem_limit_bytes=VMEM_LIMIT_BYTES_V7X)


def _const_spec(shape):
    zeros = (0,) * len(shape)
    return pl.BlockSpec(shape, lambda *_: zeros, pipeline_mode=pl.Buffered(1))


def _rms(xf, gain):
    return xf * lax.rsqrt(jnp.mean(xf * xf, axis=-1, keepdims=True) + RMS_EPS) * gain


def _dot(a, b):
    return jnp.dot(a, b, preferred_element_type=F32)


def _dot_nt(a, b):
    return lax.dot_general(a, b, (((1,), (1,)), ((), ())), preferred_element_type=F32)


def _dot_tn(a, b):
    return lax.dot_general(a, b, (((0,), (0,)), ((), ())), preferred_element_type=F32)


def _swap_halves(x, group):
    n = x.shape[-1]
    half = group // 2
    lane = lax.broadcasted_iota(jnp.int32, x.shape, x.ndim - 1)
    from_right = pltpu.roll(x, n - half, axis=x.ndim - 1)
    from_left = pltpu.roll(x, half, axis=x.ndim - 1)
    return jnp.where((lane % group) < half, from_right, from_left)


def _ab_in_kernel(x_ref, g_ref, w_ref, wvt_ref, gq_ref, gk_ref, gsum_ref, cos_ref, sin_ref,
                  qa_ref, ka_ref, va_ref, qr_ref, kr_ref, vr_ref, gr_ref):
    h = _rms(x_ref[...], g_ref[...]).astype(BF16)

    def proj(lo, hi):
        return _dot(h, w_ref[:, lo:hi])

    def qk_norm(y, gain):
        ms = _dot((y * y).astype(BF16), gsum_ref[...])
        return (y * lax.rsqrt(ms + RMS_EPS) * gain).astype(BF16)

    def rotary(y):
        return y * cos_ref[...] + _swap_halves(y, RET_QK_DIM) * sin_ref[...]

    qa_ref[...] = qk_norm(proj(0, 512), gq_ref[...])
    ka_ref[...] = qk_norm(proj(512, 1024), gk_ref[...])
    va_ref[0] = _dot_nt(wvt_ref[...], h).astype(BF16)
    qr_ref[...] = rotary(proj(1536, 1792))
    kr_ref[...] = rotary(proj(1792, 2048)) * (RET_QK_DIM ** -0.5)
    vr_ref[...] = proj(2048, 2560).astype(BF16)
    gr_ref[...] = proj(2560, 3072)


def _ab_in(x2, g, w, wv_t, gq, gk, gsum, cos_t, sin_t, seq):
    t = x2.shape[0]
    tm = min(TOKEN_TILE, seq)
    per_seq = seq // tm
    row = lambda width: pl.BlockSpec((tm, width), lambda i: (i, 0))
    va_spec = pl.BlockSpec((1, DA_WIDTH, tm), lambda i: (i // per_seq, 0, i % per_seq))
    rot_spec = pl.BlockSpec((tm, 256), lambda i: (i % per_seq, 0))
    out_shapes = (
        jax.ShapeDtypeStruct((t, 512), BF16), jax.ShapeDtypeStruct((t, 512), BF16),
        jax.ShapeDtypeStruct((t // seq, DA_WIDTH, seq), BF16), jax.ShapeDtypeStruct((t, 256), F32),
        jax.ShapeDtypeStruct((t, 256), F32), jax.ShapeDtypeStruct((t, 512), BF16),
        jax.ShapeDtypeStruct((t, 512), F32))
    return pl.pallas_call(
        _ab_in_kernel,
        grid=(t // tm,),
        in_specs=[row(D_MODEL), _const_spec((1, D_MODEL)), _const_spec((D_MODEL, AB_IN)),
                  _const_spec((DA_WIDTH, D_MODEL)),
                  _const_spec((1, 512)), _const_spec((1, 512)), _const_spec((512, 512)),
                  rot_spec, rot_spec],
        out_specs=(row(512), row(512), va_spec, row(256), row(256), row(512), row(512)),
        out_shape=out_shapes,
        compiler_params=_params("parallel"),
        name="ab_in_proj",
    )(x2, g, w, wv_t, gq, gk, gsum, cos_t, sin_t)


def _diff_attn_kernel(lq1_ref, lk1_ref, lq2_ref, lk2_ref, gout_ref, q_ref, k_ref, vt_ref, o_ref,
                      sa_ref, sb_ref, sc_ref, mxa_ref, mxb_ref, mxc_ref, qza_ref, qzb_ref, m_ref, l_ref, acc_ref,
                      *, blk, nblk, lam_init):
    lam = (jnp.exp(jnp.sum(lq1_ref[...] * lk1_ref[...], axis=-1, keepdims=True))
           - jnp.exp(jnp.sum(lq2_ref[...] * lk2_ref[...], axis=-1, keepdims=True)) + lam_init)
    gain = gout_ref[...] * (1.0 - lam_init)
    buf_a, buf_b, buf_c = (sa_ref, mxa_ref), (sb_ref, mxb_ref), (sc_ref, mxc_ref)

    def load_queries(qb, qz_ref):
        start = pl.multiple_of(qb * blk, blk)
        qt = q_ref[0, pl.ds(start, blk), :].astype(F32).T
        dim = lax.broadcasted_iota(jnp.int32, qt.shape, 0)
        qz_ref[...] = jnp.concatenate([jnp.where(dim < DA_QK_DIM, qt, 0.0), jnp.where(dim >= DA_QK_DIM, qt, 0.0)],
                                      axis=1).astype(BF16)

    def score(qz_ref, t, buf):
        s_ref, mx_ref = buf
        start = pl.multiple_of(t * blk, blk)
        st = _dot(k_ref[0, pl.ds(start, blk), :], qz_ref[...])
        s_ref[...] = st
        mx_ref[...] = jnp.max(st, axis=0, keepdims=True)

    def absorb(t, buf, diagonal=False):
        s_ref, mx_ref = buf
        start = pl.multiple_of(t * blk, blk)
        tile = ATTN_TILE
        for qg in range(2 * blk // tile):
            cs = slice(qg * tile, (qg + 1) * tile)

            def scores(kh):
                st = s_ref[kh * tile:(kh + 1) * tile, cs]
                if diagonal:
                    key = lax.broadcasted_iota(jnp.int32, st.shape, 0) + kh * tile
                    qry = lax.broadcasted_iota(jnp.int32, st.shape, 1) + (qg * tile) % blk
                    st = jnp.where(key // CHUNK <= qry // CHUNK, st, NEG_BIG)
                return st

            if diagonal:
                mx = functools.reduce(jnp.maximum, [jnp.max(scores(kh), axis=0, keepdims=True)
                                                    for kh in range(blk // tile)])
            else:
                mx = mx_ref[:, cs]
            m = m_ref[:, cs]
            m_new = jnp.maximum(m, mx)
            alpha = jnp.exp2(m - m_new)
            l = alpha * l_ref[:, cs]
            acc = alpha * acc_ref[:, cs]
            for kh in range(blk // tile):
                p = jnp.exp2(scores(kh) - m_new)
                l = l + jnp.sum(p, axis=0, keepdims=True)
                acc = acc + _dot(vt_ref[0, :, pl.ds(start + kh * tile, tile)], p.astype(BF16))
            m_ref[:, cs] = m_new
            l_ref[:, cs] = l
            acc_ref[:, cs] = acc

    def repeat(first, count, two_stages):
        def twice(v, c):
            two_stages(first + 2 * v)
            two_stages(first + 2 * v + 1)
            return c

        lax.fori_loop(0, lax.shift_right_logical(count, 1), twice, 0)

        @pl.when((count & 1) == 1)
        def _():
            two_stages(first + count - 1)

    def reset():
        m_ref[...] = jnp.full(m_ref.shape, NEG_BIG, F32)
        l_ref[...] = jnp.zeros(l_ref.shape, F32)
        acc_ref[...] = jnp.zeros(acc_ref.shape, F32)

    def finish(qb):
        o = acc_ref[...] / l_ref[...]
        o = o[:, :blk] - lam * o[:, blk:]
        o = o * lax.rsqrt(jnp.mean(o * o, axis=0, keepdims=True) + RMS_EPS)
        o_ref[0, pl.ds(pl.multiple_of(qb * blk, blk), blk), :] = (o.T * gain).astype(BF16)
        reset()

    reset()
    load_queries(0, qza_ref)
    load_queries(1, qzb_ref)
    score(qza_ref, 0, buf_a)
    score(qzb_ref, 0, buf_b)
    absorb(0, buf_a, diagonal=True)
    finish(0)
    score(qzb_ref, 1, buf_a)
    absorb(0, buf_b)
    load_queries(min(2, nblk - 1), qza_ref)
    score(qza_ref, 0, buf_c)
    absorb(1, buf_a, diagonal=True)
    finish(1)

    def pair(g, carry):
        a = 2 * g
        load_queries(a + 1, qzb_ref)
        score(qza_ref, 1, buf_b)
        absorb(0, buf_c)
        score(qza_ref, 2, buf_a)
        absorb(1, buf_b)

        def stages_a(u):
            score(qza_ref, 2 * u + 1, buf_b)
            absorb(2 * u, buf_a)
            score(qza_ref, 2 * u + 2, buf_a)
            absorb(2 * u + 1, buf_b)

        repeat(1, g - 1, stages_a)
        score(qzb_ref, 0, buf_b)
        absorb(a, buf_a, diagonal=True)
        finish(a)
        score(qzb_ref, 1, buf_a)
        absorb(0, buf_b)

        def stages_b(u):
            score(qzb_ref, 2 * u + 2, buf_b)
            absorb(2 * u + 1, buf_a)
            score(qzb_ref, 2 * u + 3, buf_a)
            absorb(2 * u + 2, buf_b)

        repeat(0, g, stages_b)
        load_queries(jnp.minimum(a + 2, nblk - 1), qza_ref)
        score(qza_ref, 0, buf_c)
        absorb(a + 1, buf_a, diagonal=True)
        finish(a + 1)
        return carry

    lax.fori_loop(1, nblk // 2, pair, 0)


def _diff_attn(qa, ka, va_t, lq1, lk1, lq2, lk2, gout, lam_init):
    b, s, _ = qa.shape
    blk = min(ATTN_BLOCK, s // 2)
    nblk = s // blk
    assert nblk % 2 == 0 and blk % ATTN_TILE == 0, (s, blk)
    vec = _const_spec((1, DA_QK_DIM))
    tok_spec = pl.BlockSpec((1, s, 128), lambda bi, h: (bi, 0, h))
    vt_spec = pl.BlockSpec((1, DA_V_DIM, s), lambda bi, h: (bi, h, 0))
    scores = pltpu.VMEM((blk, 2 * blk), F32)
    stats = pltpu.VMEM((1, 2 * blk), F32)
    queries = pltpu.VMEM((2 * DA_QK_DIM, 2 * blk), BF16)
    return pl.pallas_call(
        functools.partial(_diff_attn_kernel, blk=blk, nblk=nblk, lam_init=lam_init),
        grid=(b, DA_HEADS),
        in_specs=[vec, vec, vec, vec, _const_spec((1, DA_V_DIM)), tok_spec, tok_spec, vt_spec],
        out_specs=tok_spec,
        out_shape=jax.ShapeDtypeStruct((b, s, DA_WIDTH), BF16),
        scratch_shapes=[scores, scores, scores, stats, stats, stats, queries, queries, stats, stats,
                        pltpu.VMEM((DA_V_DIM, 2 * blk), F32)],
        compiler_params=_params("parallel", "parallel"),
        name="diff_attention",
    )(lq1, lk1, lq2, lk2, gout, qa, ka, va_t)


def _retention_kernel(dec_ref, qdec_ref, kdec_ref, sdec_ref, smask_ref, gain_ref,
                      q_ref, k_ref, v_ref, g_ref, o_ref, state_ref):
    @pl.when(pl.program_id(1) == 0)
    def _():
        state_ref[...] = jnp.zeros_like(state_ref)

    q = q_ref[0]
    k = k_ref[0]
    v = v_ref[0]
    kb = k.astype(BF16)
    lane = lax.broadcasted_iota(jnp.int32, q.shape, 1)
    state = state_ref[...]
    o_cross = _dot((q * qdec_ref[...]).astype(BF16), state.astype(BF16))
    for h in range(RET_HEADS):
        qh = jnp.where(lane // RET_QK_DIM == h, q, 0.0).astype(BF16)
        w = (_dot_nt(qh, kb) * dec_ref[h]).astype(BF16)
        vs = slice(h * RET_V_DIM, (h + 1) * RET_V_DIM)
        o = _dot(w, v[:, vs]) + o_cross[:, vs]
        o = _rms(o, gain_ref[...])
        g = g_ref[0, :, vs]
        o_ref[0, :, vs] = (o * (g * jax.nn.sigmoid(g))).astype(BF16)
    kv = _dot_tn((k * kdec_ref[...]).astype(BF16), v)
    state_ref[...] = sdec_ref[...] * state + kv * smask_ref[...]


def _retention_tables(blk):
    heads = jnp.arange(RET_HEADS, dtype=F32)
    log_gamma = jnp.log(1.0 - 2.0 ** (-5.0 - heads))
    idx = jnp.arange(blk)
    diff = (idx[:, None] - idx[None, :]).astype(F32)
    same_chunk = (idx[:, None] // CHUNK) == (idx[None, :] // CHUNK)
    visible = (idx[None, :] <= idx[:, None]) | same_chunk
    dec = jnp.where(visible[None], jnp.exp(log_gamma[:, None, None] * jnp.abs(diff)[None]), 0.0)
    per_lane = jnp.repeat(log_gamma, RET_QK_DIM)
    pos = jnp.arange(blk, dtype=F32)
    qdec = jnp.exp((pos[:, None] + 1.0) * per_lane[None, :])
    kdec = jnp.exp((blk - 1.0 - pos)[:, None] * per_lane[None, :])
    sdec = jnp.exp(blk * per_lane)[:, None]
    smask = (jnp.arange(RET_HEADS * RET_QK_DIM)[:, None] // RET_QK_DIM
             == jnp.arange(RET_WIDTH)[None, :] // RET_V_DIM).astype(F32)
    return dec, qdec, kdec, sdec, smask


def _retention(qr, kr, vr, gr, gain):
    b, s, _ = qr.shape
    blk = min(RET_BLOCK, s)
    dec, qdec, kdec, sdec, smask = _retention_tables(blk)
    tok = lambda width: pl.BlockSpec((1, blk, width), lambda bi, i: (bi, i, 0))
    return pl.pallas_call(
        _retention_kernel,
        grid=(b, s // blk),
        in_specs=[_const_spec(dec.shape), _const_spec(qdec.shape), _const_spec(kdec.shape),
                  _const_spec(sdec.shape), _const_spec(smask.shape), _const_spec((1, RET_V_DIM)),
                  tok(256), tok(256), tok(512), tok(512)],
        out_specs=tok(RET_WIDTH),
        out_shape=jax.ShapeDtypeStruct((b, s, RET_WIDTH), BF16),
        scratch_shapes=[pltpu.VMEM((RET_HEADS * RET_QK_DIM, RET_WIDTH), F32)],
        compiler_params=_params("parallel", "arbitrary"),
        name="retention",
    )(dec, qdec, kdec, sdec, smask, gain, qr, kr, vr, gr)


def _mlp_tail(x1, g_ref, wup_ref, wdn_ref, o_ref):
    h = _rms(x1, g_ref[...]).astype(BF16)
    acc = x1
    for c in range(MLP_HIDDEN // MLP_HIDDEN_TILE):
        cs = slice(c * MLP_HIDDEN_TILE, (c + 1) * MLP_HIDDEN_TILE)
        u = jnp.maximum(_dot(h, wup_ref[:, cs]), 0.0)
        acc = acc + _dot((u * u).astype(BF16), wdn_ref[cs, :])
    o_ref[...] = acc


def _out_mlp_kernel(x_ref, a_ref, b_ref, wout_ref, g_ref, wup_ref, wdn_ref, o_ref):
    ka = a_ref.shape[1]
    mixed = _dot(a_ref[...], wout_ref[:ka, :]) + _dot(b_ref[...], wout_ref[ka:, :])
    _mlp_tail(x_ref[...] + mixed, g_ref, wup_ref, wdn_ref, o_ref)


def _s5_out_mlp_kernel(x_ref, yt_ref, ut_ref, d_ref, wglut_ref, b_ref, wout_ref, g_ref, wup_ref, wdn_ref, o_ref):
    y = yt_ref[...] + d_ref[...] * ut_ref[...]
    z = jax.nn.gelu(y)
    a_t = (z * jax.nn.sigmoid(_dot(wglut_ref[...], z.astype(BF16)))).astype(BF16)
    mixed = _dot_tn(a_t, wout_ref[:S5_WIDTH, :]) + _dot(b_ref[...], wout_ref[S5_WIDTH:, :])
    _mlp_tail(x_ref[...] + mixed, g_ref, wup_ref, wdn_ref, o_ref)


def _mlp_specs():
    return [_const_spec((D_MODEL, D_MODEL)), _const_spec((1, D_MODEL)),
            _const_spec((D_MODEL, MLP_HIDDEN)), _const_spec((MLP_HIDDEN, D_MODEL))]


def _out_mlp(x2, a, bb, wout, g, wup, wdn):
    t = x2.shape[0]
    tm = TOKEN_TILE
    row = lambda width: pl.BlockSpec((tm, width), lambda i: (i, 0))
    return pl.pallas_call(
        _out_mlp_kernel,
        grid=(t // tm,),
        in_specs=[row(D_MODEL), row(a.shape[1]), row(bb.shape[1])] + _mlp_specs(),
        out_specs=row(D_MODEL),
        out_shape=jax.ShapeDtypeStruct((t, D_MODEL), F32),
        compiler_params=_params("parallel"),
        name="ab_out_mlp",
    )(x2, a, bb, wout, g, wup, wdn)


def _s5_out_mlp(x2, y_t, u_t, d_col, wglu_t, bb, wout, g, wup, wdn):
    t = x2.shape[0]
    tm = TOKEN_TILE
    row = lambda width: pl.BlockSpec((tm, width), lambda i: (i, 0))
    col = pl.BlockSpec((S5_WIDTH, tm), lambda i: (0, i))
    return pl.pallas_call(
        _s5_out_mlp_kernel,
        grid=(t // tm,),
        in_specs=[row(D_MODEL), col, col, _const_spec((S5_WIDTH, 1)),
                  _const_spec((S5_WIDTH, S5_WIDTH)), row(GLA_WIDTH)] + _mlp_specs(),
        out_specs=row(D_MODEL),
        out_shape=jax.ShapeDtypeStruct((t, D_MODEL), F32),
        compiler_params=_params("parallel"),
        name="cd_out_mlp",
    )(x2, y_t, u_t, d_col, wglu_t, bb, wout, g, wup, wdn)


def _cd_in_kernel(x_ref, g_ref, w_ref, wut_ref, wa_ref, ba_ref, ut_ref, q_ref, k_ref, v_ref, r_ref, la_ref):
    h = _rms(x_ref[...], g_ref[...]).astype(BF16)

    def proj(lo, hi):
        return _dot(h, w_ref[:, lo:hi])

    ua_t = _dot_nt(wut_ref[...], h)
    ut_ref[...] = ua_t[:S5_WIDTH]
    q_ref[...] = proj(256, 640) * (GLA_QK_DIM ** -0.5)
    k_ref[...] = proj(640, 1024)
    v_ref[...] = proj(1024, 1792).astype(BF16)
    r_ref[...] = proj(1792, 2560)
    pre = _dot_tn(ua_t[S5_WIDTH:].astype(BF16), wa_ref[...]) + ba_ref[...]
    log_sig = jnp.minimum(pre, 0.0) - jnp.log1p(jnp.exp(-jnp.abs(pre)))
    la_ref[...] = log_sig / GLA_TAU


def _cd_in(x2, g, w, wu_t, wa, ba):
    t = x2.shape[0]
    tm = TOKEN_TILE
    row = lambda width: pl.BlockSpec((tm, width), lambda i: (i, 0))
    out_shapes = (
        jax.ShapeDtypeStruct((S5_WIDTH, t), F32), jax.ShapeDtypeStruct((t, GLA_QK_WIDTH), F32),
        jax.ShapeDtypeStruct((t, GLA_QK_WIDTH), F32), jax.ShapeDtypeStruct((t, GLA_WIDTH), BF16),
        jax.ShapeDtypeStruct((t, GLA_WIDTH), F32), jax.ShapeDtypeStruct((t, GLA_QK_WIDTH), F32))
    return pl.pallas_call(
        _cd_in_kernel,
        grid=(t // tm,),
        in_specs=[row(D_MODEL), _const_spec((1, D_MODEL)), _const_spec((D_MODEL, CD_IN_MAIN)),
                  _const_spec((S5_WIDTH + GATE_ROWS, D_MODEL)),
                  _const_spec((GATE_ROWS, GLA_QK_WIDTH)), _const_spec((1, GLA_QK_WIDTH))],
        out_specs=(pl.BlockSpec((S5_WIDTH, tm), lambda i: (0, i)), row(GLA_QK_WIDTH), row(GLA_QK_WIDTH),
                   row(GLA_WIDTH), row(GLA_WIDTH), row(GLA_QK_WIDTH)),
        out_shape=out_shapes,
        compiler_params=_params("parallel"),
        name="cd_in_proj",
    )(x2, g, w, wu_t, wa, ba)


def _split3(x):
    hi = x.astype(BF16)
    r = x - hi.astype(F32)
    mid = r.astype(BF16)
    lo = (r - mid.astype(F32)).astype(BF16)
    return hi, mid, lo


def _gla_kernel(tri_ref, gain_ref, q_ref, k_ref, v_ref, r_ref, la_ref, o_ref, state_ref, *, blk, batch):
    @pl.when(pl.program_id(0) == 0)
    def _():
        state_ref[...] = jnp.zeros_like(state_ref)

    tri = tri_ref[...]
    hi, mid, lo = _split3(jnp.concatenate([la_ref[bi] for bi in range(batch)], axis=1))
    b_all = _dot(tri, hi) + _dot(tri, mid) + _dot(tri, lo)
    for bi in range(batch):
        _gla_block(bi, b_all[:, bi * GLA_QK_WIDTH:(bi + 1) * GLA_QK_WIDTH], gain_ref, q_ref, k_ref, v_ref, r_ref,
                   o_ref, state_ref, blk)


def _gla_block(bi, b, gain_ref, q_ref, k_ref, v_ref, r_ref, o_ref, state_ref, blk):
    nc = blk // CHUNK
    pairs = GLA_HEADS // 2
    q = q_ref[bi]
    k = k_ref[bi]
    b3 = b.reshape(nc, CHUNK, GLA_QK_WIDTH)
    b_last = b3[:, CHUNK - 1:CHUNK, :]
    e_pos = jnp.exp(b)
    e_neg = jnp.exp(-b)
    k_tail = jnp.exp(b_last - b3).reshape(blk, GLA_QK_WIDTH)
    chunk_decay = jnp.exp(b_last)
    qp = (q * e_pos).astype(BF16)
    qn = (q * e_neg).astype(BF16)
    kp = (k * e_pos).astype(BF16)
    kn = (k * e_neg).astype(BF16)
    kw = (k * k_tail).astype(BF16)

    row = lax.broadcasted_iota(jnp.int32, (blk, blk), 0)
    col = lax.broadcasted_iota(jnp.int32, (blk, blk), 1)
    same_chunk = row // CHUNK == col // CHUNK
    causal = row >= col
    lane = lax.broadcasted_iota(jnp.int32, (blk, 2 * GLA_QK_DIM), 1)
    smask = (lax.broadcasted_iota(jnp.int32, (2 * GLA_V_DIM, 2 * GLA_QK_DIM), 0) // GLA_V_DIM
             == lax.broadcasted_iota(jnp.int32, (2 * GLA_V_DIM, 2 * GLA_QK_DIM), 1) // GLA_QK_DIM)

    for p in range(pairs):
        ks = slice(p * 2 * GLA_QK_DIM, (p + 1) * 2 * GLA_QK_DIM)
        vs = slice(p * 2 * GLA_V_DIM, (p + 1) * 2 * GLA_V_DIM)
        v = v_ref[bi, :, vs]
        scores = []
        for half in range(2):
            sel = (lane // GLA_QK_DIM) == half
            fwd = _dot_nt(jnp.where(sel, qp[:, ks], 0).astype(BF16), kn[:, ks])
            bwd = _dot_nt(jnp.where(sel, qn[:, ks], 0).astype(BF16), kp[:, ks])
            scores.append(jnp.where(same_chunk, jnp.where(causal, fwd, bwd), 0.0).astype(BF16))
        vlane = lax.broadcasted_iota(jnp.int32, v.shape, 1) // GLA_V_DIM
        v_diag = jnp.concatenate([jnp.where(vlane == 0, v, 0), jnp.where(vlane == 1, v, 0)], axis=0)
        o_intra = _dot(jnp.concatenate(scores, axis=1), v_diag)

        state = state_ref[bi * pairs + p]
        cross = []
        for c in range(nc):
            rs = slice(c * CHUNK, (c + 1) * CHUNK)
            cross.append(_dot_nt(qp[rs, ks], state.astype(BF16)))
            kv_t = _dot_tn(v[rs, :], kw[rs, ks])
            state = state * chunk_decay[c][:, ks] + jnp.where(smask, kv_t, 0.0)
        state_ref[bi * pairs + p] = state
        o = o_intra + jnp.concatenate(cross, axis=0)
        for half in range(2):
            hs = slice(half * GLA_V_DIM, (half + 1) * GLA_V_DIM)
            os_ = slice(p * 2 * GLA_V_DIM + half * GLA_V_DIM, p * 2 * GLA_V_DIM + (half + 1) * GLA_V_DIM)
            g = r_ref[bi, :, os_]
            o_ref[bi, :, os_] = (_rms(o[:, hs], gain_ref[...]) * (g * jax.nn.sigmoid(g))).astype(BF16)


def _gla(qg, kg, vg, rg, la, gain):
    b, s, _ = qg.shape
    blk = min(GLA_BLOCK, s)
    idx = jnp.arange(blk)
    tri = (((idx[:, None] // CHUNK) == (idx[None, :] // CHUNK)) & (idx[None, :] <= idx[:, None])).astype(BF16)
    tok = lambda width: pl.BlockSpec((b, blk, width), lambda i: (0, i, 0))
    return pl.pallas_call(
        functools.partial(_gla_kernel, blk=blk, batch=b),
        grid=(s // blk,),
        in_specs=[_const_spec((blk, blk)), _const_spec((1, GLA_V_DIM)),
                  tok(GLA_QK_WIDTH), tok(GLA_QK_WIDTH), tok(GLA_WIDTH), tok(GLA_WIDTH), tok(GLA_QK_WIDTH)],
        out_specs=tok(GLA_WIDTH),
        out_shape=jax.ShapeDtypeStruct((b, s, GLA_WIDTH), BF16),
        scratch_shapes=[pltpu.VMEM((b * (GLA_HEADS // 2), 2 * GLA_V_DIM, 2 * GLA_QK_DIM), F32)],
        compiler_params=_params("arbitrary"),
        name="gla",
    )(tri, gain, qg, kg, vg, rg, la)


def _s5_kernel(u_ref, kern_ref, fre_ref, fim_ref, ere_ref, eim_ref, lre_ref, lim_ref, y_ref,
               toep_ref, vre_ref, vim_ref, hre_ref, him_ref, *, batch, chunks):
    length = S5_CHUNK
    src = lax.broadcasted_iota(jnp.int32, (length, length), 0)
    dst = lax.broadcasted_iota(jnp.int32, (length, length), 1)

    def build(ci, carry):
        for co in range(S5_GROUP):
            lags = kern_ref[0, pl.ds(ci * S5_GROUP + co, 1), :]
            blk = pltpu.roll(jnp.broadcast_to(lags, (length, length)), 0, 1, stride=1, stride_axis=0)
            blk = jnp.where(dst >= src, blk, 0.0)
            toep_ref[pl.ds(pl.multiple_of(ci * length, length), length), co * length:(co + 1) * length] = (
                blk.astype(BF16))
        return carry

    lax.fori_loop(0, S5_GROUP, build, 0)

    u = jnp.concatenate([u_ref[0, c] for c in range(S5_GROUP)], axis=1).astype(BF16)
    vre_ref[...] = _dot(u, fre_ref[0])
    vim_ref[...] = _dot(u, fim_ref[0])
    lre = lre_ref[0]
    lim = lim_ref[0]

    def step(n, carry):
        new = []
        for bi in range(batch):
            hr, hi = carry[2 * bi], carry[2 * bi + 1]
            r = bi * chunks + n
            hre_ref[pl.ds(r, 1), :] = hr
            him_ref[pl.ds(r, 1), :] = hi
            vr = vre_ref[pl.ds(r, 1), :]
            vi = vim_ref[pl.ds(r, 1), :]
            new += [lre * hr - lim * hi + vr, lre * hi + lim * hr + vi]
        return tuple(new)

    zero = jnp.zeros((1, LANES_V7X), F32)
    lax.fori_loop(0, chunks, step, (zero,) * (2 * batch))
    y = (_dot(u, toep_ref[...]) + _dot(hre_ref[...].astype(BF16), ere_ref[0])
         + _dot(him_ref[...].astype(BF16), eim_ref[0]))
    for c in range(S5_GROUP):
        y_ref[0, c] = y[:, c * length:(c + 1) * length]


def _s5_tables(a_re, a_im, log_step, b_re, b_im, c_re, c_im):
    hp = lax.Precision.HIGHEST
    g, p, c, length = S5_GROUPS, S5_STATE, S5_GROUP, S5_CHUNK
    a_re, a_im = a_re.astype(F32), a_im.astype(F32)
    delta = jnp.exp(log_step.astype(F32))[:, None]
    tau = jnp.arange(length + 1, dtype=F32)[None, :, None]
    mag = jnp.exp((a_re * delta)[:, None, :] * tau)
    ang = (a_im * delta)[:, None, :] * tau
    pw_re, pw_im = mag * jnp.cos(ang), mag * jnp.sin(ang)
    n_re, n_im = pw_re[:, 1, :] - 1.0, pw_im[:, 1, :]
    den = a_re * a_re + a_im * a_im
    q_re, q_im = (n_re * a_re + n_im * a_im) / den, (n_im * a_re - n_re * a_im) / den
    bb_re = q_re[:, :, None] * b_re - q_im[:, :, None] * b_im
    bb_im = q_re[:, :, None] * b_im + q_im[:, :, None] * b_re
    first_re, first_im = pw_re[:, :length, None, :], pw_im[:, :length, None, :]
    cp_re = c_re[:, None] * first_re - c_im[:, None] * first_im
    cp_im = c_re[:, None] * first_im + c_im[:, None] * first_re
    kern = (jnp.einsum('gtcp,gpd->gdct', cp_re, bb_re, precision=hp)
            - jnp.einsum('gtcp,gpd->gdct', cp_im, bb_im, precision=hp)).reshape(g, c * c, length)
    rev_re, rev_im = pw_re[:, length - 1::-1, :][:, None], pw_im[:, length - 1::-1, :][:, None]
    bt_re, bt_im = bb_re.transpose(0, 2, 1)[:, :, None, :], bb_im.transpose(0, 2, 1)[:, :, None, :]
    f_re = (rev_re * bt_re - rev_im * bt_im).reshape(g, c * length, p)
    f_im = (rev_re * bt_im + rev_im * bt_re).reshape(g, c * length, p)
    nx_re = pw_re[:, 1:, :].transpose(0, 2, 1)[:, :, None, :]
    nx_im = pw_im[:, 1:, :].transpose(0, 2, 1)[:, :, None, :]
    ct_re, ct_im = c_re.transpose(0, 2, 1)[:, :, :, None], c_im.transpose(0, 2, 1)[:, :, :, None]
    e_re = (ct_re * nx_re - ct_im * nx_im).reshape(g, p, c * length)
    e_im = (ct_re * nx_im + ct_im * nx_re).reshape(g, p, c * length)
    pad = lambda z: jnp.pad(z, ((0, 0), (0, 0), (0, LANES_V7X - p)))
    pad_rows = lambda z: jnp.pad(z, ((0, 0), (0, LANES_V7X - p), (0, 0)))
    return (kern, pad(f_re).astype(BF16), pad(f_im).astype(BF16), pad_rows(e_re).astype(BF16),
            pad_rows(-e_im).astype(BF16), pad(pw_re[:, length:, :]), pad(pw_im[:, length:, :]))


def _s5(u_t, tables, batch):
    t = u_t.shape[1]
    length = S5_CHUNK
    rows = t // length
    width = length * S5_GROUP
    kern, fre, fim, ere, eim, lre, lim = tables
    grp = lambda r, c: pl.BlockSpec((1, r, c), lambda gi: (gi, 0, 0))
    io_spec = pl.BlockSpec((1, S5_GROUP, rows, length), lambda gi: (gi, 0, 0, 0))
    y = pl.pallas_call(
        functools.partial(_s5_kernel, batch=batch, chunks=rows // batch),
        grid=(S5_GROUPS,),
        in_specs=[io_spec, grp(S5_GROUP * S5_GROUP, length), grp(width, LANES_V7X), grp(width, LANES_V7X),
                  grp(LANES_V7X, width), grp(LANES_V7X, width), grp(1, LANES_V7X), grp(1, LANES_V7X)],
        out_specs=io_spec,
        out_shape=jax.ShapeDtypeStruct((S5_GROUPS, S5_GROUP, rows, length), F32),
        scratch_shapes=[pltpu.VMEM((width, width), BF16)] + [pltpu.VMEM((rows, LANES_V7X), F32)] * 4,
        compiler_params=_params("parallel"),
        name="s5",
    )(u_t.reshape(S5_GROUPS, S5_GROUP, rows, length), kern, fre, fim, ere, eim, lre, lim)
    return y.reshape(S5_WIDTH, t)


def _rotary_tables(seq):
    half = RET_QK_DIM // 2
    inv_freq = 1.0 / (ROPE_BASE ** jnp.linspace(0.0, 1.0, half, dtype=F32))
    ang = jnp.arange(seq, dtype=F32)[:, None] * inv_freq[None, :]
    cos, sin = jnp.cos(ang), jnp.sin(ang)
    cos_t = jnp.tile(jnp.concatenate([cos, cos], axis=1), (1, RET_HEADS))
    sin_t = jnp.tile(jnp.concatenate([-sin, sin], axis=1), (1, RET_HEADS))
    return cos_t, sin_t


def _row(v):
    return v.reshape(1, -1).astype(F32)


def kernel(x, norm_mix_g, norm_mlp_g, w_up, w_down, ab_w_in, ab_w_out, da_q_norm, da_k_norm,
           da_lam_q1, da_lam_k1, da_lam_q2, da_lam_k2, da_out_norm, ret_out_norm, cd_w_in, cd_w_out,
           s5_a_re, s5_a_im, s5_log_step, s5_b_re, s5_b_im, s5_c_re, s5_c_im, s5_d, s5_w_glu,
           gla_w_a2, gla_b_a2, gla_out_norm):
    bsz, seq, _ = x.shape
    t = bsz * seq
    x2 = x.reshape(t, D_MODEL)

    cos_t, sin_t = _rotary_tables(seq)
    gsum = ((jnp.arange(512)[:, None] // DA_QK_DIM) == (jnp.arange(512)[None, :] // DA_QK_DIM))
    gsum = (gsum.astype(F32) / DA_QK_DIM).astype(BF16)
    gq = _row(jnp.tile(da_q_norm[0], 2 * DA_HEADS)) * (DA_QK_DIM ** -0.5 * math.log2(math.e))
    gk = _row(jnp.tile(da_k_norm[0], 2 * DA_HEADS))
    w_ab = ab_w_in[0].astype(BF16)
    qa, ka, va_t, qr, kr, vr, gr = _ab_in(x2, _row(norm_mix_g[0]), w_ab, w_ab[:, 1024:1536].T, gq, gk, gsum,
                                          cos_t, sin_t, seq)
    lam_init = 0.8 - 0.6 * math.exp(-0.3 * 0)
    seq3 = lambda a: a.reshape(bsz, seq, a.shape[-1])
    o_a = _diff_attn(seq3(qa), seq3(ka), va_t, _row(da_lam_q1[0]), _row(da_lam_k1[0]),
                     _row(da_lam_q2[0]), _row(da_lam_k2[0]), _row(da_out_norm[0]), lam_init)
    o_r = _retention(seq3(qr), seq3(kr), seq3(vr), seq3(gr), _row(ret_out_norm[0]))
    x2 = _out_mlp(x2, o_a.reshape(t, DA_WIDTH), o_r.reshape(t, RET_WIDTH), ab_w_out[0].astype(BF16),
                  _row(norm_mlp_g[0]), w_up[0].astype(BF16), w_down[0].astype(BF16))

    w_cd = cd_w_in[0].astype(BF16)
    gate_pad = GATE_ROWS - GLA_GATE_RANK
    wua_t = jnp.pad(jnp.concatenate([w_cd[:, :S5_WIDTH], w_cd[:, CD_IN_MAIN:]], axis=1).T, ((0, gate_pad), (0, 0)))
    wa = jnp.pad(gla_w_a2[0], ((0, gate_pad), (0, 0))).astype(BF16)
    u_t, qg, kg, vg, rg, la = _cd_in(x2, _row(norm_mix_g[1]), w_cd[:, :CD_IN_MAIN], wua_t, wa, _row(gla_b_a2[0]))
    o_d = _gla(seq3(qg), seq3(kg), seq3(vg), seq3(rg), seq3(la), _row(gla_out_norm[0]))
    tables = _s5_tables(s5_a_re[0], s5_a_im[0], s5_log_step[0], s5_b_re[0], s5_b_im[0], s5_c_re[0], s5_c_im[0])
    y_t = _s5(u_t, tables, bsz)
    x2 = _s5_out_mlp(x2, y_t, u_t, s5_d[0].reshape(S5_WIDTH, 1).astype(F32), s5_w_glu[0].T.astype(BF16),
                     o_d.reshape(t, GLA_WIDTH), cd_w_out[0].astype(BF16), _row(norm_mlp_g[1]),
                     w_up[1].astype(BF16), w_down[1].astype(BF16))
    return x2.reshape(bsz, seq, D_MODEL)
```

```python
import functools
import math

import jax
import jax.numpy as jnp
from jax import lax
from jax.experimental import pallas as pl
from jax.experimental.pallas import tpu as pltpu

F32 = jnp.float32
BF16 = jnp.bfloat16

D_MODEL = 1024
CHUNK = 64
RMS_EPS = 1e-6
ROPE_BASE = 10000.0
DA_HEADS = 4
DA_QK_DIM = 64
DA_V_DIM = 128
DA_WIDTH = DA_HEADS * DA_V_DIM
DA_V_ROWS = DA_V_DIM + 16
RET_HEADS = 4
RET_QK_DIM = 64
RET_V_DIM = 128
RET_WIDTH = RET_HEADS * RET_V_DIM
S5_WIDTH = 256
S5_GROUP = 16
S5_GROUPS = S5_WIDTH // S5_GROUP
S5_STATE = 64
GLA_HEADS = 6
GLA_QK_DIM = 64
GLA_V_DIM = 128
GLA_WIDTH = GLA_HEADS * GLA_V_DIM
GLA_QK_WIDTH = GLA_HEADS * GLA_QK_DIM
GLA_GATE_RANK = 16
GLA_TAU = 16.0
MLP_HIDDEN = 4 * D_MODEL
AB_IN = 3072
CD_IN = 2576

LANES_V7X = 128
VMEM_LIMIT_BYTES_V7X = 56 * 1024 * 1024

CD_IN_MAIN = CD_IN - GLA_GATE_RANK
GATE_ROWS = 32
NEG_BIG = -1e30

TOKEN_TILE = 512
ATTN_BLOCK = 512
ATTN_TILE = 256
RET_BLOCK = 512
GLA_BLOCK = 256
S5_CHUNK = LANES_V7X
MLP_HIDDEN_TILE = 1024


def _params(*semantics):
    return pltpu.CompilerParams(dimension_semantics=semantics, vmem_limit_bytes=VMEM_LIMIT_BYTES_V7X)


def _const_spec(shape):
    zeros = (0,) * len(shape)
    return pl.BlockSpec(shape, lambda *_: zeros, pipeline_mode=pl.Buffered(1))


def _rms(xf, gain):
    return xf * lax.rsqrt(jnp.mean(xf * xf, axis=-1, keepdims=True) + RMS_EPS) * gain


def _dot(a, b):
    return jnp.dot(a, b, preferred_element_type=F32)


def _dot_nt(a, b):
    return lax.dot_general(a, b, (((1,), (1,)), ((), ())), preferred_element_type=F32)


def _dot_tn(a, b):
    return lax.dot_general(a, b, (((0,), (0,)), ((), ())), preferred_element_type=F32)


def _swap_halves(x, group):
    n = x.shape[-1]
    half = group // 2
    lane = lax.broadcasted_iota(jnp.int32, x.shape, x.ndim - 1)
    from_right = pltpu.roll(x, n - half, axis=x.ndim - 1)
    from_left = pltpu.roll(x, half, axis=x.ndim - 1)
    return jnp.where((lane % group) < half, from_right, from_left)


def _ab_in_kernel(x_ref, g_ref, w_ref, wvt_ref, gq_ref, gk_ref, gsum_ref, cos_ref, sin_ref,
                  qa_ref, ka_ref, va_ref, qr_ref, kr_ref, vr_ref, gr_ref):
    h = _rms(x_ref[...], g_ref[...]).astype(BF16)

    def proj(lo, hi):
        return _dot(h, w_ref[:, lo:hi])

    def qk_norm(y, gain):
        ms = _dot((y * y).astype(BF16), gsum_ref[...])
        return (y * lax.rsqrt(ms + RMS_EPS) * gain).astype(BF16)

    def rotary(y):
        return y * cos_ref[...] + _swap_halves(y, RET_QK_DIM) * sin_ref[...]

    qa_ref[...] = qk_norm(proj(0, 512), gq_ref[...])
    ka_ref[...] = qk_norm(proj(512, 1024), gk_ref[...])
    vt = _dot_nt(wvt_ref[...], h).astype(BF16)
    pad_row = lax.broadcasted_iota(jnp.int32, (DA_V_ROWS - DA_V_DIM, vt.shape[1]), 0)
    ones_rows = jnp.where(pad_row == 0, 1.0, 0.0).astype(BF16)
    for hd in range(DA_HEADS):
        va_ref[0, hd * DA_V_ROWS:hd * DA_V_ROWS + DA_V_DIM, :] = vt[hd * DA_V_DIM:(hd + 1) * DA_V_DIM]
        va_ref[0, hd * DA_V_ROWS + DA_V_DIM:(hd + 1) * DA_V_ROWS, :] = ones_rows
    qr_ref[...] = rotary(proj(1536, 1792))
    kr_ref[...] = rotary(proj(1792, 2048)) * (RET_QK_DIM ** -0.5)
    vr_ref[...] = proj(2048, 2560).astype(BF16)
    gr_ref[...] = proj(2560, 3072)


def _ab_in(x2, g, w, wv_t, gq, gk, gsum, cos_t, sin_t, seq):
    t = x2.shape[0]
    tm = min(TOKEN_TILE, seq)
    per_seq = seq // tm
    row = lambda width: pl.BlockSpec((tm, width), lambda i: (i, 0))
    va_spec = pl.BlockSpec((1, DA_HEADS * DA_V_ROWS, tm), lambda i: (i // per_seq, 0, i % per_seq))
    rot_spec = pl.BlockSpec((tm, 256), lambda i: (i % per_seq, 0))
    out_shapes = (
        jax.ShapeDtypeStruct((t, 512), BF16), jax.ShapeDtypeStruct((t, 512), BF16),
        jax.ShapeDtypeStruct((t // seq, DA_HEADS * DA_V_ROWS, seq), BF16), jax.ShapeDtypeStruct((t, 256), F32),
        jax.ShapeDtypeStruct((t, 256), F32), jax.ShapeDtypeStruct((t, 512), BF16),
        jax.ShapeDtypeStruct((t, 512), F32))
    return pl.pallas_call(
        _ab_in_kernel,
        grid=(t // tm,),
        in_specs=[row(D_MODEL), _const_spec((1, D_MODEL)), _const_spec((D_MODEL, AB_IN)),
                  _const_spec((DA_WIDTH, D_MODEL)),
                  _const_spec((1, 512)), _const_spec((1, 512)), _const_spec((512, 512)),
                  rot_spec, rot_spec],
        out_specs=(row(512), row(512), va_spec, row(256), row(256), row(512), row(512)),
        out_shape=out_shapes,
        compiler_params=_params("parallel"),
        name="ab_in_proj",
    )(x2, g, w, wv_t, gq, gk, gsum, cos_t, sin_t)


def _diff_attn_kernel(lq1_ref, lk1_ref, lq2_ref, lk2_ref, gout_ref, q_ref, k_ref, vt_ref, o_ref,
                      sa_ref, sb_ref, sc_ref, mxa_ref, mxb_ref, mxc_ref, qza_ref, qzb_ref, m_ref, acc_ref,
                      *, blk, nblk, lam_init):
    lam = (jnp.exp(jnp.sum(lq1_ref[...] * lk1_ref[...], axis=-1, keepdims=True))
           - jnp.exp(jnp.sum(lq2_ref[...] * lk2_ref[...], axis=-1, keepdims=True)) + lam_init)
    gain = gout_ref[...] * (1.0 - lam_init)
    buf_a, buf_b, buf_c = (sa_ref, mxa_ref), (sb_ref, mxb_ref), (sc_ref, mxc_ref)

    def load_queries(qb, qz_ref):
        start = pl.multiple_of(qb * blk, blk)
        qt = q_ref[0, pl.ds(start, blk), :].astype(F32).T
        dim = lax.broadcasted_iota(jnp.int32, qt.shape, 0)
        qz_ref[...] = jnp.concatenate([jnp.where(dim < DA_QK_DIM, qt, 0.0), jnp.where(dim >= DA_QK_DIM, qt, 0.0)],
                                      axis=1).astype(BF16)

    def score(qz_ref, t, buf):
        s_ref, mx_ref = buf
        start = pl.multiple_of(t * blk, blk)
        st = _dot(k_ref[0, pl.ds(start, blk), :], qz_ref[...])
        s_ref[...] = st
        mx_ref[...] = jnp.max(st, axis=0, keepdims=True)

    def absorb(t, buf, diagonal=False):
        s_ref, mx_ref = buf
        start = pl.multiple_of(t * blk, blk)
        tile = ATTN_TILE
        for qg in range(2 * blk // tile):
            cs = slice(qg * tile, (qg + 1) * tile)

            def scores(kh):
                st = s_ref[kh * tile:(kh + 1) * tile, cs]
                if diagonal:
                    key = lax.broadcasted_iota(jnp.int32, st.shape, 0) + kh * tile
                    qry = lax.broadcasted_iota(jnp.int32, st.shape, 1) + (qg * tile) % blk
                    st = jnp.where(key // CHUNK <= qry // CHUNK, st, NEG_BIG)
                return st

            if diagonal:
                mx = functools.reduce(jnp.maximum, [jnp.max(scores(kh), axis=0, keepdims=True)
                                                    for kh in range(blk // tile)])
            else:
                mx = mx_ref[:, cs]
            m = m_ref[:, cs]
            m_new = jnp.maximum(m, mx)
            acc = jnp.exp2(m - m_new) * acc_ref[:, cs]
            for kh in range(blk // tile):
                p = jnp.exp2(scores(kh) - m_new)
                acc = acc + _dot(vt_ref[0, :, pl.ds(start + kh * tile, tile)], p.astype(BF16))
            m_ref[:, cs] = m_new
            acc_ref[:, cs] = acc

    def repeat(first, count, two_stages):
        def twice(v, c):
            two_stages(first + 2 * v)
            two_stages(first + 2 * v + 1)
            return c

        lax.fori_loop(0, lax.shift_right_logical(count, 1), twice, 0)

        @pl.when((count & 1) == 1)
        def _():
            two_stages(first + count - 1)

    def reset():
        m_ref[...] = jnp.full(m_ref.shape, NEG_BIG, F32)
        acc_ref[...] = jnp.zeros(acc_ref.shape, F32)

    def finish(qb):
        o = acc_ref[:DA_V_DIM, :] / acc_ref[DA_V_DIM:DA_V_DIM + 1, :]
        o = o[:, :blk] - lam * o[:, blk:]
        o = o * lax.rsqrt(jnp.mean(o * o, axis=0, keepdims=True) + RMS_EPS)
        o_ref[0, pl.ds(pl.multiple_of(qb * blk, blk), blk), :] = (o.T * gain).astype(BF16)
        reset()

    reset()
    load_queries(0, qza_ref)
    load_queries(1, qzb_ref)
    score(qza_ref, 0, buf_a)
    score(qzb_ref, 0, buf_b)
    absorb(0, buf_a, diagonal=True)
    finish(0)
    score(qzb_ref, 1, buf_a)
    absorb(0, buf_b)
    load_queries(min(2, nblk - 1), qza_ref)
    score(qza_ref, 0, buf_c)
    absorb(1, buf_a, diagonal=True)
    finish(1)

    def pair(g, carry):
        a = 2 * g
        load_queries(a + 1, qzb_ref)
        score(qza_ref, 1, buf_b)
        absorb(0, buf_c)
        score(qza_ref, 2, buf_a)
        absorb(1, buf_b)

        def stages_a(u):
            score(qza_ref, 2 * u + 1, buf_b)
            absorb(2 * u, buf_a)
            score(qza_ref, 2 * u + 2, buf_a)
            absorb(2 * u + 1, buf_b)

        repeat(1, g - 1, stages_a)
        score(qzb_ref, 0, buf_b)
        absorb(a, buf_a, diagonal=True)
        finish(a)
        score(qzb_ref, 1, buf_a)
        absorb(0, buf_b)

        def stages_b(u):
            score(qzb_ref, 2 * u + 2, buf_b)
            absorb(2 * u + 1, buf_a)
            score(qzb_ref, 2 * u + 3, buf_a)
            absorb(2 * u + 2, buf_b)

        repeat(0, g, stages_b)
        load_queries(jnp.minimum(a + 2, nblk - 1), qza_ref)
        score(qza_ref, 0, buf_c)
        absorb(a + 1, buf_a, diagonal=True)
        finish(a + 1)
        return carry

    lax.fori_loop(1, nblk // 2, pair, 0)


def _diff_attn(qa, ka, va_t, lq1, lk1, lq2, lk2, gout, lam_init):
    b, s, _ = qa.shape
    blk = min(ATTN_BLOCK, s // 2)
    nblk = s // blk
    assert nblk % 2 == 0 and blk % ATTN_TILE == 0, (s, blk)
    vec = _const_spec((1, DA_QK_DIM))
    tok_spec = pl.BlockSpec((1, s, 128), lambda bi, h: (bi, 0, h))
    vt_spec = pl.BlockSpec((1, DA_V_ROWS, s), lambda bi, h: (bi, h, 0))
    scores = pltpu.VMEM((blk, 2 * blk), F32)
    stats = pltpu.VMEM((1, 2 * blk), F32)
    queries = pltpu.VMEM((2 * DA_QK_DIM, 2 * blk), BF16)
    return pl.pallas_call(
        functools.partial(_diff_attn_kernel, blk=blk, nblk=nblk, lam_init=lam_init),
        grid=(b, DA_HEADS),
        in_specs=[vec, vec, vec, vec, _const_spec((1, DA_V_DIM)), tok_spec, tok_spec, vt_spec],
        out_specs=tok_spec,
        out_shape=jax.ShapeDtypeStruct((b, s, DA_WIDTH), BF16),
        scratch_shapes=[scores, scores, scores, stats, stats, stats, queries, queries, stats,
                        pltpu.VMEM((DA_V_ROWS, 2 * blk), F32)],
        compiler_params=_params("parallel", "parallel"),
        name="diff_attention",
    )(lq1, lk1, lq2, lk2, gout, qa, ka, va_t)


def _retention_kernel(dec_ref, qdec_ref, kdec_ref, sdec_ref, smask_ref, gain_ref,
                      q_ref, k_ref, v_ref, g_ref, o_ref, state_ref):
    @pl.when(pl.program_id(1) == 0)
    def _():
        state_ref[...] = jnp.zeros_like(state_ref)

    q = q_ref[0]
    k = k_ref[0]
    v = v_ref[0]
    kb = k.astype(BF16)
    lane = lax.broadcasted_iota(jnp.int32, q.shape, 1)
    state = state_ref[...]
    o_cross = _dot((q * qdec_ref[...]).astype(BF16), state.astype(BF16))
    for h in range(RET_HEADS):
        qh = jnp.where(lane // RET_QK_DIM == h, q, 0.0).astype(BF16)
        w = (_dot_nt(qh, kb) * dec_ref[h]).astype(BF16)
        vs = slice(h * RET_V_DIM, (h + 1) * RET_V_DIM)
        o = _dot(w, v[:, vs]) + o_cross[:, vs]
        o = _rms(o, gain_ref[...])
        g = g_ref[0, :, vs]
        o_ref[0, :, vs] = (o * (g * jax.nn.sigmoid(g))).astype(BF16)
    kv = _dot_tn((k * kdec_ref[...]).astype(BF16), v)
    state_ref[...] = sdec_ref[...] * state + kv * smask_ref[...]


def _retention_tables(blk):
    heads = jnp.arange(RET_HEADS, dtype=F32)
    log_gamma = jnp.log(1.0 - 2.0 ** (-5.0 - heads))
    idx = jnp.arange(blk)
    diff = (idx[:, None] - idx[None, :]).astype(F32)
    same_chunk = (idx[:, None] // CHUNK) == (idx[None, :] // CHUNK)
    visible = (idx[None, :] <= idx[:, None]) | same_chunk
    dec = jnp.where(visible[None], jnp.exp(log_gamma[:, None, None] * jnp.abs(diff)[None]), 0.0)
    per_lane = jnp.repeat(log_gamma, RET_QK_DIM)
    pos = jnp.arange(blk, dtype=F32)
    qdec = jnp.exp((pos[:, None] + 1.0) * per_lane[None, :])
    kdec = jnp.exp((blk - 1.0 - pos)[:, None] * per_lane[None, :])
    sdec = jnp.exp(blk * per_lane)[:, None]
    smask = (jnp.arange(RET_HEADS * RET_QK_DIM)[:, None] // RET_QK_DIM
             == jnp.arange(RET_WIDTH)[None, :] // RET_V_DIM).astype(F32)
    return dec, qdec, kdec, sdec, smask


def _retention(qr, kr, vr, gr, gain):
    b, s, _ = qr.shape
    blk = min(RET_BLOCK, s)
    dec, qdec, kdec, sdec, smask = _retention_tables(blk)
    tok = lambda width: pl.BlockSpec((1, blk, width), lambda bi, i: (bi, i, 0))
    return pl.pallas_call(
        _retention_kernel,
        grid=(b, s // blk),
        in_specs=[_const_spec(dec.shape), _const_spec(qdec.shape), _const_spec(kdec.shape),
                  _const_spec(sdec.shape), _const_spec(smask.shape), _const_spec((1, RET_V_DIM)),
                  tok(256), tok(256), tok(512), tok(512)],
        out_specs=tok(RET_WIDTH),
        out_shape=jax.ShapeDtypeStruct((b, s, RET_WIDTH), BF16),
        scratch_shapes=[pltpu.VMEM((RET_HEADS * RET_QK_DIM, RET_WIDTH), F32)],
        compiler_params=_params("parallel", "arbitrary"),
        name="retention",
    )(dec, qdec, kdec, sdec, smask, gain, qr, kr, vr, gr)


def _mlp_tail(x1, g_ref, wup_ref, wdn_ref, o_ref):
    h = _rms(x1, g_ref[...]).astype(BF16)
    acc = x1
    for c in range(MLP_HIDDEN // MLP_HIDDEN_TILE):
        cs = slice(c * MLP_HIDDEN_TILE, (c + 1) * MLP_HIDDEN_TILE)
        u = jnp.maximum(_dot(h, wup_ref[:, cs]), 0.0)
        acc = acc + _dot((u * u).astype(BF16), wdn_ref[cs, :])
    o_ref[...] = acc


def _out_mlp_kernel(x_ref, a_ref, b_ref, wout_ref, g_ref, wup_ref, wdn_ref, o_ref):
    ka = a_ref.shape[1]
    mixed = _dot(a_ref[...], wout_ref[:ka, :]) + _dot(b_ref[...], wout_ref[ka:, :])
    _mlp_tail(x_ref[...] + mixed, g_ref, wup_ref, wdn_ref, o_ref)


def _s5_out_mlp_kernel(x_ref, yt_ref, ut_ref, d_ref, wglut_ref, b_ref, wout_ref, g_ref, wup_ref, wdn_ref, o_ref):
    y = yt_ref[...] + d_ref[...] * ut_ref[...]
    z = jax.nn.gelu(y)
    a_t = (z * jax.nn.sigmoid(_dot(wglut_ref[...], z.astype(BF16)))).astype(BF16)
    mixed = _dot_tn(a_t, wout_ref[:S5_WIDTH, :]) + _dot(b_ref[...], wout_ref[S5_WIDTH:, :])
    _mlp_tail(x_ref[...] + mixed, g_ref, wup_ref, wdn_ref, o_ref)


def _mlp_specs():
    return [_const_spec((D_MODEL, D_MODEL)), _const_spec((1, D_MODEL)),
            _const_spec((D_MODEL, MLP_HIDDEN)), _const_spec((MLP_HIDDEN, D_MODEL))]


def _out_mlp(x2, a, bb, wout, g, wup, wdn):
    t = x2.shape[0]
    tm = TOKEN_TILE
    row = lambda width: pl.BlockSpec((tm, width), lambda i: (i, 0))
    return pl.pallas_call(
        _out_mlp_kernel,
        grid=(t // tm,),
        in_specs=[row(D_MODEL), row(a.shape[1]), row(bb.shape[1])] + _mlp_specs(),
        out_specs=row(D_MODEL),
        out_shape=jax.ShapeDtypeStruct((t, D_MODEL), F32),
        compiler_params=_params("parallel"),
        name="ab_out_mlp",
    )(x2, a, bb, wout, g, wup, wdn)


def _s5_out_mlp(x2, y_t, u_t, d_col, wglu_t, bb, wout, g, wup, wdn):
    t = x2.shape[0]
    tm = TOKEN_TILE
    row = lambda width: pl.BlockSpec((tm, width), lambda i: (i, 0))
    col = pl.BlockSpec((S5_WIDTH, tm), lambda i: (0, i))
    return pl.pallas_call(
        _s5_out_mlp_kernel,
        grid=(t // tm,),
        in_specs=[row(D_MODEL), col, col, _const_spec((S5_WIDTH, 1)),
                  _const_spec((S5_WIDTH, S5_WIDTH)), row(GLA_WIDTH)] + _mlp_specs(),
        out_specs=row(D_MODEL),
        out_shape=jax.ShapeDtypeStruct((t, D_MODEL), F32),
        compiler_params=_params("parallel"),
        name="cd_out_mlp",
    )(x2, y_t, u_t, d_col, wglu_t, bb, wout, g, wup, wdn)


def _cd_in_kernel(x_ref, g_ref, w_ref, wut_ref, wa_ref, ba_ref, ut_ref, q_ref, k_ref, v_ref, r_ref, la_ref):
    h = _rms(x_ref[...], g_ref[...]).astype(BF16)

    def proj(lo, hi):
        return _dot(h, w_ref[:, lo:hi])

    ua_t = _dot_nt(wut_ref[...], h)
    ut_ref[...] = ua_t[:S5_WIDTH]
    q_ref[...] = proj(256, 640) * (GLA_QK_DIM ** -0.5)
    k_ref[...] = proj(640, 1024)
    v_ref[...] = proj(1024, 1792).astype(BF16)
    r_ref[...] = proj(1792, 2560)
    pre = _dot_tn(ua_t[S5_WIDTH:].astype(BF16), wa_ref[...]) + ba_ref[...]
    log_sig = jnp.minimum(pre, 0.0) - jnp.log1p(jnp.exp(-jnp.abs(pre)))
    la_ref[...] = log_sig / GLA_TAU


def _cd_in(x2, g, w, wu_t, wa, ba):
    t = x2.shape[0]
    tm = TOKEN_TILE
    row = lambda width: pl.BlockSpec((tm, width), lambda i: (i, 0))
    out_shapes = (
        jax.ShapeDtypeStruct((S5_WIDTH, t), F32), jax.ShapeDtypeStruct((t, GLA_QK_WIDTH), F32),
        jax.ShapeDtypeStruct((t, GLA_QK_WIDTH), F32), jax.ShapeDtypeStruct((t, GLA_WIDTH), BF16),
        jax.ShapeDtypeStruct((t, GLA_WIDTH), F32), jax.ShapeDtypeStruct((t, GLA_QK_WIDTH), F32))
    return pl.pallas_call(
        _cd_in_kernel,
        grid=(t // tm,),
        in_specs=[row(D_MODEL), _const_spec((1, D_MODEL)), _const_spec((D_MODEL, CD_IN_MAIN)),
                  _const_spec((S5_WIDTH + GATE_ROWS, D_MODEL)),
                  _const_spec((GATE_ROWS, GLA_QK_WIDTH)), _const_spec((1, GLA_QK_WIDTH))],
        out_specs=(pl.BlockSpec((S5_WIDTH, tm), lambda i: (0, i)), row(GLA_QK_WIDTH), row(GLA_QK_WIDTH),
                   row(GLA_WIDTH), row(GLA_WIDTH), row(GLA_QK_WIDTH)),
        out_shape=out_shapes,
        compiler_params=_params("parallel"),
        name="cd_in_proj",
    )(x2, g, w, wu_t, wa, ba)


def _split3(x):
    hi = x.astype(BF16)
    r = x - hi.astype(F32)
    mid = r.astype(BF16)
    lo = (r - mid.astype(F32)).astype(BF16)
    return hi, mid, lo


def _gla_kernel(tri_ref, gain_ref, q_ref, k_ref, v_ref, r_ref, la_ref, o_ref, state_ref, *, blk, batch):
    @pl.when(pl.program_id(0) == 0)
    def _():
        state_ref[...] = jnp.zeros_like(state_ref)

    tri = tri_ref[...]
    hi, mid, lo = _split3(jnp.concatenate([la_ref[bi] for bi in range(batch)], axis=1))
    b_all = _dot(tri, hi) + _dot(tri, mid) + _dot(tri, lo)
    for bi in range(batch):
        _gla_block(bi, b_all[:, bi * GLA_QK_WIDTH:(bi + 1) * GLA_QK_WIDTH], gain_ref, q_ref, k_ref, v_ref, r_ref,
                   o_ref, state_ref, blk)


def _gla_block(bi, b, gain_ref, q_ref, k_ref, v_ref, r_ref, o_ref, state_ref, blk):
    nc = blk // CHUNK
    pairs = GLA_HEADS // 2
    q = q_ref[bi]
    k = k_ref[bi]
    b3 = b.reshape(nc, CHUNK, GLA_QK_WIDTH)
    b_last = b3[:, CHUNK - 1:CHUNK, :]
    e_pos = jnp.exp(b)
    e_neg = jnp.exp(-b)
    k_tail = jnp.exp(b_last - b3).reshape(blk, GLA_QK_WIDTH)
    chunk_decay = jnp.exp(b_last)
    qp = (q * e_pos).astype(BF16)
    qn = (q * e_neg).astype(BF16)
    kp = (k * e_pos).astype(BF16)
    kn = (k * e_neg).astype(BF16)
    kw = (k * k_tail).astype(BF16)

    row = lax.broadcasted_iota(jnp.int32, (blk, blk), 0)
    col = lax.broadcasted_iota(jnp.int32, (blk, blk), 1)
    same_chunk = row // CHUNK == col // CHUNK
    causal = row >= col
    lane = lax.broadcasted_iota(jnp.int32, (blk, 2 * GLA_QK_DIM), 1)
    smask = (lax.broadcasted_iota(jnp.int32, (2 * GLA_V_DIM, 2 * GLA_QK_DIM), 0) // GLA_V_DIM
             == lax.broadcasted_iota(jnp.int32, (2 * GLA_V_DIM, 2 * GLA_QK_DIM), 1) // GLA_QK_DIM)

    for p in range(pairs):
        ks = slice(p * 2 * GLA_QK_DIM, (p + 1) * 2 * GLA_QK_DIM)
        vs = slice(p * 2 * GLA_V_DIM, (p + 1) * 2 * GLA_V_DIM)
        v = v_ref[bi, :, vs]
        scores = []
        for half in range(2):
            sel = (lane // GLA_QK_DIM) == half
            fwd = _dot_nt(jnp.where(sel, qp[:, ks], 0).astype(BF16), kn[:, ks])
            bwd = _dot_nt(jnp.where(sel, qn[:, ks], 0).astype(BF16), kp[:, ks])
            scores.append(jnp.where(same_chunk, jnp.where(causal, fwd, bwd), 0.0).astype(BF16))
        vlane = lax.broadcasted_iota(jnp.int32, v.shape, 1) // GLA_V_DIM
        v_diag = jnp.concatenate([jnp.where(vlane == 0, v, 0), jnp.where(vlane == 1, v, 0)], axis=0)
        o_intra = _dot(jnp.concatenate(scores, axis=1), v_diag)

        state = state_ref[bi * pairs + p]
        cross = []
        for c in range(nc):
            rs = slice(c * CHUNK, (c + 1) * CHUNK)
            cross.append(_dot_nt(qp[rs, ks], state.astype(BF16)))
            kv_t = _dot_tn(v[rs, :], kw[rs, ks])
            state = state * chunk_decay[c][:, ks] + jnp.where(smask, kv_t, 0.0)
        state_ref[bi * pairs + p] = state
        o = o_intra + jnp.concatenate(cross, axis=0)
        for half in range(2):
            hs = slice(half * GLA_V_DIM, (half + 1) * GLA_V_DIM)
            os_ = slice(p * 2 * GLA_V_DIM + half * GLA_V_DIM, p * 2 * GLA_V_DIM + (half + 1) * GLA_V_DIM)
            g = r_ref[bi, :, os_]
            o_ref[bi, :, os_] = (_rms(o[:, hs], gain_ref[...]) * (g * jax.nn.sigmoid(g))).astype(BF16)


def _gla(qg, kg, vg, rg, la, gain):
    b, s, _ = qg.shape
    blk = min(GLA_BLOCK, s)
    idx = jnp.arange(blk)
    tri = (((idx[:, None] // CHUNK) == (idx[None, :] // CHUNK)) & (idx[None, :] <= idx[:, None])).astype(BF16)
    tok = lambda width: pl.BlockSpec((b, blk, width), lambda i: (0, i, 0))
    return pl.pallas_call(
        functools.partial(_gla_kernel, blk=blk, batch=b),
        grid=(s // blk,),
        in_specs=[_const_spec((blk, blk)), _const_spec((1, GLA_V_DIM)),
                  tok(GLA_QK_WIDTH), tok(GLA_QK_WIDTH), tok(GLA_WIDTH), tok(GLA_WIDTH), tok(GLA_QK_WIDTH)],
        out_specs=tok(GLA_WIDTH),
        out_shape=jax.ShapeDtypeStruct((b, s, GLA_WIDTH), BF16),
        scratch_shapes=[pltpu.VMEM((b * (GLA_HEADS // 2), 2 * GLA_V_DIM, 2 * GLA_QK_DIM), F32)],
        compiler_params=_params("arbitrary"),
        name="gla",
    )(tri, gain, qg, kg, vg, rg, la)


def _s5_kernel(u_ref, kern_ref, fre_ref, fim_ref, ere_ref, eim_ref, lre_ref, lim_ref, y_ref,
               toep_ref, vre_ref, vim_ref, hre_ref, him_ref, *, batch, chunks):
    length = S5_CHUNK
    src = lax.broadcasted_iota(jnp.int32, (length, length), 0)
    dst = lax.broadcasted_iota(jnp.int32, (length, length), 1)

    def build(ci, carry):
        for co in range(S5_GROUP):
            lags = kern_ref[0, pl.ds(ci * S5_GROUP + co, 1), :]
            blk = pltpu.roll(jnp.broadcast_to(lags, (length, length)), 0, 1, stride=1, stride_axis=0)
            blk = jnp.where(dst >= src, blk, 0.0)
            toep_ref[pl.ds(pl.multiple_of(ci * length, length), length), co * length:(co + 1) * length] = (
                blk.astype(BF16))
        return carry

    lax.fori_loop(0, S5_GROUP, build, 0)

    u = jnp.concatenate([u_ref[0, c] for c in range(S5_GROUP)], axis=1).astype(BF16)
    vre_ref[...] = _dot(u, fre_ref[0])
    vim_ref[...] = _dot(u, fim_ref[0])
    lre = lre_ref[0]
    lim = lim_ref[0]

    def step(n, carry):
        new = []
        for bi in range(batch):
            hr, hi = carry[2 * bi], carry[2 * bi + 1]
            r = bi * chunks + n
            hre_ref[pl.ds(r, 1), :] = hr
            him_ref[pl.ds(r, 1), :] = hi
            vr = vre_ref[pl.ds(r, 1), :]
            vi = vim_ref[pl.ds(r, 1), :]
            new += [lre * hr - lim * hi + vr, lre * hi + lim * hr + vi]
        return tuple(new)

    zero = jnp.zeros((1, LANES_V7X), F32)
    lax.fori_loop(0, chunks, step, (zero,) * (2 * batch))
    y = (_dot(u, toep_ref[...]) + _dot(hre_ref[...].astype(BF16), ere_ref[0])
         + _dot(him_ref[...].astype(BF16), eim_ref[0]))
    for c in range(S5_GROUP):
        y_ref[0, c] = y[:, c * length:(c + 1) * length]


def _s5_tables(a_re, a_im, log_step, b_re, b_im, c_re, c_im):
    hp = lax.Precision.HIGHEST
    g, p, c, length = S5_GROUPS, S5_STATE, S5_GROUP, S5_CHUNK
    a_re, a_im = a_re.astype(F32), a_im.astype(F32)
    delta = jnp.exp(log_step.astype(F32))[:, None]
    tau = jnp.arange(length + 1, dtype=F32)[None, :, None]
    mag = jnp.exp((a_re * delta)[:, None, :] * tau)
    ang = (a_im * delta)[:, None, :] * tau
    pw_re, pw_im = mag * jnp.cos(ang), mag * jnp.sin(ang)
    n_re, n_im = pw_re[:, 1, :] - 1.0, pw_im[:, 1, :]
    den = a_re * a_re + a_im * a_im
    q_re, q_im = (n_re * a_re + n_im * a_im) / den, (n_im * a_re - n_re * a_im) / den
    bb_re = q_re[:, :, None] * b_re - q_im[:, :, None] * b_im
    bb_im = q_re[:, :, None] * b_im + q_im[:, :, None] * b_re
    first_re, first_im = pw_re[:, :length, None, :], pw_im[:, :length, None, :]
    cp_re = c_re[:, None] * first_re - c_im[:, None] * first_im
    cp_im = c_re[:, None] * first_im + c_im[:, None] * first_re
    kern = (jnp.einsum('gtcp,gpd->gdct', cp_re, bb_re, precision=hp)
            - jnp.einsum('gtcp,gpd->gdct', cp_im, bb_im, precision=hp)).reshape(g, c * c, length)
    rev_re, rev_im = pw_re[:, length - 1::-1, :][:, None], pw_im[:, length - 1::-1, :][:, None]
    bt_re, bt_im = bb_re.transpose(0, 2, 1)[:, :, None, :], bb_im.transpose(0, 2, 1)[:, :, None, :]
    f_re = (rev_re * bt_re - rev_im * bt_im).reshape(g, c * length, p)
    f_im = (rev_re * bt_im + rev_im * bt_re).reshape(g, c * length, p)
    nx_re = pw_re[:, 1:, :].transpose(0, 2, 1)[:, :, None, :]
    nx_im = pw_im[:, 1:, :].transpose(0, 2, 1)[:, :, None, :]
    ct_re, ct_im = c_re.transpose(0, 2, 1)[:, :, :, None], c_im.transpose(0, 2, 1)[:, :, :, None]
    e_re = (ct_re * nx_re - ct_im * nx_im).reshape(g, p, c * length)
    e_im = (ct_re * nx_im + ct_im * nx_re).reshape(g, p, c * length)
    pad = lambda z: jnp.pad(z, ((0, 0), (0, 0), (0, LANES_V7X - p)))
    pad_rows = lambda z: jnp.pad(z, ((0, 0), (0, LANES_V7X - p), (0, 0)))
    return (kern, pad(f_re).astype(BF16), pad(f_im).astype(BF16), pad_rows(e_re).astype(BF16),
            pad_rows(-e_im).astype(BF16), pad(pw_re[:, length:, :]), pad(pw_im[:, length:, :]))


def _s5(u_t, tables, batch):
    t = u_t.shape[1]
    length = S5_CHUNK
    rows = t // length
    width = length * S5_GROUP
    kern, fre, fim, ere, eim, lre, lim = tables
    grp = lambda r, c: pl.BlockSpec((1, r, c), lambda gi: (gi, 0, 0))
    io_spec = pl.BlockSpec((1, S5_GROUP, rows, length), lambda gi: (gi, 0, 0, 0))
    y = pl.pallas_call(
        functools.partial(_s5_kernel, batch=batch, chunks=rows // batch),
        grid=(S5_GROUPS,),
        in_specs=[io_spec, grp(S5_GROUP * S5_GROUP, length), grp(width, LANES_V7X), grp(width, LANES_V7X),
                  grp(LANES_V7X, width), grp(LANES_V7X, width), grp(1, LANES_V7X), grp(1, LANES_V7X)],
        out_specs=io_spec,
        out_shape=jax.ShapeDtypeStruct((S5_GROUPS, S5_GROUP, rows, length), F32),
        scratch_shapes=[pltpu.VMEM((width, width), BF16)] + [pltpu.VMEM((rows, LANES_V7X), F32)] * 4,
        compiler_params=_params("parallel"),
        name="s5",
    )(u_t.reshape(S5_GROUPS, S5_GROUP, rows, length), kern, fre, fim, ere, eim, lre, lim)
    return y.reshape(S5_WIDTH, t)


def _rotary_tables(seq):
    half = RET_QK_DIM // 2
    inv_freq = 1.0 / (ROPE_BASE ** jnp.linspace(0.0, 1.0, half, dtype=F32))
    ang = jnp.arange(seq, dtype=F32)[:, None] * inv_freq[None, :]
    cos, sin = jnp.cos(ang), jnp.sin(ang)
    cos_t = jnp.tile(jnp.concatenate([cos, cos], axis=1), (1, RET_HEADS))
    sin_t = jnp.tile(jnp.concatenate([-sin, sin], axis=1), (1, RET_HEADS))
    return cos_t, sin_t


def _row(v):
    return v.reshape(1, -1).astype(F32)


def kernel(x, norm_mix_g, norm_mlp_g, w_up, w_down, ab_w_in, ab_w_out, da_q_norm, da_k_norm,
           da_lam_q1, da_lam_k1, da_lam_q2, da_lam_k2, da_out_norm, ret_out_norm, cd_w_in, cd_w_out,
           s5_a_re, s5_a_im, s5_log_step, s5_b_re, s5_b_im, s5_c_re, s5_c_im, s5_d, s5_w_glu,
           gla_w_a2, gla_b_a2, gla_out_norm):
    bsz, seq, _ = x.shape
    t = bsz * seq
    x2 = x.reshape(t, D_MODEL)

    cos_t, sin_t = _rotary_tables(seq)
    gsum = ((jnp.arange(512)[:, None] // DA_QK_DIM) == (jnp.arange(512)[None, :] // DA_QK_DIM))
    gsum = (gsum.astype(F32) / DA_QK_DIM).astype(BF16)
    gq = _row(jnp.tile(da_q_norm[0], 2 * DA_HEADS)) * (DA_QK_DIM ** -0.5 * math.log2(math.e))
    gk = _row(jnp.tile(da_k_norm[0], 2 * DA_HEADS))
    w_ab = ab_w_in[0].astype(BF16)
    qa, ka, va_t, qr, kr, vr, gr = _ab_in(x2, _row(norm_mix_g[0]), w_ab, w_ab[:, 1024:1536].T, gq, gk, gsum,
                                          cos_t, sin_t, seq)
    lam_init = 0.8 - 0.6 * math.exp(-0.3 * 0)
    seq3 = lambda a: a.reshape(bsz, seq, a.shape[-1])
    o_a = _diff_attn(seq3(qa), seq3(ka), va_t, _row(da_lam_q1[0]), _row(da_lam_k1[0]),
                     _row(da_lam_q2[0]), _row(da_lam_k2[0]), _row(da_out_norm[0]), lam_init)
    o_r = _retention(seq3(qr), seq3(kr), seq3(vr), seq3(gr), _row(ret_out_norm[0]))
    x2 = _out_mlp(x2, o_a.reshape(t, DA_WIDTH), o_r.reshape(t, RET_WIDTH), ab_w_out[0].astype(BF16),
                  _row(norm_mlp_g[0]), w_up[0].astype(BF16), w_down[0].astype(BF16))

    w_cd = cd_w_in[0].astype(BF16)
    gate_pad = GATE_ROWS - GLA_GATE_RANK
    wua_t = jnp.pad(jnp.concatenate([w_cd[:, :S5_WIDTH], w_cd[:, CD_IN_MAIN:]], axis=1).T, ((0, gate_pad), (0, 0)))
    wa = jnp.pad(gla_w_a2[0], ((0, gate_pad), (0, 0))).astype(BF16)
    u_t, qg, kg, vg, rg, la = _cd_in(x2, _row(norm_mix_g[1]), w_cd[:, :CD_IN_MAIN], wua_t, wa, _row(gla_b_a2[0]))
    o_d = _gla(seq3(qg), seq3(kg), seq3(vg), seq3(rg), seq3(la), _row(gla_out_norm[0]))
    tables = _s5_tables(s5_a_re[0], s5_a_im[0], s5_log_step[0], s5_b_re[0], s5_b_im[0], s5_c_re[0], s5_c_im[0])
    y_t = _s5(u_t, tables, bsz)
    x2 = _s5_out_mlp(x2, y_t, u_t, s5_d[0].reshape(S5_WIDTH, 1).astype(F32), s5_w_glu[0].T.astype(BF16),
                     o_d.reshape(t, GLA_WIDTH), cd_w_out[0].astype(BF16), _row(norm_mlp_g[1]),
                     w_up[1].astype(BF16), w_down[1].astype(BF16))
    return x2.reshape(bsz, seq, D_MODEL)
```

```python
import functools
import math

import jax
import jax.numpy as jnp
import numpy as np
from jax import lax
from jax.experimental import pallas as pl
from jax.experimental.pallas import tpu as pltpu

F32 = jnp.float32
BF16 = jnp.bfloat16

D_MODEL = 1024
CHUNK = 64
RMS_EPS = 1e-6
ROPE_BASE = 10000.0
DA_HEADS = 4
DA_QK_DIM = 64
DA_V_DIM = 128
DA_WIDTH = DA_HEADS * DA_V_DIM
DA_V_ROWS = DA_V_DIM + 16
RET_HEADS = 4
RET_QK_DIM = 64
RET_V_DIM = 128
RET_WIDTH = RET_HEADS * RET_V_DIM
S5_WIDTH = 256
S5_GROUP = 16
S5_GROUPS = S5_WIDTH // S5_GROUP
S5_STATE = 64
GLA_HEADS = 6
GLA_QK_DIM = 64
GLA_V_DIM = 128
GLA_WIDTH = GLA_HEADS * GLA_V_DIM
GLA_QK_WIDTH = GLA_HEADS * GLA_QK_DIM
GLA_GATE_RANK = 16
GLA_TAU = 16.0
MLP_HIDDEN = 4 * D_MODEL
AB_IN = 3072
CD_IN = 2576

LANES_V7X = 128
VMEM_LIMIT_BYTES_V7X = 56 * 1024 * 1024

CD_IN_MAIN = CD_IN - GLA_GATE_RANK
GATE_ROWS = 32
NEG_BIG = -1e30

TOKEN_TILE = 512
ATTN_BLOCK = 512
ATTN_TILE = 256
RET_BLOCK = 512
GLA_BLOCK = 256
S5_CHUNK = LANES_V7X
MLP_HIDDEN_TILE = 1024


def _params(*semantics):
    return pltpu.CompilerParams(dimension_semantics=semantics, vmem_limit_bytes=VMEM_LIMIT_BYTES_V7X)


def _const_spec(shape):
    zeros = (0,) * len(shape)
    return pl.BlockSpec(shape, lambda *_: zeros, pipeline_mode=pl.Buffered(1))


def _rms(xf, gain):
    return xf * lax.rsqrt(jnp.mean(xf * xf, axis=-1, keepdims=True) + RMS_EPS) * gain


def _dot(a, b):
    return jnp.dot(a, b, preferred_element_type=F32)


def _dot_nt(a, b):
    return lax.dot_general(a, b, (((1,), (1,)), ((), ())), preferred_element_type=F32)


def _dot_tn(a, b):
    return lax.dot_general(a, b, (((0,), (0,)), ((), ())), preferred_element_type=F32)


def _swap_halves(x, group):
    n = x.shape[-1]
    half = group // 2
    lane = lax.broadcasted_iota(jnp.int32, x.shape, x.ndim - 1)
    from_right = pltpu.roll(x, n - half, axis=x.ndim - 1)
    from_left = pltpu.roll(x, half, axis=x.ndim - 1)
    return jnp.where((lane % group) < half, from_right, from_left)


def _ab_in_kernel(x_ref, g_ref, w_ref, wvt_ref, gq_ref, gk_ref, gsum_ref, cos_ref, sin_ref,
                  qa_ref, ka_ref, va_ref, qr_ref, kr_ref, vr_ref, gr_ref):
    h = _rms(x_ref[...], g_ref[...]).astype(BF16)

    def proj(lo, hi):
        return _dot(h, w_ref[:, lo:hi])

    def qk_norm(y, gain):
        ms = _dot((y * y).astype(BF16), gsum_ref[...])
        return (y * lax.rsqrt(ms + RMS_EPS) * gain).astype(BF16)

    def rotary(y):
        return y * cos_ref[...] + _swap_halves(y, RET_QK_DIM) * sin_ref[...]

    qa_ref[...] = qk_norm(proj(0, 512), gq_ref[...])
    ka_ref[...] = qk_norm(proj(512, 1024), gk_ref[...])
    vt = _dot_nt(wvt_ref[...], h).astype(BF16)
    pad_row = lax.broadcasted_iota(jnp.int32, (DA_V_ROWS - DA_V_DIM, vt.shape[1]), 0)
    ones_rows = jnp.where(pad_row == 0, 1.0, 0.0).astype(BF16)
    for hd in range(DA_HEADS):
        va_ref[0, hd * DA_V_ROWS:hd * DA_V_ROWS + DA_V_DIM, :] = vt[hd * DA_V_DIM:(hd + 1) * DA_V_DIM]
        va_ref[0, hd * DA_V_ROWS + DA_V_DIM:(hd + 1) * DA_V_ROWS, :] = ones_rows
    qr_ref[...] = rotary(proj(1536, 1792))
    kr_ref[...] = rotary(proj(1792, 2048)) * (RET_QK_DIM ** -0.5)
    vr_ref[...] = proj(2048, 2560).astype(BF16)
    gr_ref[...] = proj(2560, 3072)


def _ab_in(x2, g, w, wv_t, gq, gk, gsum, cos_t, sin_t, seq):
    t = x2.shape[0]
    tm = min(TOKEN_TILE, seq)
    per_seq = seq // tm
    row = lambda width: pl.BlockSpec((tm, width), lambda i: (i, 0))
    va_spec = pl.BlockSpec((1, DA_HEADS * DA_V_ROWS, tm), lambda i: (i // per_seq, 0, i % per_seq))
    rot_spec = pl.BlockSpec((tm, 256), lambda i: (i % per_seq, 0))
    out_shapes = (
        jax.ShapeDtypeStruct((t, 512), BF16), jax.ShapeDtypeStruct((t, 512), BF16),
        jax.ShapeDtypeStruct((t // seq, DA_HEADS * DA_V_ROWS, seq), BF16), jax.ShapeDtypeStruct((t, 256), F32),
        jax.ShapeDtypeStruct((t, 256), F32), jax.ShapeDtypeStruct((t, 512), BF16),
        jax.ShapeDtypeStruct((t, 512), F32))
    return pl.pallas_call(
        _ab_in_kernel,
        grid=(t // tm,),
        in_specs=[row(D_MODEL), _const_spec((1, D_MODEL)), _const_spec((D_MODEL, AB_IN)),
                  _const_spec((DA_WIDTH, D_MODEL)),
                  _const_spec((1, 512)), _const_spec((1, 512)), _const_spec((512, 512)),
                  rot_spec, rot_spec],
        out_specs=(row(512), row(512), va_spec, row(256), row(256), row(512), row(512)),
        out_shape=out_shapes,
        compiler_params=_params("parallel"),
        name="ab_in_proj",
    )(x2, g, w, wv_t, gq, gk, gsum, cos_t, sin_t)


def _diff_attn_kernel(lq1_ref, lk1_ref, lq2_ref, lk2_ref, gout_ref, q_ref, k_ref, vt_ref, o_ref,
                      sa_ref, sb_ref, sc_ref, mxa_ref, mxb_ref, mxc_ref, qza_ref, qzb_ref, m_ref, acc_ref,
                      *, blk, nblk, lam_init):
    lam = (jnp.exp(jnp.sum(lq1_ref[...] * lk1_ref[...], axis=-1, keepdims=True))
           - jnp.exp(jnp.sum(lq2_ref[...] * lk2_ref[...], axis=-1, keepdims=True)) + lam_init)
    gain = gout_ref[...] * (1.0 - lam_init)
    buf_a, buf_b, buf_c = (sa_ref, mxa_ref), (sb_ref, mxb_ref), (sc_ref, mxc_ref)

    def load_queries(qb, qz_ref):
        start = pl.multiple_of(qb * blk, blk)
        qt = q_ref[0, pl.ds(start, blk), :].astype(F32).T
        dim = lax.broadcasted_iota(jnp.int32, qt.shape, 0)
        qz_ref[...] = jnp.concatenate([jnp.where(dim < DA_QK_DIM, qt, 0.0), jnp.where(dim >= DA_QK_DIM, qt, 0.0)],
                                      axis=1).astype(BF16)

    def score(qz_ref, t, buf):
        s_ref, mx_ref = buf
        start = pl.multiple_of(t * blk, blk)
        st = _dot(k_ref[0, pl.ds(start, blk), :], qz_ref[...])
        s_ref[...] = st
        mx_ref[...] = jnp.max(st, axis=0, keepdims=True)

    def absorb(t, buf, diagonal=False):
        s_ref, mx_ref = buf
        start = pl.multiple_of(t * blk, blk)
        tile = ATTN_TILE
        for qg in range(2 * blk // tile):
            cs = slice(qg * tile, (qg + 1) * tile)

            def scores(kh):
                st = s_ref[kh * tile:(kh + 1) * tile, cs]
                if diagonal:
                    key = lax.broadcasted_iota(jnp.int32, st.shape, 0) + kh * tile
                    qry = lax.broadcasted_iota(jnp.int32, st.shape, 1) + (qg * tile) % blk
                    st = jnp.where(key // CHUNK <= qry // CHUNK, st, NEG_BIG)
                return st

            if diagonal:
                mx = functools.reduce(jnp.maximum, [jnp.max(scores(kh), axis=0, keepdims=True)
                                                    for kh in range(blk // tile)])
            else:
                mx = mx_ref[:, cs]
            m = m_ref[:, cs]
            m_new = jnp.maximum(m, mx)
            acc = jnp.exp2(m - m_new) * acc_ref[:, cs]
            for kh in range(blk // tile):
                p = jnp.exp2(scores(kh) - m_new)
                acc = acc + _dot(vt_ref[0, :, pl.ds(start + kh * tile, tile)], p.astype(BF16))
            m_ref[:, cs] = m_new
            acc_ref[:, cs] = acc

    def repeat(first, count, two_stages):
        def twice(v, c):
            two_stages(first + 2 * v)
            two_stages(first + 2 * v + 1)
            return c

        lax.fori_loop(0, lax.shift_right_logical(count, 1), twice, 0)

        @pl.when((count & 1) == 1)
        def _():
            two_stages(first + count - 1)

    def reset():
        m_ref[...] = jnp.full(m_ref.shape, NEG_BIG, F32)
        acc_ref[...] = jnp.zeros(acc_ref.shape, F32)

    def finish(qb):
        o = acc_ref[:DA_V_DIM, :] / acc_ref[DA_V_DIM:DA_V_DIM + 1, :]
        o = o[:, :blk] - lam * o[:, blk:]
        o = o * lax.rsqrt(jnp.mean(o * o, axis=0, keepdims=True) + RMS_EPS)
        o_ref[0, pl.ds(pl.multiple_of(qb * blk, blk), blk), :] = (o.T * gain).astype(BF16)
        reset()

    reset()
    load_queries(0, qza_ref)
    load_queries(1, qzb_ref)
    score(qza_ref, 0, buf_a)
    score(qzb_ref, 0, buf_b)
    absorb(0, buf_a, diagonal=True)
    finish(0)
    score(qzb_ref, 1, buf_a)
    absorb(0, buf_b)
    load_queries(min(2, nblk - 1), qza_ref)
    score(qza_ref, 0, buf_c)
    absorb(1, buf_a, diagonal=True)
    finish(1)

    def pair(g, carry):
        a = 2 * g
        load_queries(a + 1, qzb_ref)
        score(qza_ref, 1, buf_b)
        absorb(0, buf_c)
        score(qza_ref, 2, buf_a)
        absorb(1, buf_b)

        def stages_a(u):
            score(qza_ref, 2 * u + 1, buf_b)
            absorb(2 * u, buf_a)
            score(qza_ref, 2 * u + 2, buf_a)
            absorb(2 * u + 1, buf_b)

        repeat(1, g - 1, stages_a)
        score(qzb_ref, 0, buf_b)
        absorb(a, buf_a, diagonal=True)
        finish(a)
        score(qzb_ref, 1, buf_a)
        absorb(0, buf_b)

        def stages_b(u):
            score(qzb_ref, 2 * u + 2, buf_b)
            absorb(2 * u + 1, buf_a)
            score(qzb_ref, 2 * u + 3, buf_a)
            absorb(2 * u + 2, buf_b)

        repeat(0, g, stages_b)
        load_queries(jnp.minimum(a + 2, nblk - 1), qza_ref)
        score(qza_ref, 0, buf_c)
        absorb(a + 1, buf_a, diagonal=True)
        finish(a + 1)
        return carry

    lax.fori_loop(1, nblk // 2, pair, 0)


def _diff_attn(qa, ka, va_t, lq1, lk1, lq2, lk2, gout, lam_init):
    b, s, _ = qa.shape
    blk = min(ATTN_BLOCK, s // 2)
    nblk = s // blk
    assert nblk % 2 == 0 and blk % ATTN_TILE == 0, (s, blk)
    vec = _const_spec((1, DA_QK_DIM))
    tok_spec = pl.BlockSpec((1, s, 128), lambda bi, h: (bi, 0, h))
    vt_spec = pl.BlockSpec((1, DA_V_ROWS, s), lambda bi, h: (bi, h, 0))
    scores = pltpu.VMEM((blk, 2 * blk), F32)
    stats = pltpu.VMEM((1, 2 * blk), F32)
    queries = pltpu.VMEM((2 * DA_QK_DIM, 2 * blk), BF16)
    return pl.pallas_call(
        functools.partial(_diff_attn_kernel, blk=blk, nblk=nblk, lam_init=lam_init),
        grid=(b, DA_HEADS),
        in_specs=[vec, vec, vec, vec, _const_spec((1, DA_V_DIM)), tok_spec, tok_spec, vt_spec],
        out_specs=tok_spec,
        out_shape=jax.ShapeDtypeStruct((b, s, DA_WIDTH), BF16),
        scratch_shapes=[scores, scores, scores, stats, stats, stats, queries, queries, stats,
                        pltpu.VMEM((DA_V_ROWS, 2 * blk), F32)],
        compiler_params=_params("parallel", "parallel"),
        name="diff_attention",
    )(lq1, lk1, lq2, lk2, gout, qa, ka, va_t)


def _retention_kernel(dec_ref, qdec_ref, kdec_ref, sdec_ref, smask_ref, gain_ref,
                      q_ref, k_ref, v_ref, g_ref, o_ref, state_ref):
    @pl.when(pl.program_id(1) == 0)
    def _():
        state_ref[...] = jnp.zeros_like(state_ref)

    q = q_ref[0]
    k = k_ref[0]
    v = v_ref[0]
    kb = k.astype(BF16)
    lane = lax.broadcasted_iota(jnp.int32, q.shape, 1)
    state = state_ref[...]
    o_cross = _dot((q * qdec_ref[...]).astype(BF16), state.astype(BF16))
    for h in range(RET_HEADS):
        qh = jnp.where(lane // RET_QK_DIM == h, q, 0.0).astype(BF16)
        w = (_dot_nt(qh, kb) * dec_ref[h]).astype(BF16)
        vs = slice(h * RET_V_DIM, (h + 1) * RET_V_DIM)
        o = _dot(w, v[:, vs]) + o_cross[:, vs]
        o = _rms(o, gain_ref[...])
        g = g_ref[0, :, vs]
        o_ref[0, :, vs] = (o * (g * jax.nn.sigmoid(g))).astype(BF16)
    kv = _dot_tn((k * kdec_ref[...]).astype(BF16), v)
    state_ref[...] = sdec_ref[...] * state + kv * smask_ref[...]


def _retention_tables(blk):
    f32 = np.float32
    heads = np.arange(RET_HEADS, dtype=f32)
    log_gamma = np.log(f32(1.0) - f32(2.0) ** (f32(-5.0) - heads)).astype(f32)
    idx = np.arange(blk)
    diff = np.abs(idx[:, None] - idx[None, :]).astype(f32)
    same_chunk = (idx[:, None] // CHUNK) == (idx[None, :] // CHUNK)
    visible = (idx[None, :] <= idx[:, None]) | same_chunk
    dec = np.where(visible[None], np.exp(log_gamma[:, None, None] * diff[None]), f32(0.0)).astype(f32)
    per_lane = np.repeat(log_gamma, RET_QK_DIM)
    pos = np.arange(blk, dtype=f32)
    qdec = np.exp((pos[:, None] + f32(1.0)) * per_lane[None, :]).astype(f32)
    kdec = np.exp((f32(blk - 1.0) - pos)[:, None] * per_lane[None, :]).astype(f32)
    sdec = np.exp(f32(blk) * per_lane)[:, None].astype(f32)
    smask = (np.arange(RET_HEADS * RET_QK_DIM)[:, None] // RET_QK_DIM
             == np.arange(RET_WIDTH)[None, :] // RET_V_DIM).astype(f32)
    return tuple(jnp.asarray(z) for z in (dec, qdec, kdec, sdec, smask))


def _retention(qr, kr, vr, gr, gain):
    b, s, _ = qr.shape
    blk = min(RET_BLOCK, s)
    dec, qdec, kdec, sdec, smask = _retention_tables(blk)
    tok = lambda width: pl.BlockSpec((1, blk, width), lambda bi, i: (bi, i, 0))
    return pl.pallas_call(
        _retention_kernel,
        grid=(b, s // blk),
        in_specs=[_const_spec(dec.shape), _const_spec(qdec.shape), _const_spec(kdec.shape),
                  _const_spec(sdec.shape), _const_spec(smask.shape), _const_spec((1, RET_V_DIM)),
                  tok(256), tok(256), tok(512), tok(512)],
        out_specs=tok(RET_WIDTH),
        out_shape=jax.ShapeDtypeStruct((b, s, RET_WIDTH), BF16),
        scratch_shapes=[pltpu.VMEM((RET_HEADS * RET_QK_DIM, RET_WIDTH), F32)],
        compiler_params=_params("parallel", "arbitrary"),
        name="retention",
    )(dec, qdec, kdec, sdec, smask, gain, qr, kr, vr, gr)


def _mlp_tail(x1, g_ref, wup_ref, wdn_ref, o_ref):
    h = _rms(x1, g_ref[...]).astype(BF16)
    acc = x1
    for c in range(MLP_HIDDEN // MLP_HIDDEN_TILE):
        cs = slice(c * MLP_HIDDEN_TILE, (c + 1) * MLP_HIDDEN_TILE)
        u = jnp.maximum(_dot(h, wup_ref[:, cs]), 0.0)
        acc = acc + _dot((u * u).astype(BF16), wdn_ref[cs, :])
    o_ref[...] = acc


def _out_mlp_kernel(x_ref, a_ref, b_ref, wout_ref, g_ref, wup_ref, wdn_ref, o_ref):
    ka = a_ref.shape[1]
    mixed = _dot(a_ref[...], wout_ref[:ka, :]) + _dot(b_ref[...], wout_ref[ka:, :])
    _mlp_tail(x_ref[...] + mixed, g_ref, wup_ref, wdn_ref, o_ref)


def _s5_out_mlp_kernel(x_ref, yt_ref, ut_ref, d_ref, wglut_ref, b_ref, wout_ref, g_ref, wup_ref, wdn_ref, o_ref):
    y = yt_ref[...] + d_ref[...] * ut_ref[...]
    z = jax.nn.gelu(y)
    a_t = (z * jax.nn.sigmoid(_dot(wglut_ref[...], z.astype(BF16)))).astype(BF16)
    mixed = _dot_tn(a_t, wout_ref[:S5_WIDTH, :]) + _dot(b_ref[...], wout_ref[S5_WIDTH:, :])
    _mlp_tail(x_ref[...] + mixed, g_ref, wup_ref, wdn_ref, o_ref)


def _mlp_specs():
    return [_const_spec((D_MODEL, D_MODEL)), _const_spec((1, D_MODEL)),
            _const_spec((D_MODEL, MLP_HIDDEN)), _const_spec((MLP_HIDDEN, D_MODEL))]


def _out_mlp(x2, a, bb, wout, g, wup, wdn):
    t = x2.shape[0]
    tm = TOKEN_TILE
    row = lambda width: pl.BlockSpec((tm, width), lambda i: (i, 0))
    return pl.pallas_call(
        _out_mlp_kernel,
        grid=(t // tm,),
        in_specs=[row(D_MODEL), row(a.shape[1]), row(bb.shape[1])] + _mlp_specs(),
        out_specs=row(D_MODEL),
        out_shape=jax.ShapeDtypeStruct((t, D_MODEL), F32),
        compiler_params=_params("parallel"),
        name="ab_out_mlp",
    )(x2, a, bb, wout, g, wup, wdn)


def _s5_out_mlp(x2, y_t, u_t, d_col, wglu_t, bb, wout, g, wup, wdn):
    t = x2.shape[0]
    tm = TOKEN_TILE
    row = lambda width: pl.BlockSpec((tm, width), lambda i: (i, 0))
    col = pl.BlockSpec((S5_WIDTH, tm), lambda i: (0, i))
    return pl.pallas_call(
        _s5_out_mlp_kernel,
        grid=(t // tm,),
        in_specs=[row(D_MODEL), col, col, _const_spec((S5_WIDTH, 1)),
                  _const_spec((S5_WIDTH, S5_WIDTH)), row(GLA_WIDTH)] + _mlp_specs(),
        out_specs=row(D_MODEL),
        out_shape=jax.ShapeDtypeStruct((t, D_MODEL), F32),
        compiler_params=_params("parallel"),
        name="cd_out_mlp",
    )(x2, y_t, u_t, d_col, wglu_t, bb, wout, g, wup, wdn)


def _cd_in_kernel(x_ref, g_ref, w_ref, wut_ref, wa_ref, ba_ref, ut_ref, q_ref, k_ref, v_ref, r_ref, la_ref):
    h = _rms(x_ref[...], g_ref[...]).astype(BF16)

    def proj(lo, hi):
        return _dot(h, w_ref[:, lo:hi])

    ua_t = _dot_nt(wut_ref[...], h)
    ut_ref[...] = ua_t[:S5_WIDTH]
    q_ref[...] = proj(256, 640) * (GLA_QK_DIM ** -0.5)
    k_ref[...] = proj(640, 1024)
    v_ref[...] = proj(1024, 1792).astype(BF16)
    r_ref[...] = proj(1792, 2560)
    pre = _dot_tn(ua_t[S5_WIDTH:].astype(BF16), wa_ref[...]) + ba_ref[...]
    log_sig = jnp.minimum(pre, 0.0) - jnp.log1p(jnp.exp(-jnp.abs(pre)))
    la_ref[...] = log_sig / GLA_TAU


def _cd_in(x2, g, w, wu_t, wa, ba):
    t = x2.shape[0]
    tm = TOKEN_TILE
    row = lambda width: pl.BlockSpec((tm, width), lambda i: (i, 0))
    out_shapes = (
        jax.ShapeDtypeStruct((S5_WIDTH, t), F32), jax.ShapeDtypeStruct((t, GLA_QK_WIDTH), F32),
        jax.ShapeDtypeStruct((t, GLA_QK_WIDTH), F32), jax.ShapeDtypeStruct((t, GLA_WIDTH), BF16),
        jax.ShapeDtypeStruct((t, GLA_WIDTH), F32), jax.ShapeDtypeStruct((t, GLA_QK_WIDTH), F32))
    return pl.pallas_call(
        _cd_in_kernel,
        grid=(t // tm,),
        in_specs=[row(D_MODEL), _const_spec((1, D_MODEL)), _const_spec((D_MODEL, CD_IN_MAIN)),
                  _const_spec((S5_WIDTH + GATE_ROWS, D_MODEL)),
                  _const_spec((GATE_ROWS, GLA_QK_WIDTH)), _const_spec((1, GLA_QK_WIDTH))],
        out_specs=(pl.BlockSpec((S5_WIDTH, tm), lambda i: (0, i)), row(GLA_QK_WIDTH), row(GLA_QK_WIDTH),
                   row(GLA_WIDTH), row(GLA_WIDTH), row(GLA_QK_WIDTH)),
        out_shape=out_shapes,
        compiler_params=_params("parallel"),
        name="cd_in_proj",
    )(x2, g, w, wu_t, wa, ba)


def _split3(x):
    hi = x.astype(BF16)
    r = x - hi.astype(F32)
    mid = r.astype(BF16)
    lo = (r - mid.astype(F32)).astype(BF16)
    return hi, mid, lo


def _gla_kernel(tri_ref, gain_ref, q_ref, k_ref, v_ref, r_ref, la_ref, o_ref, state_ref, *, blk, batch):
    @pl.when(pl.program_id(0) == 0)
    def _():
        state_ref[...] = jnp.zeros_like(state_ref)

    tri = tri_ref[...]
    hi, mid, lo = _split3(jnp.concatenate([la_ref[bi] for bi in range(batch)], axis=1))
    b_all = _dot(tri, hi) + _dot(tri, mid) + _dot(tri, lo)
    for bi in range(batch):
        _gla_block(bi, b_all[:, bi * GLA_QK_WIDTH:(bi + 1) * GLA_QK_WIDTH], gain_ref, q_ref, k_ref, v_ref, r_ref,
                   o_ref, state_ref, blk)


def _gla_block(bi, b, gain_ref, q_ref, k_ref, v_ref, r_ref, o_ref, state_ref, blk):
    nc = blk // CHUNK
    pairs = GLA_HEADS // 2
    q = q_ref[bi]
    k = k_ref[bi]
    b3 = b.reshape(nc, CHUNK, GLA_QK_WIDTH)
    b_last = b3[:, CHUNK - 1:CHUNK, :]
    e_pos = jnp.exp(b)
    e_neg = jnp.exp(-b)
    k_tail = jnp.exp(b_last - b3).reshape(blk, GLA_QK_WIDTH)
    chunk_decay = jnp.exp(b_last)
    qp = (q * e_pos).astype(BF16)
    qn = (q * e_neg).astype(BF16)
    kp = (k * e_pos).astype(BF16)
    kn = (k * e_neg).astype(BF16)
    kw = (k * k_tail).astype(BF16)

    row = lax.broadcasted_iota(jnp.int32, (blk, blk), 0)
    col = lax.broadcasted_iota(jnp.int32, (blk, blk), 1)
    same_chunk = row // CHUNK == col // CHUNK
    causal = row >= col
    lane = lax.broadcasted_iota(jnp.int32, (blk, 2 * GLA_QK_DIM), 1)
    smask = (lax.broadcasted_iota(jnp.int32, (2 * GLA_V_DIM, 2 * GLA_QK_DIM), 0) // GLA_V_DIM
             == lax.broadcasted_iota(jnp.int32, (2 * GLA_V_DIM, 2 * GLA_QK_DIM), 1) // GLA_QK_DIM)

    for p in range(pairs):
        ks = slice(p * 2 * GLA_QK_DIM, (p + 1) * 2 * GLA_QK_DIM)
        vs = slice(p * 2 * GLA_V_DIM, (p + 1) * 2 * GLA_V_DIM)
        v = v_ref[bi, :, vs]
        scores = []
        for half in range(2):
            sel = (lane // GLA_QK_DIM) == half
            fwd = _dot_nt(jnp.where(sel, qp[:, ks], 0).astype(BF16), kn[:, ks])
            bwd = _dot_nt(jnp.where(sel, qn[:, ks], 0).astype(BF16), kp[:, ks])
            scores.append(jnp.where(same_chunk, jnp.where(causal, fwd, bwd), 0.0).astype(BF16))
        vlane = lax.broadcasted_iota(jnp.int32, v.shape, 1) // GLA_V_DIM
        v_diag = jnp.concatenate([jnp.where(vlane == 0, v, 0), jnp.where(vlane == 1, v, 0)], axis=0)
        o_intra = _dot(jnp.concatenate(scores, axis=1), v_diag)

        state = state_ref[bi * pairs + p]
        cross = []
        for c in range(nc):
            rs = slice(c * CHUNK, (c + 1) * CHUNK)
            cross.append(_dot_nt(qp[rs, ks], state.astype(BF16)))
            kv_t = _dot_tn(v[rs, :], kw[rs, ks])
            state = state * chunk_decay[c][:, ks] + jnp.where(smask, kv_t, 0.0)
        state_ref[bi * pairs + p] = state
        o = o_intra + jnp.concatenate(cross, axis=0)
        for half in range(2):
            hs = slice(half * GLA_V_DIM, (half + 1) * GLA_V_DIM)
            os_ = slice(p * 2 * GLA_V_DIM + half * GLA_V_DIM, p * 2 * GLA_V_DIM + (half + 1) * GLA_V_DIM)
            g = r_ref[bi, :, os_]
            o_ref[bi, :, os_] = (_rms(o[:, hs], gain_ref[...]) * (g * jax.nn.sigmoid(g))).astype(BF16)


def _gla(qg, kg, vg, rg, la, gain):
    b, s, _ = qg.shape
    blk = min(GLA_BLOCK, s)
    idx = np.arange(blk)
    tri = ((idx[:, None] // CHUNK) == (idx[None, :] // CHUNK)) & (idx[None, :] <= idx[:, None])
    tri = jnp.asarray(tri.astype(np.float32)).astype(BF16)
    tok = lambda width: pl.BlockSpec((b, blk, width), lambda i: (0, i, 0))
    return pl.pallas_call(
        functools.partial(_gla_kernel, blk=blk, batch=b),
        grid=(s // blk,),
        in_specs=[_const_spec((blk, blk)), _const_spec((1, GLA_V_DIM)),
                  tok(GLA_QK_WIDTH), tok(GLA_QK_WIDTH), tok(GLA_WIDTH), tok(GLA_WIDTH), tok(GLA_QK_WIDTH)],
        out_specs=tok(GLA_WIDTH),
        out_shape=jax.ShapeDtypeStruct((b, s, GLA_WIDTH), BF16),
        scratch_shapes=[pltpu.VMEM((b * (GLA_HEADS // 2), 2 * GLA_V_DIM, 2 * GLA_QK_DIM), F32)],
        compiler_params=_params("arbitrary"),
        name="gla",
    )(tri, gain, qg, kg, vg, rg, la)


def _s5_kernel(u_ref, kern_ref, fre_ref, fim_ref, ere_ref, eim_ref, lre_ref, lim_ref, y_ref,
               toep_ref, vre_ref, vim_ref, hre_ref, him_ref, *, batch, chunks):
    length = S5_CHUNK
    src = lax.broadcasted_iota(jnp.int32, (length, length), 0)
    dst = lax.broadcasted_iota(jnp.int32, (length, length), 1)

    def build(ci, carry):
        for co in range(S5_GROUP):
            lags = kern_ref[0, pl.ds(ci * S5_GROUP + co, 1), :]
            blk = pltpu.roll(jnp.broadcast_to(lags, (length, length)), 0, 1, stride=1, stride_axis=0)
            blk = jnp.where(dst >= src, blk, 0.0)
            toep_ref[pl.ds(pl.multiple_of(ci * length, length), length), co * length:(co + 1) * length] = (
                blk.astype(BF16))
        return carry

    lax.fori_loop(0, S5_GROUP, build, 0)

    u = jnp.concatenate([u_ref[0, c] for c in range(S5_GROUP)], axis=1).astype(BF16)
    vre_ref[...] = _dot(u, fre_ref[0])
    vim_ref[...] = _dot(u, fim_ref[0])
    lre = lre_ref[0]
    lim = lim_ref[0]

    def step(n, carry):
        new = []
        for bi in range(batch):
            hr, hi = carry[2 * bi], carry[2 * bi + 1]
            r = bi * chunks + n
            hre_ref[pl.ds(r, 1), :] = hr
            him_ref[pl.ds(r, 1), :] = hi
            vr = vre_ref[pl.ds(r, 1), :]
            vi = vim_ref[pl.ds(r, 1), :]
            new += [lre * hr - lim * hi + vr, lre * hi + lim * hr + vi]
        return tuple(new)

    zero = jnp.zeros((1, LANES_V7X), F32)
    lax.fori_loop(0, chunks, step, (zero,) * (2 * batch))
    y = (_dot(u, toep_ref[...]) + _dot(hre_ref[...].astype(BF16), ere_ref[0])
         + _dot(him_ref[...].astype(BF16), eim_ref[0]))
    for c in range(S5_GROUP):
        y_ref[0, c] = y[:, c * length:(c + 1) * length]


def _s5_tables(a_re, a_im, log_step, b_re, b_im, c_re, c_im):
    hp = lax.Precision.HIGHEST
    g, p, c, length = S5_GROUPS, LANES_V7X, S5_GROUP, S5_CHUNK
    extra = p - S5_STATE
    a_re = jnp.pad(a_re.astype(F32), ((0, 0), (0, extra)), constant_values=-1.0)
    a_im = jnp.pad(a_im.astype(F32), ((0, 0), (0, extra)))
    b_re, b_im = (jnp.pad(z.astype(F32), ((0, 0), (0, extra), (0, 0))) for z in (b_re, b_im))
    c_re, c_im = (jnp.pad(z.astype(F32), ((0, 0), (0, 0), (0, extra))) for z in (c_re, c_im))
    delta = jnp.exp(log_step.astype(F32))[:, None]
    tau = jnp.arange(length + 1, dtype=F32)[None, :, None]
    mag = jnp.exp((a_re * delta)[:, None, :] * tau)
    ang = (a_im * delta)[:, None, :] * tau
    pw_re, pw_im = mag * jnp.cos(ang), mag * jnp.sin(ang)
    n_re, n_im = pw_re[:, 1, :] - 1.0, pw_im[:, 1, :]
    den = a_re * a_re + a_im * a_im
    q_re, q_im = (n_re * a_re + n_im * a_im) / den, (n_im * a_re - n_re * a_im) / den
    bb_re = q_re[:, :, None] * b_re - q_im[:, :, None] * b_im
    bb_im = q_re[:, :, None] * b_im + q_im[:, :, None] * b_re
    first_re, first_im = pw_re[:, :length, None, :], pw_im[:, :length, None, :]
    cp_re = c_re[:, None] * first_re - c_im[:, None] * first_im
    cp_im = c_re[:, None] * first_im + c_im[:, None] * first_re
    kern = (jnp.einsum('gtcp,gpd->gdct', cp_re, bb_re, precision=hp)
            - jnp.einsum('gtcp,gpd->gdct', cp_im, bb_im, precision=hp)).reshape(g, c * c, length)
    rev_re, rev_im = pw_re[:, length - 1::-1, :][:, None], pw_im[:, length - 1::-1, :][:, None]
    bt_re, bt_im = bb_re.transpose(0, 2, 1)[:, :, None, :], bb_im.transpose(0, 2, 1)[:, :, None, :]
    f_re = (rev_re * bt_re - rev_im * bt_im).reshape(g, c * length, p)
    f_im = (rev_re * bt_im + rev_im * bt_re).reshape(g, c * length, p)
    nx_re = pw_re[:, 1:, :].transpose(0, 2, 1)[:, :, None, :]
    nx_im = pw_im[:, 1:, :].transpose(0, 2, 1)[:, :, None, :]
    ct_re, ct_im = c_re.transpose(0, 2, 1)[:, :, :, None], c_im.transpose(0, 2, 1)[:, :, :, None]
    e_re = (ct_re * nx_re - ct_im * nx_im).reshape(g, p, c * length)
    e_im = (ct_re * nx_im + ct_im * nx_re).reshape(g, p, c * length)
    return (kern, f_re.astype(BF16), f_im.astype(BF16), e_re.astype(BF16), (-e_im).astype(BF16),
            pw_re[:, length:, :], pw_im[:, length:, :])


def _s5(u_t, tables, batch):
    t = u_t.shape[1]
    length = S5_CHUNK
    rows = t // length
    width = length * S5_GROUP
    kern, fre, fim, ere, eim, lre, lim = tables
    grp = lambda r, c: pl.BlockSpec((1, r, c), lambda gi: (gi, 0, 0))
    io_spec = pl.BlockSpec((1, S5_GROUP, rows, length), lambda gi: (gi, 0, 0, 0))
    y = pl.pallas_call(
        functools.partial(_s5_kernel, batch=batch, chunks=rows // batch),
        grid=(S5_GROUPS,),
        in_specs=[io_spec, grp(S5_GROUP * S5_GROUP, length), grp(width, LANES_V7X), grp(width, LANES_V7X),
                  grp(LANES_V7X, width), grp(LANES_V7X, width), grp(1, LANES_V7X), grp(1, LANES_V7X)],
        out_specs=io_spec,
        out_shape=jax.ShapeDtypeStruct((S5_GROUPS, S5_GROUP, rows, length), F32),
        scratch_shapes=[pltpu.VMEM((width, width), BF16)] + [pltpu.VMEM((rows, LANES_V7X), F32)] * 4,
        compiler_params=_params("parallel"),
        name="s5",
    )(u_t.reshape(S5_GROUPS, S5_GROUP, rows, length), kern, fre, fim, ere, eim, lre, lim)
    return y.reshape(S5_WIDTH, t)


def _rotary_tables(seq):
    half = RET_QK_DIM // 2
    inv_freq = 1.0 / (ROPE_BASE ** jnp.linspace(0.0, 1.0, half, dtype=F32))
    ang = jnp.arange(seq, dtype=F32)[:, None] * inv_freq[None, :]
    cos, sin = jnp.cos(ang), jnp.sin(ang)
    cos_t = jnp.tile(jnp.concatenate([cos, cos], axis=1), (1, RET_HEADS))
    sin_t = jnp.tile(jnp.concatenate([-sin, sin], axis=1), (1, RET_HEADS))
    return cos_t, sin_t


def _row(v):
    return v.reshape(1, -1).astype(F32)


def kernel(x, norm_mix_g, norm_mlp_g, w_up, w_down, ab_w_in, ab_w_out, da_q_norm, da_k_norm,
           da_lam_q1, da_lam_k1, da_lam_q2, da_lam_k2, da_out_norm, ret_out_norm, cd_w_in, cd_w_out,
           s5_a_re, s5_a_im, s5_log_step, s5_b_re, s5_b_im, s5_c_re, s5_c_im, s5_d, s5_w_glu,
           gla_w_a2, gla_b_a2, gla_out_norm):
    bsz, seq, _ = x.shape
    t = bsz * seq
    x2 = x.reshape(t, D_MODEL)

    cos_t, sin_t = _rotary_tables(seq)
    gsum = (np.arange(512)[:, None] // DA_QK_DIM) == (np.arange(512)[None, :] // DA_QK_DIM)
    gsum = jnp.asarray(gsum.astype(np.float32) / DA_QK_DIM).astype(BF16)
    gq = _row(jnp.tile(da_q_norm[0], 2 * DA_HEADS)) * (DA_QK_DIM ** -0.5 * math.log2(math.e))
    gk = _row(jnp.tile(da_k_norm[0], 2 * DA_HEADS))
    w_ab = ab_w_in[0].astype(BF16)
    qa, ka, va_t, qr, kr, vr, gr = _ab_in(x2, _row(norm_mix_g[0]), w_ab, w_ab[:, 1024:1536].T, gq, gk, gsum,
                                          cos_t, sin_t, seq)
    lam_init = 0.8 - 0.6 * math.exp(-0.3 * 0)
    seq3 = lambda a: a.reshape(bsz, seq, a.shape[-1])
    o_a = _diff_attn(seq3(qa), seq3(ka), va_t, _row(da_lam_q1[0]), _row(da_lam_k1[0]),
                     _row(da_lam_q2[0]), _row(da_lam_k2[0]), _row(da_out_norm[0]), lam_init)
    o_r = _retention(seq3(qr), seq3(kr), seq3(vr), seq3(gr), _row(ret_out_norm[0]))
    x2 = _out_mlp(x2, o_a.reshape(t, DA_WIDTH), o_r.reshape(t, RET_WIDTH), ab_w_out[0].astype(BF16),
                  _row(norm_mlp_g[0]), w_up[0].astype(BF16), w_down[0].astype(BF16))

    w_cd = cd_w_in[0].astype(BF16)
    gate_pad = GATE_ROWS - GLA_GATE_RANK
    wua_t = jnp.pad(jnp.concatenate([w_cd[:, :S5_WIDTH], w_cd[:, CD_IN_MAIN:]], axis=1).T, ((0, gate_pad), (0, 0)))
    wa = jnp.pad(gla_w_a2[0], ((0, gate_pad), (0, 0))).astype(BF16)
    u_t, qg, kg, vg, rg, la = _cd_in(x2, _row(norm_mix_g[1]), w_cd[:, :CD_IN_MAIN], wua_t, wa, _row(gla_b_a2[0]))
    o_d = _gla(seq3(qg), seq3(kg), seq3(vg), seq3(rg), seq3(la), _row(gla_out_norm[0]))
    tables = _s5_tables(s5_a_re[0], s5_a_im[0], s5_log_step[0], s5_b_re[0], s5_b_im[0], s5_c_re[0], s5_c_im[0])
    y_t = _s5(u_t, tables, bsz)
    x2 = _s5_out_mlp(x2, y_t, u_t, s5_d[0].reshape(S5_WIDTH, 1).astype(F32), s5_w_glu[0].T.astype(BF16),
                     o_d.reshape(t, GLA_WIDTH), cd_w_out[0].astype(BF16), _row(norm_mlp_g[1]),
                     w_up[1].astype(BF16), w_down[1].astype(BF16))
    return x2.reshape(bsz, seq, D_MODEL)
```

```python
import functools
import math

import jax
import jax.numpy as jnp
from jax import lax
from jax.experimental import pallas as pl
from jax.experimental.pallas import tpu as pltpu

F32 = jnp.float32
BF16 = jnp.bfloat16

D_MODEL = 1024
CHUNK = 64
RMS_EPS = 1e-6
ROPE_BASE = 10000.0
DA_HEADS = 4
DA_QK_DIM = 64
DA_V_DIM = 128
DA_WIDTH = DA_HEADS * DA_V_DIM
DA_V_ROWS = DA_V_DIM + 16
RET_HEADS = 4
RET_QK_DIM = 64
RET_V_DIM = 128
RET_WIDTH = RET_HEADS * RET_V_DIM
S5_WIDTH = 256
S5_GROUP = 16
S5_GROUPS = S5_WIDTH // S5_GROUP
S5_STATE = 64
GLA_HEADS = 6
GLA_QK_DIM = 64
GLA_V_DIM = 128
GLA_WIDTH = GLA_HEADS * GLA_V_DIM
GLA_QK_WIDTH = GLA_HEADS * GLA_QK_DIM
GLA_GATE_RANK = 16
GLA_TAU = 16.0
MLP_HIDDEN = 4 * D_MODEL
AB_IN = 3072
CD_IN = 2576

LANES_V7X = 128
VMEM_LIMIT_BYTES_V7X = 56 * 1024 * 1024

CD_IN_MAIN = CD_IN - GLA_GATE_RANK
GATE_ROWS = 32
NEG_BIG = -1e30

TOKEN_TILE = 512
ATTN_BLOCK = 512
ATTN_TILE = 256
RET_BLOCK = 512
GLA_BLOCK = 256
S5_CHUNK = LANES_V7X
MLP_HIDDEN_TILE = 1024


def _params(*semantics):
    return pltpu.CompilerParams(dimension_semantics=semantics, vmem_limit_bytes=VMEM_LIMIT_BYTES_V7X)


def _const_spec(shape):
    zeros = (0,) * len(shape)
    return pl.BlockSpec(shape, lambda *_: zeros, pipeline_mode=pl.Buffered(1))


def _rms(xf, gain):
    return xf * lax.rsqrt(jnp.mean(xf * xf, axis=-1, keepdims=True) + RMS_EPS) * gain


def _dot(a, b):
    return jnp.dot(a, b, preferred_element_type=F32)


def _dot_nt(a, b):
    return lax.dot_general(a, b, (((1,), (1,)), ((), ())), preferred_element_type=F32)


def _dot_tn(a, b):
    return lax.dot_general(a, b, (((0,), (0,)), ((), ())), preferred_element_type=F32)


def _swap_halves(x, group):
    n = x.shape[-1]
    half = group // 2
    lane = lax.broadcasted_iota(jnp.int32, x.shape, x.ndim - 1)
    from_right = pltpu.roll(x, n - half, axis=x.ndim - 1)
    from_left = pltpu.roll(x, half, axis=x.ndim - 1)
    return jnp.where((lane % group) < half, from_right, from_left)


def _ab_in_kernel(x_ref, g_ref, w_ref, wvt_ref, gq_ref, gk_ref, gsum_ref, cos_ref, sin_ref,
                  qa_ref, ka_ref, va_ref, qr_ref, kr_ref, vr_ref, gr_ref):
    h = _rms(x_ref[...], g_ref[...]).astype(BF16)

    def proj(lo, hi):
        return _dot(h, w_ref[:, lo:hi])

    def qk_norm(y, gain):
        ms = _dot((y * y).astype(BF16), gsum_ref[...])
        return (y * lax.rsqrt(ms + RMS_EPS) * gain).astype(BF16)

    def rotary(y):
        return y * cos_ref[...] + _swap_halves(y, RET_QK_DIM) * sin_ref[...]

    qa_ref[...] = qk_norm(proj(0, 512), gq_ref[...])
    ka_ref[...] = qk_norm(proj(512, 1024), gk_ref[...])
    vt = _dot_nt(wvt_ref[...], h).astype(BF16)
    pad_row = lax.broadcasted_iota(jnp.int32, (DA_V_ROWS - DA_V_DIM, vt.shape[1]), 0)
    ones_rows = jnp.where(pad_row == 0, 1.0, 0.0).astype(BF16)
    for hd in range(DA_HEADS):
        va_ref[0, hd * DA_V_ROWS:hd * DA_V_ROWS + DA_V_DIM, :] = vt[hd * DA_V_DIM:(hd + 1) * DA_V_DIM]
        va_ref[0, hd * DA_V_ROWS + DA_V_DIM:(hd + 1) * DA_V_ROWS, :] = ones_rows
    qr_ref[...] = rotary(proj(1536, 1792))
    kr_ref[...] = rotary(proj(1792, 2048)) * (RET_QK_DIM ** -0.5)
    vr_ref[...] = proj(2048, 2560).astype(BF16)
    gr_ref[...] = proj(2560, 3072)


def _ab_in(x2, g, w, wv_t, gq, gk, gsum, cos_t, sin_t, seq):
    t = x2.shape[0]
    tm = min(TOKEN_TILE, seq)
    per_seq = seq // tm
    row = lambda width: pl.BlockSpec((tm, width), lambda i: (i, 0))
    va_spec = pl.BlockSpec((1, DA_HEADS * DA_V_ROWS, tm), lambda i: (i // per_seq, 0, i % per_seq))
    rot_spec = pl.BlockSpec((tm, 256), lambda i: (i % per_seq, 0))
    out_shapes = (
        jax.ShapeDtypeStruct((t, 512), BF16), jax.ShapeDtypeStruct((t, 512), BF16),
        jax.ShapeDtypeStruct((t // seq, DA_HEADS * DA_V_ROWS, seq), BF16), jax.ShapeDtypeStruct((t, 256), F32),
        jax.ShapeDtypeStruct((t, 256), F32), jax.ShapeDtypeStruct((t, 512), BF16),
        jax.ShapeDtypeStruct((t, 512), F32))
    return pl.pallas_call(
        _ab_in_kernel,
        grid=(t // tm,),
        in_specs=[row(D_MODEL), _const_spec((1, D_MODEL)), _const_spec((D_MODEL, AB_IN)),
                  _const_spec((DA_WIDTH, D_MODEL)),
                  _const_spec((1, 512)), _const_spec((1, 512)), _const_spec((512, 512)),
                  rot_spec, rot_spec],
        out_specs=(row(512), row(512), va_spec, row(256), row(256), row(512), row(512)),
        out_shape=out_shapes,
        compiler_params=_params("parallel"),
        name="ab_in_proj",
    )(x2, g, w, wv_t, gq, gk, gsum, cos_t, sin_t)


def _diff_attn_kernel(lq1_ref, lk1_ref, lq2_ref, lk2_ref, gout_ref, q_ref, k_ref, vt_ref, o_ref,
                      sa_ref, sb_ref, sc_ref, mxa_ref, mxb_ref, mxc_ref, qza_ref, qzb_ref, m_ref, acc_ref,
                      *, blk, nblk, lam_init):
    lam = (jnp.exp(jnp.sum(lq1_ref[...] * lk1_ref[...], axis=-1, keepdims=True))
           - jnp.exp(jnp.sum(lq2_ref[...] * lk2_ref[...], axis=-1, keepdims=True)) + lam_init)
    gain = gout_ref[...] * (1.0 - lam_init)
    buf_a, buf_b, buf_c = (sa_ref, mxa_ref), (sb_ref, mxb_ref), (sc_ref, mxc_ref)

    def load_queries(qb, qz_ref):
        start = pl.multiple_of(qb * blk, blk)
        qt = q_ref[0, pl.ds(start, blk), :].astype(F32).T
        dim = lax.broadcasted_iota(jnp.int32, qt.shape, 0)
        qz_ref[...] = jnp.concatenate([jnp.where(dim < DA_QK_DIM, qt, 0.0), jnp.where(dim >= DA_QK_DIM, qt, 0.0)],
                                      axis=1).astype(BF16)

    def score(qz_ref, t, buf):
        s_ref, mx_ref = buf
        start = pl.multiple_of(t * blk, blk)
        st = _dot(k_ref[0, pl.ds(start, blk), :], qz_ref[...])
        s_ref[...] = st
        mx_ref[...] = jnp.max(st, axis=0, keepdims=True)

    def absorb(t, buf, diagonal=False):
        s_ref, mx_ref = buf
        start = pl.multiple_of(t * blk, blk)
        tile = ATTN_TILE
        for qg in range(2 * blk // tile):
            cs = slice(qg * tile, (qg + 1) * tile)
            qs = (qg * tile) % blk
            key_tiles = [kh for kh in range(blk // tile) if not (diagonal and kh * tile >= qs + tile)]

            def scores(kh):
                st = s_ref[kh * tile:(kh + 1) * tile, cs]
                if diagonal and (kh + 1) * tile > qs:
                    key = lax.broadcasted_iota(jnp.int32, st.shape, 0) + kh * tile
                    qry = lax.broadcasted_iota(jnp.int32, st.shape, 1) + qs
                    st = jnp.where(key // CHUNK <= qry // CHUNK, st, NEG_BIG)
                return st

            if diagonal:
                mx = functools.reduce(jnp.maximum, [jnp.max(scores(kh), axis=0, keepdims=True) for kh in key_tiles])
            else:
                mx = mx_ref[:, cs]
            m = m_ref[:, cs]
            m_new = jnp.maximum(m, mx)
            acc = jnp.exp2(m - m_new) * acc_ref[:, cs]
            for kh in key_tiles:
                p = jnp.exp2(scores(kh) - m_new)
                acc = acc + _dot(vt_ref[0, :, pl.ds(start + kh * tile, tile)], p.astype(BF16))
            m_ref[:, cs] = m_new
            acc_ref[:, cs] = acc

    def repeat(first, count, two_stages):
        def twice(v, c):
            two_stages(first + 2 * v)
            two_stages(first + 2 * v + 1)
            return c

        lax.fori_loop(0, lax.shift_right_logical(count, 1), twice, 0)

        @pl.when((count & 1) == 1)
        def _():
            two_stages(first + count - 1)

    def reset():
        m_ref[...] = jnp.full(m_ref.shape, NEG_BIG, F32)
        acc_ref[...] = jnp.zeros(acc_ref.shape, F32)

    def finish(qb):
        o = acc_ref[:DA_V_DIM, :] / acc_ref[DA_V_DIM:DA_V_DIM + 1, :]
        o = o[:, :blk] - lam * o[:, blk:]
        o = o * lax.rsqrt(jnp.mean(o * o, axis=0, keepdims=True) + RMS_EPS)
        o_ref[0, pl.ds(pl.multiple_of(qb * blk, blk), blk), :] = (o.T * gain).astype(BF16)
        reset()

    reset()
    load_queries(0, qza_ref)
    load_queries(1, qzb_ref)
    score(qza_ref, 0, buf_a)
    score(qzb_ref, 0, buf_b)
    absorb(0, buf_a, diagonal=True)
    finish(0)
    score(qzb_ref, 1, buf_a)
    absorb(0, buf_b)
    load_queries(min(2, nblk - 1), qza_ref)
    score(qza_ref, 0, buf_c)
    absorb(1, buf_a, diagonal=True)
    finish(1)

    def pair(g, carry):
        a = 2 * g
        load_queries(a + 1, qzb_ref)
        score(qza_ref, 1, buf_b)
        absorb(0, buf_c)
        score(qza_ref, 2, buf_a)
        absorb(1, buf_b)

        def stages_a(u):
            score(qza_ref, 2 * u + 1, buf_b)
            absorb(2 * u, buf_a)
            score(qza_ref, 2 * u + 2, buf_a)
            absorb(2 * u + 1, buf_b)

        repeat(1, g - 1, stages_a)
        score(qzb_ref, 0, buf_b)
        absorb(a, buf_a, diagonal=True)
        finish(a)
        score(qzb_ref, 1, buf_a)
        absorb(0, buf_b)

        def stages_b(u):
            score(qzb_ref, 2 * u + 2, buf_b)
            absorb(2 * u + 1, buf_a)
            score(qzb_ref, 2 * u + 3, buf_a)
            absorb(2 * u + 2, buf_b)

        repeat(0, g, stages_b)
        load_queries(jnp.minimum(a + 2, nblk - 1), qza_ref)
        score(qza_ref, 0, buf_c)
        absorb(a + 1, buf_a, diagonal=True)
        finish(a + 1)
        return carry

    lax.fori_loop(1, nblk // 2, pair, 0)


def _diff_attn(qa, ka, va_t, lq1, lk1, lq2, lk2, gout, lam_init):
    b, s, _ = qa.shape
    blk = min(ATTN_BLOCK, s // 2)
    nblk = s // blk
    assert nblk % 2 == 0 and blk % ATTN_TILE == 0, (s, blk)
    vec = _const_spec((1, DA_QK_DIM))
    tok_spec = pl.BlockSpec((1, s, 128), lambda bi, h: (bi, 0, h))
    vt_spec = pl.BlockSpec((1, DA_V_ROWS, s), lambda bi, h: (bi, h, 0))
    scores = pltpu.VMEM((blk, 2 * blk), F32)
    stats = pltpu.VMEM((1, 2 * blk), F32)
    queries = pltpu.VMEM((2 * DA_QK_DIM, 2 * blk), BF16)
    return pl.pallas_call(
        functools.partial(_diff_attn_kernel, blk=blk, nblk=nblk, lam_init=lam_init),
        grid=(b, DA_HEADS),
        in_specs=[vec, vec, vec, vec, _const_spec((1, DA_V_DIM)), tok_spec, tok_spec, vt_spec],
        out_specs=tok_spec,
        out_shape=jax.ShapeDtypeStruct((b, s, DA_WIDTH), BF16),
        scratch_shapes=[scores, scores, scores, stats, stats, stats, queries, queries, stats,
                        pltpu.VMEM((DA_V_ROWS, 2 * blk), F32)],
        compiler_params=_params("parallel", "parallel"),
        name="diff_attention",
    )(lq1, lk1, lq2, lk2, gout, qa, ka, va_t)


def _retention_kernel(dec_ref, qdec_ref, kdec_ref, sdec_ref, smask_ref, gain_ref,
                      q_ref, k_ref, v_ref, g_ref, o_ref, state_ref):
    @pl.when(pl.program_id(1) == 0)
    def _():
        state_ref[...] = jnp.zeros_like(state_ref)

    q = q_ref[0]
    k = k_ref[0]
    v = v_ref[0]
    kb = k.astype(BF16)
    lane = lax.broadcasted_iota(jnp.int32, q.shape, 1)
    state = state_ref[...]
    o_cross = _dot((q * qdec_ref[...]).astype(BF16), state.astype(BF16))
    for h in range(RET_HEADS):
        qh = jnp.where(lane // RET_QK_DIM == h, q, 0.0).astype(BF16)
        w = (_dot_nt(qh, kb) * dec_ref[h]).astype(BF16)
        vs = slice(h * RET_V_DIM, (h + 1) * RET_V_DIM)
        o = _dot(w, v[:, vs]) + o_cross[:, vs]
        o = _rms(o, gain_ref[...])
        g = g_ref[0, :, vs]
        o_ref[0, :, vs] = (o * (g * jax.nn.sigmoid(g))).astype(BF16)
    kv = _dot_tn((k * kdec_ref[...]).astype(BF16), v)
    state_ref[...] = sdec_ref[...] * state + kv * smask_ref[...]


def _retention_tables(blk):
    heads = jnp.arange(RET_HEADS, dtype=F32)
    log_gamma = jnp.log(1.0 - 2.0 ** (-5.0 - heads))
    idx = jnp.arange(blk)
    diff = (idx[:, None] - idx[None, :]).astype(F32)
    same_chunk = (idx[:, None] // CHUNK) == (idx[None, :] // CHUNK)
    visible = (idx[None, :] <= idx[:, None]) | same_chunk
    dec = jnp.where(visible[None], jnp.exp(log_gamma[:, None, None] * jnp.abs(diff)[None]), 0.0)
    per_lane = jnp.repeat(log_gamma, RET_QK_DIM)
    pos = jnp.arange(blk, dtype=F32)
    qdec = jnp.exp((pos[:, None] + 1.0) * per_lane[None, :])
    kdec = jnp.exp((blk - 1.0 - pos)[:, None] * per_lane[None, :])
    sdec = jnp.exp(blk * per_lane)[:, None]
    smask = (jnp.arange(RET_HEADS * RET_QK_DIM)[:, None] // RET_QK_DIM
             == jnp.arange(RET_WIDTH)[None, :] // RET_V_DIM).astype(F32)
    return dec, qdec, kdec, sdec, smask


def _retention(qr, kr, vr, gr, gain):
    b, s, _ = qr.shape
    blk = min(RET_BLOCK, s)
    dec, qdec, kdec, sdec, smask = _retention_tables(blk)
    tok = lambda width: pl.BlockSpec((1, blk, width), lambda bi, i: (bi, i, 0))
    return pl.pallas_call(
        _retention_kernel,
        grid=(b, s // blk),
        in_specs=[_const_spec(dec.shape), _const_spec(qdec.shape), _const_spec(kdec.shape),
                  _const_spec(sdec.shape), _const_spec(smask.shape), _const_spec((1, RET_V_DIM)),
                  tok(256), tok(256), tok(512), tok(512)],
        out_specs=tok(RET_WIDTH),
        out_shape=jax.ShapeDtypeStruct((b, s, RET_WIDTH), BF16),
        scratch_shapes=[pltpu.VMEM((RET_HEADS * RET_QK_DIM, RET_WIDTH), F32)],
        compiler_params=_params("parallel", "arbitrary"),
        name="retention",
    )(dec, qdec, kdec, sdec, smask, gain, qr, kr, vr, gr)


def _mlp_tail(x1, g_ref, wup_ref, wdn_ref, o_ref):
    h = _rms(x1, g_ref[...]).astype(BF16)
    acc = x1
    for c in range(MLP_HIDDEN // MLP_HIDDEN_TILE):
        cs = slice(c * MLP_HIDDEN_TILE, (c + 1) * MLP_HIDDEN_TILE)
        u = jnp.maximum(_dot(h, wup_ref[:, cs]), 0.0)
        acc = acc + _dot((u * u).astype(BF16), wdn_ref[cs, :])
    o_ref[...] = acc


def _out_mlp_kernel(x_ref, a_ref, b_ref, wout_ref, g_ref, wup_ref, wdn_ref, o_ref):
    ka = a_ref.shape[1]
    mixed = _dot(a_ref[...], wout_ref[:ka, :]) + _dot(b_ref[...], wout_ref[ka:, :])
    _mlp_tail(x_ref[...] + mixed, g_ref, wup_ref, wdn_ref, o_ref)


def _s5_out_mlp_kernel(x_ref, yt_ref, ut_ref, d_ref, wglut_ref, b_ref, wout_ref, g_ref, wup_ref, wdn_ref, o_ref):
    y = yt_ref[...] + d_ref[...] * ut_ref[...]
    z = jax.nn.gelu(y)
    a_t = (z * jax.nn.sigmoid(_dot(wglut_ref[...], z.astype(BF16)))).astype(BF16)
    mixed = _dot_tn(a_t, wout_ref[:S5_WIDTH, :]) + _dot(b_ref[...], wout_ref[S5_WIDTH:, :])
    _mlp_tail(x_ref[...] + mixed, g_ref, wup_ref, wdn_ref, o_ref)


def _mlp_specs():
    return [_const_spec((D_MODEL, D_MODEL)), _const_spec((1, D_MODEL)),
            _const_spec((D_MODEL, MLP_HIDDEN)), _const_spec((MLP_HIDDEN, D_MODEL))]


def _out_mlp(x2, a, bb, wout, g, wup, wdn):
    t = x2.shape[0]
    tm = TOKEN_TILE
    row = lambda width: pl.BlockSpec((tm, width), lambda i: (i, 0))
    return pl.pallas_call(
        _out_mlp_kernel,
        grid=(t // tm,),
        in_specs=[row(D_MODEL), row(a.shape[1]), row(bb.shape[1])] + _mlp_specs(),
        out_specs=row(D_MODEL),
        out_shape=jax.ShapeDtypeStruct((t, D_MODEL), F32),
        compiler_params=_params("parallel"),
        name="ab_out_mlp",
    )(x2, a, bb, wout, g, wup, wdn)


def _s5_out_mlp(x2, y_t, u_t, d_col, wglu_t, bb, wout, g, wup, wdn):
    t = x2.shape[0]
    tm = TOKEN_TILE
    row = lambda width: pl.BlockSpec((tm, width), lambda i: (i, 0))
    col = pl.BlockSpec((S5_WIDTH, tm), lambda i: (0, i))
    return pl.pallas_call(
        _s5_out_mlp_kernel,
        grid=(t // tm,),
        in_specs=[row(D_MODEL), col, col, _const_spec((S5_WIDTH, 1)),
                  _const_spec((S5_WIDTH, S5_WIDTH)), row(GLA_WIDTH)] + _mlp_specs(),
        out_specs=row(D_MODEL),
        out_shape=jax.ShapeDtypeStruct((t, D_MODEL), F32),
        compiler_params=_params("parallel"),
        name="cd_out_mlp",
    )(x2, y_t, u_t, d_col, wglu_t, bb, wout, g, wup, wdn)


def _cd_in_kernel(x_ref, g_ref, w_ref, wut_ref, wa_ref, ba_ref, ut_ref, q_ref, k_ref, v_ref, r_ref, la_ref):
    h = _rms(x_ref[...], g_ref[...]).astype(BF16)

    def proj(lo, hi):
        return _dot(h, w_ref[:, lo:hi])

    ua_t = _dot_nt(wut_ref[...], h)
    ut_ref[...] = ua_t[:S5_WIDTH]
    q_ref[...] = proj(256, 640) * (GLA_QK_DIM ** -0.5)
    k_ref[...] = proj(640, 1024)
    v_ref[...] = proj(1024, 1792).astype(BF16)
    r_ref[...] = proj(1792, 2560)
    pre = _dot_tn(ua_t[S5_WIDTH:].astype(BF16), wa_ref[...]) + ba_ref[...]
    log_sig = jnp.minimum(pre, 0.0) - jnp.log1p(jnp.exp(-jnp.abs(pre)))
    la_ref[...] = log_sig / GLA_TAU


def _cd_in(x2, g, w, wu_t, wa, ba):
    t = x2.shape[0]
    tm = TOKEN_TILE
    row = lambda width: pl.BlockSpec((tm, width), lambda i: (i, 0))
    out_shapes = (
        jax.ShapeDtypeStruct((S5_WIDTH, t), F32), jax.ShapeDtypeStruct((t, GLA_QK_WIDTH), F32),
        jax.ShapeDtypeStruct((t, GLA_QK_WIDTH), F32), jax.ShapeDtypeStruct((t, GLA_WIDTH), BF16),
        jax.ShapeDtypeStruct((t, GLA_WIDTH), F32), jax.ShapeDtypeStruct((t, GLA_QK_WIDTH), F32))
    return pl.pallas_call(
        _cd_in_kernel,
        grid=(t // tm,),
        in_specs=[row(D_MODEL), _const_spec((1, D_MODEL)), _const_spec((D_MODEL, CD_IN_MAIN)),
                  _const_spec((S5_WIDTH + GATE_ROWS, D_MODEL)),
                  _const_spec((GATE_ROWS, GLA_QK_WIDTH)), _const_spec((1, GLA_QK_WIDTH))],
        out_specs=(pl.BlockSpec((S5_WIDTH, tm), lambda i: (0, i)), row(GLA_QK_WIDTH), row(GLA_QK_WIDTH),
                   row(GLA_WIDTH), row(GLA_WIDTH), row(GLA_QK_WIDTH)),
        out_shape=out_shapes,
        compiler_params=_params("parallel"),
        name="cd_in_proj",
    )(x2, g, w, wu_t, wa, ba)


def _split3(x):
    hi = x.astype(BF16)
    r = x - hi.astype(F32)
    mid = r.astype(BF16)
    lo = (r - mid.astype(F32)).astype(BF16)
    return hi, mid, lo


def _gla_kernel(tri_ref, gain_ref, q_ref, k_ref, v_ref, r_ref, la_ref, o_ref, state_ref, *, blk, batch):
    @pl.when(pl.program_id(0) == 0)
    def _():
        state_ref[...] = jnp.zeros_like(state_ref)

    tri = tri_ref[...]
    hi, mid, lo = _split3(jnp.concatenate([la_ref[bi] for bi in range(batch)], axis=1))
    b_all = _dot(tri, hi) + _dot(tri, mid) + _dot(tri, lo)
    for bi in range(batch):
        _gla_block(bi, b_all[:, bi * GLA_QK_WIDTH:(bi + 1) * GLA_QK_WIDTH], gain_ref, q_ref, k_ref, v_ref, r_ref,
                   o_ref, state_ref, blk)


def _gla_block(bi, b, gain_ref, q_ref, k_ref, v_ref, r_ref, o_ref, state_ref, blk):
    nc = blk // CHUNK
    pairs = GLA_HEADS // 2
    q = q_ref[bi]
    k = k_ref[bi]
    b3 = b.reshape(nc, CHUNK, GLA_QK_WIDTH)
    b_last = b3[:, CHUNK - 1:CHUNK, :]
    e_pos = jnp.exp(b)
    e_neg = jnp.exp(-b)
    k_tail = jnp.exp(b_last - b3).reshape(blk, GLA_QK_WIDTH)
    chunk_decay = jnp.exp(b_last)
    qp = (q * e_pos).astype(BF16)
    qn = (q * e_neg).astype(BF16)
    kp = (k * e_pos).astype(BF16)
    kn = (k * e_neg).astype(BF16)
    kw = (k * k_tail).astype(BF16)

    row = lax.broadcasted_iota(jnp.int32, (blk, blk), 0)
    col = lax.broadcasted_iota(jnp.int32, (blk, blk), 1)
    same_chunk = row // CHUNK == col // CHUNK
    causal = row >= col
    lane = lax.broadcasted_iota(jnp.int32, (blk, 2 * GLA_QK_DIM), 1)
    smask = (lax.broadcasted_iota(jnp.int32, (2 * GLA_V_DIM, 2 * GLA_QK_DIM), 0) // GLA_V_DIM
             == lax.broadcasted_iota(jnp.int32, (2 * GLA_V_DIM, 2 * GLA_QK_DIM), 1) // GLA_QK_DIM)

    for p in range(pairs):
        ks = slice(p * 2 * GLA_QK_DIM, (p + 1) * 2 * GLA_QK_DIM)
        vs = slice(p * 2 * GLA_V_DIM, (p + 1) * 2 * GLA_V_DIM)
        v = v_ref[bi, :, vs]
        scores = []
        for half in range(2):
            sel = (lane // GLA_QK_DIM) == half
            fwd = _dot_nt(jnp.where(sel, qp[:, ks], 0).astype(BF16), kn[:, ks])
            bwd = _dot_nt(jnp.where(sel, qn[:, ks], 0).astype(BF16), kp[:, ks])
            scores.append(jnp.where(same_chunk, jnp.where(causal, fwd, bwd), 0.0).astype(BF16))
        vlane = lax.broadcasted_iota(jnp.int32, v.shape, 1) // GLA_V_DIM
        v_diag = jnp.concatenate([jnp.where(vlane == 0, v, 0), jnp.where(vlane == 1, v, 0)], axis=0)
        o_intra = _dot(jnp.concatenate(scores, axis=1), v_diag)

        state = state_ref[bi * pairs + p]
        cross = []
        for c in range(nc):
            rs = slice(c * CHUNK, (c + 1) * CHUNK)
            cross.append(_dot_nt(qp[rs, ks], state.astype(BF16)))
            kv_t = _dot_tn(v[rs, :], kw[rs, ks])
            state = state * chunk_decay[c][:, ks] + jnp.where(smask, kv_t, 0.0)
        state_ref[bi * pairs + p] = state
        o = o_intra + jnp.concatenate(cross, axis=0)
        for half in range(2):
            hs = slice(half * GLA_V_DIM, (half + 1) * GLA_V_DIM)
            os_ = slice(p * 2 * GLA_V_DIM + half * GLA_V_DIM, p * 2 * GLA_V_DIM + (half + 1) * GLA_V_DIM)
            g = r_ref[bi, :, os_]
            o_ref[bi, :, os_] = (_rms(o[:, hs], gain_ref[...]) * (g * jax.nn.sigmoid(g))).astype(BF16)


def _gla(qg, kg, vg, rg, la, gain):
    b, s, _ = qg.shape
    blk = min(GLA_BLOCK, s)
    idx = jnp.arange(blk)
    tri = (((idx[:, None] // CHUNK) == (idx[None, :] // CHUNK)) & (idx[None, :] <= idx[:, None])).astype(BF16)
    tok = lambda width: pl.BlockSpec((b, blk, width), lambda i: (0, i, 0))
    return pl.pallas_call(
        functools.partial(_gla_kernel, blk=blk, batch=b),
        grid=(s // blk,),
        in_specs=[_const_spec((blk, blk)), _const_spec((1, GLA_V_DIM)),
                  tok(GLA_QK_WIDTH), tok(GLA_QK_WIDTH), tok(GLA_WIDTH), tok(GLA_WIDTH), tok(GLA_QK_WIDTH)],
        out_specs=tok(GLA_WIDTH),
        out_shape=jax.ShapeDtypeStruct((b, s, GLA_WIDTH), BF16),
        scratch_shapes=[pltpu.VMEM((b * (GLA_HEADS // 2), 2 * GLA_V_DIM, 2 * GLA_QK_DIM), F32)],
        compiler_params=_params("arbitrary"),
        name="gla",
    )(tri, gain, qg, kg, vg, rg, la)


def _s5_kernel(u_ref, kern_ref, fre_ref, fim_ref, ere_ref, eim_ref, lre_ref, lim_ref, y_ref,
               toep_ref, vre_ref, vim_ref, hre_ref, him_ref, *, batch, chunks):
    length = S5_CHUNK
    src = lax.broadcasted_iota(jnp.int32, (length, length), 0)
    dst = lax.broadcasted_iota(jnp.int32, (length, length), 1)

    def build(ci, carry):
        for co in range(S5_GROUP):
            lags = kern_ref[0, pl.ds(ci * S5_GROUP + co, 1), :]
            blk = pltpu.roll(jnp.broadcast_to(lags, (length, length)), 0, 1, stride=1, stride_axis=0)
            blk = jnp.where(dst >= src, blk, 0.0)
            toep_ref[pl.ds(pl.multiple_of(ci * length, length), length), co * length:(co + 1) * length] = (
                blk.astype(BF16))
        return carry

    lax.fori_loop(0, S5_GROUP, build, 0)

    u = jnp.concatenate([u_ref[0, c] for c in range(S5_GROUP)], axis=1).astype(BF16)
    vre_ref[...] = _dot(u, fre_ref[0])
    vim_ref[...] = _dot(u, fim_ref[0])
    lre = lre_ref[0]
    lim = lim_ref[0]

    def step(n, carry):
        new = []
        for bi in range(batch):
            hr, hi = carry[2 * bi], carry[2 * bi + 1]
            r = bi * chunks + n
            hre_ref[pl.ds(r, 1), :] = hr
            him_ref[pl.ds(r, 1), :] = hi
            vr = vre_ref[pl.ds(r, 1), :]
            vi = vim_ref[pl.ds(r, 1), :]
            new += [lre * hr - lim * hi + vr, lre * hi + lim * hr + vi]
        return tuple(new)

    zero = jnp.zeros((1, LANES_V7X), F32)
    lax.fori_loop(0, chunks, step, (zero,) * (2 * batch))
    y = (_dot(u, toep_ref[...]) + _dot(hre_ref[...].astype(BF16), ere_ref[0])
         + _dot(him_ref[...].astype(BF16), eim_ref[0]))
    for c in range(S5_GROUP):
        y_ref[0, c] = y[:, c * length:(c + 1) * length]


def _s5_tables(a_re, a_im, log_step, b_re, b_im, c_re, c_im):
    hp = lax.Precision.HIGHEST
    g, p, c, length = S5_GROUPS, S5_STATE, S5_GROUP, S5_CHUNK
    a_re, a_im = a_re.astype(F32), a_im.astype(F32)
    delta = jnp.exp(log_step.astype(F32))[:, None]
    tau = jnp.arange(length + 1, dtype=F32)[None, :, None]
    mag = jnp.exp((a_re * delta)[:, None, :] * tau)
    ang = (a_im * delta)[:, None, :] * tau
    pw_re, pw_im = mag * jnp.cos(ang), mag * jnp.sin(ang)
    n_re, n_im = pw_re[:, 1, :] - 1.0, pw_im[:, 1, :]
    den = a_re * a_re + a_im * a_im
    q_re, q_im = (n_re * a_re + n_im * a_im) / den, (n_im * a_re - n_re * a_im) / den
    bb_re = q_re[:, :, None] * b_re - q_im[:, :, None] * b_im
    bb_im = q_re[:, :, None] * b_im + q_im[:, :, None] * b_re
    first_re, first_im = pw_re[:, :length, None, :], pw_im[:, :length, None, :]
    cp_re = c_re[:, None] * first_re - c_im[:, None] * first_im
    cp_im = c_re[:, None] * first_im + c_im[:, None] * first_re
    kern = (jnp.einsum('gtcp,gpd->gdct', cp_re, bb_re, precision=hp)
            - jnp.einsum('gtcp,gpd->gdct', cp_im, bb_im, precision=hp)).reshape(g, c * c, length)
    rev_re, rev_im = pw_re[:, length - 1::-1, :][:, None], pw_im[:, length - 1::-1, :][:, None]
    bt_re, bt_im = bb_re.transpose(0, 2, 1)[:, :, None, :], bb_im.transpose(0, 2, 1)[:, :, None, :]
    f_re = (rev_re * bt_re - rev_im * bt_im).reshape(g, c * length, p)
    f_im = (rev_re * bt_im + rev_im * bt_re).reshape(g, c * length, p)
    nx_re = pw_re[:, 1:, :].transpose(0, 2, 1)[:, :, None, :]
    nx_im = pw_im[:, 1:, :].transpose(0, 2, 1)[:, :, None, :]
    ct_re, ct_im = c_re.transpose(0, 2, 1)[:, :, :, None], c_im.transpose(0, 2, 1)[:, :, :, None]
    e_re = (ct_re * nx_re - ct_im * nx_im).reshape(g, p, c * length)
    e_im = (ct_re * nx_im + ct_im * nx_re).reshape(g, p, c * length)
    pad = lambda z: jnp.pad(z, ((0, 0), (0, 0), (0, LANES_V7X - p)))
    pad_rows = lambda z: jnp.pad(z, ((0, 0), (0, LANES_V7X - p), (0, 0)))
    return (kern, pad(f_re).astype(BF16), pad(f_im).astype(BF16), pad_rows(e_re).astype(BF16),
            pad_rows(-e_im).astype(BF16), pad(pw_re[:, length:, :]), pad(pw_im[:, length:, :]))


def _s5(u_t, tables, batch):
    t = u_t.shape[1]
    length = S5_CHUNK
    rows = t // length
    width = length * S5_GROUP
    kern, fre, fim, ere, eim, lre, lim = tables
    grp = lambda r, c: pl.BlockSpec((1, r, c), lambda gi: (gi, 0, 0))
    io_spec = pl.BlockSpec((1, S5_GROUP, rows, length), lambda gi: (gi, 0, 0, 0))
    y = pl.pallas_call(
        functools.partial(_s5_kernel, batch=batch, chunks=rows // batch),
        grid=(S5_GROUPS,),
        in_specs=[io_spec, grp(S5_GROUP * S5_GROUP, length), grp(width, LANES_V7X), grp(width, LANES_V7X),
                  grp(LANES_V7X, width), grp(LANES_V7X, width), grp(1, LANES_V7X), grp(1, LANES_V7X)],
        out_specs=io_spec,
        out_shape=jax.ShapeDtypeStruct((S5_GROUPS, S5_GROUP, rows, length), F32),
        scratch_shapes=[pltpu.VMEM((width, width), BF16)] + [pltpu.VMEM((rows, LANES_V7X), F32)] * 4,
        compiler_params=_params("parallel"),
        name="s5",
    )(u_t.reshape(S5_GROUPS, S5_GROUP, rows, length), kern, fre, fim, ere, eim, lre, lim)
    return y.reshape(S5_WIDTH, t)


def _rotary_tables(seq):
    half = RET_QK_DIM // 2
    inv_freq = 1.0 / (ROPE_BASE ** jnp.linspace(0.0, 1.0, half, dtype=F32))
    ang = jnp.arange(seq, dtype=F32)[:, None] * inv_freq[None, :]
    cos, sin = jnp.cos(ang), jnp.sin(ang)
    cos_t = jnp.tile(jnp.concatenate([cos, cos], axis=1), (1, RET_HEADS))
    sin_t = jnp.tile(jnp.concatenate([-sin, sin], axis=1), (1, RET_HEADS))
    return cos_t, sin_t


def _row(v):
    return v.reshape(1, -1).astype(F32)


def kernel(x, norm_mix_g, norm_mlp_g, w_up, w_down, ab_w_in, ab_w_out, da_q_norm, da_k_norm,
           da_lam_q1, da_lam_k1, da_lam_q2, da_lam_k2, da_out_norm, ret_out_norm, cd_w_in, cd_w_out,
           s5_a_re, s5_a_im, s5_log_step, s5_b_re, s5_b_im, s5_c_re, s5_c_im, s5_d, s5_w_glu,
           gla_w_a2, gla_b_a2, gla_out_norm):
    bsz, seq, _ = x.shape
    t = bsz * seq
    x2 = x.reshape(t, D_MODEL)

    cos_t, sin_t = _rotary_tables(seq)
    gsum = ((jnp.arange(512)[:, None] // DA_QK_DIM) == (jnp.arange(512)[None, :] // DA_QK_DIM))
    gsum = (gsum.astype(F32) / DA_QK_DIM).astype(BF16)
    gq = _row(jnp.tile(da_q_norm[0], 2 * DA_HEADS)) * (DA_QK_DIM ** -0.5 * math.log2(math.e))
    gk = _row(jnp.tile(da_k_norm[0], 2 * DA_HEADS))
    w_ab = ab_w_in[0].astype(BF16)
    qa, ka, va_t, qr, kr, vr, gr = _ab_in(x2, _row(norm_mix_g[0]), w_ab, w_ab[:, 1024:1536].T, gq, gk, gsum,
                                          cos_t, sin_t, seq)
    lam_init = 0.8 - 0.6 * math.exp(-0.3 * 0)
    seq3 = lambda a: a.reshape(bsz, seq, a.shape[-1])
    o_a = _diff_attn(seq3(qa), seq3(ka), va_t, _row(da_lam_q1[0]), _row(da_lam_k1[0]),
                     _row(da_lam_q2[0]), _row(da_lam_k2[0]), _row(da_out_norm[0]), lam_init)
    o_r = _retention(seq3(qr), seq3(kr), seq3(vr), seq3(gr), _row(ret_out_norm[0]))
    x2 = _out_mlp(x2, o_a.reshape(t, DA_WIDTH), o_r.reshape(t, RET_WIDTH), ab_w_out[0].astype(BF16),
                  _row(norm_mlp_g[0]), w_up[0].astype(BF16), w_down[0].astype(BF16))

    w_cd = cd_w_in[0].astype(BF16)
    gate_pad = GATE_ROWS - GLA_GATE_RANK
    wua_t = jnp.pad(jnp.concatenate([w_cd[:, :S5_WIDTH], w_cd[:, CD_IN_MAIN:]], axis=1).T, ((0, gate_pad), (0, 0)))
    wa = jnp.pad(gla_w_a2[0], ((0, gate_pad), (0, 0))).astype(BF16)
    u_t, qg, kg, vg, rg, la = _cd_in(x2, _row(norm_mix_g[1]), w_cd[:, :CD_IN_MAIN], wua_t, wa, _row(gla_b_a2[0]))
    o_d = _gla(seq3(qg), seq3(kg), seq3(vg), seq3(rg), seq3(la), _row(gla_out_norm[0]))
    tables = _s5_tables(s5_a_re[0], s5_a_im[0], s5_log_step[0], s5_b_re[0], s5_b_im[0], s5_c_re[0], s5_c_im[0])
    y_t = _s5(u_t, tables, bsz)
    x2 = _s5_out_mlp(x2, y_t, u_t, s5_d[0].reshape(S5_WIDTH, 1).astype(F32), s5_w_glu[0].T.astype(BF16),
                     o_d.reshape(t, GLA_WIDTH), cd_w_out[0].astype(BF16), _row(norm_mlp_g[1]),
                     w_up[1].astype(BF16), w_down[1].astype(BF16))
    return x2.reshape(bsz, seq, D_MODEL)
```

```python
import functools
import math

import jax
import jax.numpy as jnp
from jax import lax
from jax.experimental import pallas as pl
from jax.experimental.pallas import tpu as pltpu

F32 = jnp.float32
BF16 = jnp.bfloat16

D_MODEL = 1024
CHUNK = 64
RMS_EPS = 1e-6
ROPE_BASE = 10000.0
DA_HEADS = 4
DA_QK_DIM = 64
DA_V_DIM = 128
DA_WIDTH = DA_HEADS * DA_V_DIM
DA_V_ROWS = DA_V_DIM + 16
RET_HEADS = 4
RET_QK_DIM = 64
RET_V_DIM = 128
RET_WIDTH = RET_HEADS * RET_V_DIM
S5_WIDTH = 256
S5_GROUP = 16
S5_GROUPS = S5_WIDTH // S5_GROUP
S5_STATE = 64
GLA_HEADS = 6
GLA_QK_DIM = 64
GLA_V_DIM = 128
GLA_WIDTH = GLA_HEADS * GLA_V_DIM
GLA_QK_WIDTH = GLA_HEADS * GLA_QK_DIM
GLA_GATE_RANK = 16
GLA_TAU = 16.0
MLP_HIDDEN = 4 * D_MODEL
DA_QK_WIDTH = DA_HEADS * 2 * DA_QK_DIM
RET_QK_WIDTH = RET_HEADS * RET_QK_DIM


def _offsets(*sizes):
    bounds = [0]
    for size in sizes:
        bounds.append(bounds[-1] + size)
    return tuple(bounds)


AB_COLS = _offsets(DA_QK_WIDTH, DA_QK_WIDTH, DA_WIDTH, RET_QK_WIDTH, RET_QK_WIDTH, RET_WIDTH, RET_WIDTH)
AB_IN = AB_COLS[-1]
CD_COLS = _offsets(S5_WIDTH, GLA_QK_WIDTH, GLA_QK_WIDTH, GLA_WIDTH, GLA_WIDTH, GLA_GATE_RANK)
CD_IN = CD_COLS[-1]

LANES_V7X = 128
VMEM_LIMIT_BYTES_V7X = 56 * 1024 * 1024

CD_IN_MAIN = CD_COLS[-2]
GATE_ROWS = 32
NEG_BIG = -1e30

TOKEN_TILE = 512
ATTN_BLOCK = 512
ATTN_TILE = 256
RET_BLOCK = 512
GLA_BLOCK = 256
S5_CHUNK = LANES_V7X
MLP_HIDDEN_TILE = 1024


def _params(*semantics):
    return pltpu.CompilerParams(dimension_semantics=semantics, vmem_limit_bytes=VMEM_LIMIT_BYTES_V7X)


def _const_spec(shape):
    zeros = (0,) * len(shape)
    return pl.BlockSpec(shape, lambda *_: zeros, pipeline_mode=pl.Buffered(1))


def _rms(xf, gain):
    return xf * lax.rsqrt(jnp.mean(xf * xf, axis=-1, keepdims=True) + RMS_EPS) * gain


def _dot(a, b):
    return jnp.dot(a, b, preferred_element_type=F32)


def _dot_nt(a, b):
    return lax.dot_general(a, b, (((1,), (1,)), ((), ())), preferred_element_type=F32)


def _dot_tn(a, b):
    return lax.dot_general(a, b, (((0,), (0,)), ((), ())), preferred_element_type=F32)


def _swap_halves(x, group):
    n = x.shape[-1]
    half = group // 2
    lane = lax.broadcasted_iota(jnp.int32, x.shape, x.ndim - 1)
    from_right = pltpu.roll(x, n - half, axis=x.ndim - 1)
    from_left = pltpu.roll(x, half, axis=x.ndim - 1)
    return jnp.where((lane % group) < half, from_right, from_left)


def _ab_in_kernel(x_ref, g_ref, w_ref, wvt_ref, gq_ref, gk_ref, gsum_ref, cos_ref, sin_ref,
                  qa_ref, ka_ref, va_ref, qr_ref, kr_ref, vr_ref, gr_ref):
    h = _rms(x_ref[...], g_ref[...]).astype(BF16)

    def proj(part):
        return _dot(h, w_ref[:, AB_COLS[part]:AB_COLS[part + 1]])

    def qk_norm(y, gain):
        ms = _dot((y * y).astype(BF16), gsum_ref[...])
        return (y * lax.rsqrt(ms + RMS_EPS) * gain).astype(BF16)

    def rotary(y):
        return y * cos_ref[...] + _swap_halves(y, RET_QK_DIM) * sin_ref[...]

    qa_ref[...] = qk_norm(proj(0), gq_ref[...])
    ka_ref[...] = qk_norm(proj(1), gk_ref[...])
    vt = _dot_nt(wvt_ref[...], h).astype(BF16)
    pad_row = lax.broadcasted_iota(jnp.int32, (DA_V_ROWS - DA_V_DIM, vt.shape[1]), 0)
    ones_rows = jnp.where(pad_row == 0, 1.0, 0.0).astype(BF16)
    for hd in range(DA_HEADS):
        va_ref[0, hd * DA_V_ROWS:hd * DA_V_ROWS + DA_V_DIM, :] = vt[hd * DA_V_DIM:(hd + 1) * DA_V_DIM]
        va_ref[0, hd * DA_V_ROWS + DA_V_DIM:(hd + 1) * DA_V_ROWS, :] = ones_rows
    qr_ref[...] = rotary(proj(3))
    kr_ref[...] = rotary(proj(4)) * (RET_QK_DIM ** -0.5)
    vr_ref[...] = proj(5).astype(BF16)
    gr_ref[...] = proj(6)


def _ab_in(x2, g, w, wv_t, gq, gk, gsum, cos_t, sin_t, seq):
    t = x2.shape[0]
    tm = min(TOKEN_TILE, seq)
    per_seq = seq // tm
    row = lambda width: pl.BlockSpec((tm, width), lambda i: (i, 0))
    va_spec = pl.BlockSpec((1, DA_HEADS * DA_V_ROWS, tm), lambda i: (i // per_seq, 0, i % per_seq))
    rot_spec = pl.BlockSpec((tm, RET_QK_WIDTH), lambda i: (i % per_seq, 0))
    out_shapes = (
        jax.ShapeDtypeStruct((t, DA_QK_WIDTH), BF16), jax.ShapeDtypeStruct((t, DA_QK_WIDTH), BF16),
        jax.ShapeDtypeStruct((t // seq, DA_HEADS * DA_V_ROWS, seq), BF16),
        jax.ShapeDtypeStruct((t, RET_QK_WIDTH), F32), jax.ShapeDtypeStruct((t, RET_QK_WIDTH), F32),
        jax.ShapeDtypeStruct((t, RET_WIDTH), BF16), jax.ShapeDtypeStruct((t, RET_WIDTH), F32))
    return pl.pallas_call(
        _ab_in_kernel,
        grid=(t // tm,),
        in_specs=[row(D_MODEL), _const_spec((1, D_MODEL)), _const_spec((D_MODEL, AB_IN)),
                  _const_spec((DA_WIDTH, D_MODEL)),
                  _const_spec((1, DA_QK_WIDTH)), _const_spec((1, DA_QK_WIDTH)),
                  _const_spec((DA_QK_WIDTH, DA_QK_WIDTH)), rot_spec, rot_spec],
        out_specs=(row(DA_QK_WIDTH), row(DA_QK_WIDTH), va_spec, row(RET_QK_WIDTH), row(RET_QK_WIDTH),
                   row(RET_WIDTH), row(RET_WIDTH)),
        out_shape=out_shapes,
        compiler_params=_params("parallel"),
        name="ab_in_proj",
    )(x2, g, w, wv_t, gq, gk, gsum, cos_t, sin_t)


def _diff_attn_kernel(lq1_ref, lk1_ref, lq2_ref, lk2_ref, gout_ref, q_ref, k_ref, vt_ref, o_ref,
                      sa_ref, sb_ref, sc_ref, mxa_ref, mxb_ref, mxc_ref, qza_ref, qzb_ref, m_ref, acc_ref,
                      *, blk, nblk, lam_init):
    lam = (jnp.exp(jnp.sum(lq1_ref[...] * lk1_ref[...], axis=-1, keepdims=True))
           - jnp.exp(jnp.sum(lq2_ref[...] * lk2_ref[...], axis=-1, keepdims=True)) + lam_init)
    gain = gout_ref[...] * (1.0 - lam_init)
    buf_a, buf_b, buf_c = (sa_ref, mxa_ref), (sb_ref, mxb_ref), (sc_ref, mxc_ref)

    def load_queries(qb, qz_ref):
        start = pl.multiple_of(qb * blk, blk)
        qt = q_ref[0, pl.ds(start, blk), :].astype(F32).T
        dim = lax.broadcasted_iota(jnp.int32, qt.shape, 0)
        qz_ref[...] = jnp.concatenate([jnp.where(dim < DA_QK_DIM, qt, 0.0), jnp.where(dim >= DA_QK_DIM, qt, 0.0)],
                                      axis=1).astype(BF16)

    def score(qz_ref, t, buf):
        s_ref, mx_ref = buf
        start = pl.multiple_of(t * blk, blk)
        st = _dot(k_ref[0, pl.ds(start, blk), :], qz_ref[...])
        s_ref[...] = st
        mx_ref[...] = jnp.max(st, axis=0, keepdims=True)

    def absorb(t, buf, diagonal=False):
        s_ref, mx_ref = buf
        start = pl.multiple_of(t * blk, blk)
        tile = ATTN_TILE
        for qg in range(2 * blk // tile):
            cs = slice(qg * tile, (qg + 1) * tile)

            def scores(kh):
                st = s_ref[kh * tile:(kh + 1) * tile, cs]
                if diagonal:
                    key = lax.broadcasted_iota(jnp.int32, st.shape, 0) + kh * tile
                    qry = lax.broadcasted_iota(jnp.int32, st.shape, 1) + (qg * tile) % blk
                    st = jnp.where(key // CHUNK <= qry // CHUNK, st, NEG_BIG)
                return st

            if diagonal:
                mx = functools.reduce(jnp.maximum, [jnp.max(scores(kh), axis=0, keepdims=True)
                                                    for kh in range(blk // tile)])
            else:
                mx = mx_ref[:, cs]
            m = m_ref[:, cs]
            m_new = jnp.maximum(m, mx)
            acc = jnp.exp2(m - m_new) * acc_ref[:, cs]
            for kh in range(blk // tile):
                p = jnp.exp2(scores(kh) - m_new)
                acc = acc + _dot(vt_ref[0, :, pl.ds(start + kh * tile, tile)], p.astype(BF16))
            m_ref[:, cs] = m_new
            acc_ref[:, cs] = acc

    def repeat(first, count, two_stages):
        def twice(v, c):
            two_stages(first + 2 * v)
            two_stages(first + 2 * v + 1)
            return c

        lax.fori_loop(0, lax.shift_right_logical(count, 1), twice, 0)

        @pl.when((count & 1) == 1)
        def _():
            two_stages(first + count - 1)

    def reset():
        m_ref[...] = jnp.full(m_ref.shape, NEG_BIG, F32)
        acc_ref[...] = jnp.zeros(acc_ref.shape, F32)

    def finish(qb):
        o = acc_ref[:DA_V_DIM, :] / acc_ref[DA_V_DIM:DA_V_DIM + 1, :]
        o = o[:, :blk] - lam * o[:, blk:]
        o = o * lax.rsqrt(jnp.mean(o * o, axis=0, keepdims=True) + RMS_EPS)
        o_ref[0, pl.ds(pl.multiple_of(qb * blk, blk), blk), :] = (o.T * gain).astype(BF16)
        reset()

    reset()
    load_queries(0, qza_ref)
    load_queries(1, qzb_ref)
    score(qza_ref, 0, buf_a)
    score(qzb_ref, 0, buf_b)
    absorb(0, buf_a, diagonal=True)
    finish(0)
    score(qzb_ref, 1, buf_a)
    absorb(0, buf_b)
    load_queries(min(2, nblk - 1), qza_ref)
    score(qza_ref, 0, buf_c)
    absorb(1, buf_a, diagonal=True)
    finish(1)

    def pair(g, carry):
        a = 2 * g
        load_queries(a + 1, qzb_ref)
        score(qza_ref, 1, buf_b)
        absorb(0, buf_c)
        score(qza_ref, 2, buf_a)
        absorb(1, buf_b)

        def stages_a(u):
            score(qza_ref, 2 * u + 1, buf_b)
            absorb(2 * u, buf_a)
            score(qza_ref, 2 * u + 2, buf_a)
            absorb(2 * u + 1, buf_b)

        repeat(1, g - 1, stages_a)
        score(qzb_ref, 0, buf_b)
        absorb(a, buf_a, diagonal=True)
        finish(a)
        score(qzb_ref, 1, buf_a)
        absorb(0, buf_b)

        def stages_b(u):
            score(qzb_ref, 2 * u + 2, buf_b)
            absorb(2 * u + 1, buf_a)
            score(qzb_ref, 2 * u + 3, buf_a)
            absorb(2 * u + 2, buf_b)

        repeat(0, g, stages_b)
        load_queries(jnp.minimum(a + 2, nblk - 1), qza_ref)
        score(qza_ref, 0, buf_c)
        absorb(a + 1, buf_a, diagonal=True)
        finish(a + 1)
        return carry

    lax.fori_loop(1, nblk // 2, pair, 0)


def _diff_attn(qa, ka, va_t, lq1, lk1, lq2, lk2, gout, lam_init):
    b, s, _ = qa.shape
    blk = min(ATTN_BLOCK, s // 2)
    nblk = s // blk
    assert nblk % 2 == 0 and blk % ATTN_TILE == 0, (s, blk)
    vec = _const_spec((1, DA_QK_DIM))
    assert 2 * DA_QK_DIM == DA_V_DIM
    tok_spec = pl.BlockSpec((1, s, DA_V_DIM), lambda bi, h: (bi, 0, h))
    vt_spec = pl.BlockSpec((1, DA_V_ROWS, s), lambda bi, h: (bi, h, 0))
    scores = pltpu.VMEM((blk, 2 * blk), F32)
    stats = pltpu.VMEM((1, 2 * blk), F32)
    queries = pltpu.VMEM((2 * DA_QK_DIM, 2 * blk), BF16)
    return pl.pallas_call(
        functools.partial(_diff_attn_kernel, blk=blk, nblk=nblk, lam_init=lam_init),
        grid=(b, DA_HEADS),
        in_specs=[vec, vec, vec, vec, _const_spec((1, DA_V_DIM)), tok_spec, tok_spec, vt_spec],
        out_specs=tok_spec,
        out_shape=jax.ShapeDtypeStruct((b, s, DA_WIDTH), BF16),
        scratch_shapes=[scores, scores, scores, stats, stats, stats, queries, queries, stats,
                        pltpu.VMEM((DA_V_ROWS, 2 * blk), F32)],
        compiler_params=_params("parallel", "parallel"),
        name="diff_attention",
    )(lq1, lk1, lq2, lk2, gout, qa, ka, va_t)


def _retention_kernel(dec_ref, qdec_ref, kdec_ref, sdec_ref, smask_ref, gain_ref,
                      q_ref, k_ref, v_ref, g_ref, o_ref, state_ref):
    @pl.when(pl.program_id(1) == 0)
    def _():
        state_ref[...] = jnp.zeros_like(state_ref)

    q = q_ref[0]
    k = k_ref[0]
    v = v_ref[0]
    kb = k.astype(BF16)
    lane = lax.broadcasted_iota(jnp.int32, q.shape, 1)
    state = state_ref[...]
    o_cross = _dot((q * qdec_ref[...]).astype(BF16), state.astype(BF16))
    for h in range(RET_HEADS):
        qh = jnp.where(lane // RET_QK_DIM == h, q, 0.0).astype(BF16)
        w = (_dot_nt(qh, kb) * dec_ref[h]).astype(BF16)
        vs = slice(h * RET_V_DIM, (h + 1) * RET_V_DIM)
        o = _dot(w, v[:, vs]) + o_cross[:, vs]
        o = _rms(o, gain_ref[...])
        g = g_ref[0, :, vs]
        o_ref[0, :, vs] = (o * (g * jax.nn.sigmoid(g))).astype(BF16)
    kv = _dot_tn((k * kdec_ref[...]).astype(BF16), v)
    state_ref[...] = sdec_ref[...] * state + kv * smask_ref[...]


def _retention_tables(blk):
    heads = jnp.arange(RET_HEADS, dtype=F32)
    log_gamma = jnp.log(1.0 - 2.0 ** (-5.0 - heads))
    idx = jnp.arange(blk)
    diff = (idx[:, None] - idx[None, :]).astype(F32)
    same_chunk = (idx[:, None] // CHUNK) == (idx[None, :] // CHUNK)
    visible = (idx[None, :] <= idx[:, None]) | same_chunk
    dec = jnp.where(visible[None], jnp.exp(log_gamma[:, None, None] * jnp.abs(diff)[None]), 0.0)
    per_lane = jnp.repeat(log_gamma, RET_QK_DIM)
    pos = jnp.arange(blk, dtype=F32)
    qdec = jnp.exp((pos[:, None] + 1.0) * per_lane[None, :])
    kdec = jnp.exp((blk - 1.0 - pos)[:, None] * per_lane[None, :])
    sdec = jnp.exp(blk * per_lane)[:, None]
    smask = (jnp.arange(RET_HEADS * RET_QK_DIM)[:, None] // RET_QK_DIM
             == jnp.arange(RET_WIDTH)[None, :] // RET_V_DIM).astype(F32)
    return dec, qdec, kdec, sdec, smask


def _retention(qr, kr, vr, gr, gain):
    b, s, _ = qr.shape
    blk = min(RET_BLOCK, s)
    dec, qdec, kdec, sdec, smask = _retention_tables(blk)
    tok = lambda width: pl.BlockSpec((1, blk, width), lambda bi, i: (bi, i, 0))
    return pl.pallas_call(
        _retention_kernel,
        grid=(b, s // blk),
        in_specs=[_const_spec(dec.shape), _const_spec(qdec.shape), _const_spec(kdec.shape),
                  _const_spec(sdec.shape), _const_spec(smask.shape), _const_spec((1, RET_V_DIM)),
                  tok(RET_QK_WIDTH), tok(RET_QK_WIDTH), tok(RET_WIDTH), tok(RET_WIDTH)],
        out_specs=tok(RET_WIDTH),
        out_shape=jax.ShapeDtypeStruct((b, s, RET_WIDTH), BF16),
        scratch_shapes=[pltpu.VMEM((RET_HEADS * RET_QK_DIM, RET_WIDTH), F32)],
        compiler_params=_params("parallel", "arbitrary"),
        name="retention",
    )(dec, qdec, kdec, sdec, smask, gain, qr, kr, vr, gr)


def _mlp_tail(x1, g_ref, wup_ref, wdn_ref, o_ref):
    h = _rms(x1, g_ref[...]).astype(BF16)
    acc = x1
    for c in range(MLP_HIDDEN // MLP_HIDDEN_TILE):
        cs = slice(c * MLP_HIDDEN_TILE, (c + 1) * MLP_HIDDEN_TILE)
        u = jnp.maximum(_dot(h, wup_ref[:, cs]), 0.0)
        acc = acc + _dot((u * u).astype(BF16), wdn_ref[cs, :])
    o_ref[...] = acc


def _out_mlp_kernel(x_ref, a_ref, b_ref, wout_ref, g_ref, wup_ref, wdn_ref, o_ref):
    ka = a_ref.shape[1]
    mixed = _dot(a_ref[...], wout_ref[:ka, :]) + _dot(b_ref[...], wout_ref[ka:, :])
    _mlp_tail(x_ref[...] + mixed, g_ref, wup_ref, wdn_ref, o_ref)


def _s5_out_mlp_kernel(x_ref, yt_ref, ut_ref, d_ref, wglut_ref, b_ref, wout_ref, g_ref, wup_ref, wdn_ref, o_ref):
    y = yt_ref[...] + d_ref[...] * ut_ref[...]
    z = jax.nn.gelu(y)
    a_t = (z * jax.nn.sigmoid(_dot(wglut_ref[...], z.astype(BF16)))).astype(BF16)
    mixed = _dot_tn(a_t, wout_ref[:S5_WIDTH, :]) + _dot(b_ref[...], wout_ref[S5_WIDTH:, :])
    _mlp_tail(x_ref[...] + mixed, g_ref, wup_ref, wdn_ref, o_ref)


def _mlp_specs():
    return [_const_spec((D_MODEL, D_MODEL)), _const_spec((1, D_MODEL)),
            _const_spec((D_MODEL, MLP_HIDDEN)), _const_spec((MLP_HIDDEN, D_MODEL))]


def _out_mlp(x2, a, bb, wout, g, wup, wdn):
    t = x2.shape[0]
    tm = TOKEN_TILE
    row = lambda width: pl.BlockSpec((tm, width), lambda i: (i, 0))
    return pl.pallas_call(
        _out_mlp_kernel,
        grid=(t // tm,),
        in_specs=[row(D_MODEL), row(a.shape[1]), row(bb.shape[1])] + _mlp_specs(),
        out_specs=row(D_MODEL),
        out_shape=jax.ShapeDtypeStruct((t, D_MODEL), F32),
        compiler_params=_params("parallel"),
        name="ab_out_mlp",
    )(x2, a, bb, wout, g, wup, wdn)


def _s5_out_mlp(x2, y_t, u_t, d_col, wglu_t, bb, wout, g, wup, wdn):
    t = x2.shape[0]
    tm = TOKEN_TILE
    row = lambda width: pl.BlockSpec((tm, width), lambda i: (i, 0))
    col = pl.BlockSpec((S5_WIDTH, tm), lambda i: (0, i))
    return pl.pallas_call(
        _s5_out_mlp_kernel,
        grid=(t // tm,),
        in_specs=[row(D_MODEL), col, col, _const_spec((S5_WIDTH, 1)),
                  _const_spec((S5_WIDTH, S5_WIDTH)), row(GLA_WIDTH)] + _mlp_specs(),
        out_specs=row(D_MODEL),
        out_shape=jax.ShapeDtypeStruct((t, D_MODEL), F32),
        compiler_params=_params("parallel"),
        name="cd_out_mlp",
    )(x2, y_t, u_t, d_col, wglu_t, bb, wout, g, wup, wdn)


def _cd_in_kernel(x_ref, g_ref, w_ref, wut_ref, wa_ref, ba_ref, ut_ref, q_ref, k_ref, v_ref, r_ref, la_ref):
    h = _rms(x_ref[...], g_ref[...]).astype(BF16)

    def proj(part):
        return _dot(h, w_ref[:, CD_COLS[part]:CD_COLS[part + 1]])

    ua_t = _dot_nt(wut_ref[...], h)
    ut_ref[...] = ua_t[:S5_WIDTH]
    q_ref[...] = proj(1) * (GLA_QK_DIM ** -0.5)
    k_ref[...] = proj(2)
    v_ref[...] = proj(3).astype(BF16)
    r_ref[...] = proj(4)
    pre = _dot_tn(ua_t[S5_WIDTH:].astype(BF16), wa_ref[...]) + ba_ref[...]
    log_sig = jnp.minimum(pre, 0.0) - jnp.log1p(jnp.exp(-jnp.abs(pre)))
    la_ref[...] = log_sig / GLA_TAU


def _cd_in(x2, g, w, wu_t, wa, ba):
    t = x2.shape[0]
    tm = TOKEN_TILE
    row = lambda width: pl.BlockSpec((tm, width), lambda i: (i, 0))
    out_shapes = (
        jax.ShapeDtypeStruct((S5_WIDTH, t), F32), jax.ShapeDtypeStruct((t, GLA_QK_WIDTH), F32),
        jax.ShapeDtypeStruct((t, GLA_QK_WIDTH), F32), jax.ShapeDtypeStruct((t, GLA_WIDTH), BF16),
        jax.ShapeDtypeStruct((t, GLA_WIDTH), F32), jax.ShapeDtypeStruct((t, GLA_QK_WIDTH), F32))
    return pl.pallas_call(
        _cd_in_kernel,
        grid=(t // tm,),
        in_specs=[row(D_MODEL), _const_spec((1, D_MODEL)), _const_spec((D_MODEL, CD_IN_MAIN)),
                  _const_spec((S5_WIDTH + GATE_ROWS, D_MODEL)),
                  _const_spec((GATE_ROWS, GLA_QK_WIDTH)), _const_spec((1, GLA_QK_WIDTH))],
        out_specs=(pl.BlockSpec((S5_WIDTH, tm), lambda i: (0, i)), row(GLA_QK_WIDTH), row(GLA_QK_WIDTH),
                   row(GLA_WIDTH), row(GLA_WIDTH), row(GLA_QK_WIDTH)),
        out_shape=out_shapes,
        compiler_params=_params("parallel"),
        name="cd_in_proj",
    )(x2, g, w, wu_t, wa, ba)


def _split3(x):
    hi = x.astype(BF16)
    r = x - hi.astype(F32)
    mid = r.astype(BF16)
    lo = (r - mid.astype(F32)).astype(BF16)
    return hi, mid, lo


def _gla_kernel(tri_ref, gain_ref, q_ref, k_ref, v_ref, r_ref, la_ref, o_ref, state_ref, *, blk, batch):
    @pl.when(pl.program_id(0) == 0)
    def _():
        state_ref[...] = jnp.zeros_like(state_ref)

    tri = tri_ref[...]
    hi, mid, lo = _split3(jnp.concatenate([la_ref[bi] for bi in range(batch)], axis=1))
    b_all = _dot(tri, hi) + _dot(tri, mid) + _dot(tri, lo)
    for bi in range(batch):
        _gla_block(bi, b_all[:, bi * GLA_QK_WIDTH:(bi + 1) * GLA_QK_WIDTH], gain_ref, q_ref, k_ref, v_ref, r_ref,
                   o_ref, state_ref, blk)


def _gla_block(bi, b, gain_ref, q_ref, k_ref, v_ref, r_ref, o_ref, state_ref, blk):
    nc = blk // CHUNK
    pairs = GLA_HEADS // 2
    q = q_ref[bi]
    k = k_ref[bi]
    b3 = b.reshape(nc, CHUNK, GLA_QK_WIDTH)
    b_last = b3[:, CHUNK - 1:CHUNK, :]
    e_pos = jnp.exp(b)
    e_neg = jnp.exp(-b)
    k_tail = jnp.exp(b_last - b3).reshape(blk, GLA_QK_WIDTH)
    chunk_decay = jnp.exp(b_last)
    qp = (q * e_pos).astype(BF16)
    qn = (q * e_neg).astype(BF16)
    kp = (k * e_pos).astype(BF16)
    kn = (k * e_neg).astype(BF16)
    kw = (k * k_tail).astype(BF16)

    row = lax.broadcasted_iota(jnp.int32, (blk, blk), 0)
    col = lax.broadcasted_iota(jnp.int32, (blk, blk), 1)
    same_chunk = row // CHUNK == col // CHUNK
    causal = row >= col
    lane = lax.broadcasted_iota(jnp.int32, (blk, 2 * GLA_QK_DIM), 1)
    smask = (lax.broadcasted_iota(jnp.int32, (2 * GLA_V_DIM, 2 * GLA_QK_DIM), 0) // GLA_V_DIM
             == lax.broadcasted_iota(jnp.int32, (2 * GLA_V_DIM, 2 * GLA_QK_DIM), 1) // GLA_QK_DIM)

    for p in range(pairs):
        ks = slice(p * 2 * GLA_QK_DIM, (p + 1) * 2 * GLA_QK_DIM)
        vs = slice(p * 2 * GLA_V_DIM, (p + 1) * 2 * GLA_V_DIM)
        v = v_ref[bi, :, vs]
        scores = []
        for half in range(2):
            sel = (lane // GLA_QK_DIM) == half
            fwd = _dot_nt(jnp.where(sel, qp[:, ks], 0).astype(BF16), kn[:, ks])
            bwd = _dot_nt(jnp.where(sel, qn[:, ks], 0).astype(BF16), kp[:, ks])
            scores.append(jnp.where(same_chunk, jnp.where(causal, fwd, bwd), 0.0).astype(BF16))
        vlane = lax.broadcasted_iota(jnp.int32, v.shape, 1) // GLA_V_DIM
        v_diag = jnp.concatenate([jnp.where(vlane == 0, v, 0), jnp.where(vlane == 1, v, 0)], axis=0)
        o_intra = _dot(jnp.concatenate(scores, axis=1), v_diag)

        state = state_ref[bi * pairs + p]
        cross = []
        for c in range(nc):
            rs = slice(c * CHUNK, (c + 1) * CHUNK)
            cross.append(_dot_nt(qp[rs, ks], state.astype(BF16)))
            kv_t = _dot_tn(v[rs, :], kw[rs, ks])
            state = state * chunk_decay[c][:, ks] + jnp.where(smask, kv_t, 0.0)
        state_ref[bi * pairs + p] = state
        o = o_intra + jnp.concatenate(cross, axis=0)
        for half in range(2):
            hs = slice(half * GLA_V_DIM, (half + 1) * GLA_V_DIM)
            os_ = slice(p * 2 * GLA_V_DIM + half * GLA_V_DIM, p * 2 * GLA_V_DIM + (half + 1) * GLA_V_DIM)
            g = r_ref[bi, :, os_]
            o_ref[bi, :, os_] = (_rms(o[:, hs], gain_ref[...]) * (g * jax.nn.sigmoid(g))).astype(BF16)


def _gla(qg, kg, vg, rg, la, gain):
    b, s, _ = qg.shape
    blk = min(GLA_BLOCK, s)
    idx = jnp.arange(blk)
    tri = (((idx[:, None] // CHUNK) == (idx[None, :] // CHUNK)) & (idx[None, :] <= idx[:, None])).astype(BF16)
    tok = lambda width: pl.BlockSpec((b, blk, width), lambda i: (0, i, 0))
    return pl.pallas_call(
        functools.partial(_gla_kernel, blk=blk, batch=b),
        grid=(s // blk,),
        in_specs=[_const_spec((blk, blk)), _const_spec((1, GLA_V_DIM)),
                  tok(GLA_QK_WIDTH), tok(GLA_QK_WIDTH), tok(GLA_WIDTH), tok(GLA_WIDTH), tok(GLA_QK_WIDTH)],
        out_specs=tok(GLA_WIDTH),
        out_shape=jax.ShapeDtypeStruct((b, s, GLA_WIDTH), BF16),
        scratch_shapes=[pltpu.VMEM((b * (GLA_HEADS // 2), 2 * GLA_V_DIM, 2 * GLA_QK_DIM), F32)],
        compiler_params=_params("arbitrary"),
        name="gla",
    )(tri, gain, qg, kg, vg, rg, la)


def _s5_kernel(u_ref, kern_ref, fre_ref, fim_ref, ere_ref, eim_ref, lre_ref, lim_ref, y_ref,
               toep_ref, vre_ref, vim_ref, hre_ref, him_ref, *, batch, chunks):
    length = S5_CHUNK
    src = lax.broadcasted_iota(jnp.int32, (length, length), 0)
    dst = lax.broadcasted_iota(jnp.int32, (length, length), 1)

    def build(ci, carry):
        for co in range(S5_GROUP):
            lags = kern_ref[0, pl.ds(ci * S5_GROUP + co, 1), :]
            blk = pltpu.roll(jnp.broadcast_to(lags, (length, length)), 0, 1, stride=1, stride_axis=0)
            blk = jnp.where(dst >= src, blk, 0.0)
            toep_ref[pl.ds(pl.multiple_of(ci * length, length), length), co * length:(co + 1) * length] = (
                blk.astype(BF16))
        return carry

    lax.fori_loop(0, S5_GROUP, build, 0)

    u = jnp.concatenate([u_ref[0, c] for c in range(S5_GROUP)], axis=1).astype(BF16)
    vre_ref[...] = _dot(u, fre_ref[0])
    vim_ref[...] = _dot(u, fim_ref[0])
    lre = lre_ref[0]
    lim = lim_ref[0]

    def step(n, carry):
        new = []
        for bi in range(batch):
            hr, hi = carry[2 * bi], carry[2 * bi + 1]
            r = bi * chunks + n
            hre_ref[pl.ds(r, 1), :] = hr
            him_ref[pl.ds(r, 1), :] = hi
            vr = vre_ref[pl.ds(r, 1), :]
            vi = vim_ref[pl.ds(r, 1), :]
            new += [lre * hr - lim * hi + vr, lre * hi + lim * hr + vi]
        return tuple(new)

    zero = jnp.zeros((1, LANES_V7X), F32)
    lax.fori_loop(0, chunks, step, (zero,) * (2 * batch))
    y = (_dot(u, toep_ref[...]) + _dot(hre_ref[...].astype(BF16), ere_ref[0])
         + _dot(him_ref[...].astype(BF16), eim_ref[0]))
    for c in range(S5_GROUP):
        y_ref[0, c] = y[:, c * length:(c + 1) * length]


def _s5_tables(a_re, a_im, log_step, b_re, b_im, c_re, c_im):
    hp = lax.Precision.HIGHEST
    g, p, c, length = S5_GROUPS, S5_STATE, S5_GROUP, S5_CHUNK
    a_re, a_im = a_re.astype(F32), a_im.astype(F32)
    delta = jnp.exp(log_step.astype(F32))[:, None]
    tau = jnp.arange(length + 1, dtype=F32)[None, :, None]
    mag = jnp.exp((a_re * delta)[:, None, :] * tau)
    ang = (a_im * delta)[:, None, :] * tau
    pw_re, pw_im = mag * jnp.cos(ang), mag * jnp.sin(ang)
    n_re, n_im = pw_re[:, 1, :] - 1.0, pw_im[:, 1, :]
    den = a_re * a_re + a_im * a_im
    q_re, q_im = (n_re * a_re + n_im * a_im) / den, (n_im * a_re - n_re * a_im) / den
    bb_re = q_re[:, :, None] * b_re - q_im[:, :, None] * b_im
    bb_im = q_re[:, :, None] * b_im + q_im[:, :, None] * b_re
    first_re, first_im = pw_re[:, :length, None, :], pw_im[:, :length, None, :]
    cp_re = c_re[:, None] * first_re - c_im[:, None] * first_im
    cp_im = c_re[:, None] * first_im + c_im[:, None] * first_re
    kern = (jnp.einsum('gtcp,gpd->gdct', cp_re, bb_re, precision=hp)
            - jnp.einsum('gtcp,gpd->gdct', cp_im, bb_im, precision=hp)).reshape(g, c * c, length)
    rev_re, rev_im = pw_re[:, length - 1::-1, :][:, None], pw_im[:, length - 1::-1, :][:, None]
    bt_re, bt_im = bb_re.transpose(0, 2, 1)[:, :, None, :], bb_im.transpose(0, 2, 1)[:, :, None, :]
    f_re = (rev_re * bt_re - rev_im * bt_im).reshape(g, c * length, p)
    f_im = (rev_re * bt_im + rev_im * bt_re).reshape(g, c * length, p)
    nx_re = pw_re[:, 1:, :].transpose(0, 2, 1)[:, :, None, :]
    nx_im = pw_im[:, 1:, :].transpose(0, 2, 1)[:, :, None, :]
    ct_re, ct_im = c_re.transpose(0, 2, 1)[:, :, :, None], c_im.transpose(0, 2, 1)[:, :, :, None]
    e_re = (ct_re * nx_re - ct_im * nx_im).reshape(g, p, c * length)
    e_im = (ct_re * nx_im + ct_im * nx_re).reshape(g, p, c * length)
    pad = lambda z: jnp.pad(z, ((0, 0), (0, 0), (0, LANES_V7X - p)))
    pad_rows = lambda z: jnp.pad(z, ((0, 0), (0, LANES_V7X - p), (0, 0)))
    return (kern, pad(f_re).astype(BF16), pad(f_im).astype(BF16), pad_rows(e_re).astype(BF16),
            pad_rows(-e_im).astype(BF16), pad(pw_re[:, length:, :]), pad(pw_im[:, length:, :]))


def _s5(u_t, tables, batch):
    t = u_t.shape[1]
    length = S5_CHUNK
    rows = t // length
    width = length * S5_GROUP
    kern, fre, fim, ere, eim, lre, lim = tables
    grp = lambda r, c: pl.BlockSpec((1, r, c), lambda gi: (gi, 0, 0))
    io_spec = pl.BlockSpec((1, S5_GROUP, rows, length), lambda gi: (gi, 0, 0, 0))
    y = pl.pallas_call(
        functools.partial(_s5_kernel, batch=batch, chunks=rows // batch),
        grid=(S5_GROUPS,),
        in_specs=[io_spec, grp(S5_GROUP * S5_GROUP, length), grp(width, LANES_V7X), grp(width, LANES_V7X),
                  grp(LANES_V7X, width), grp(LANES_V7X, width), grp(1, LANES_V7X), grp(1, LANES_V7X)],
        out_specs=io_spec,
        out_shape=jax.ShapeDtypeStruct((S5_GROUPS, S5_GROUP, rows, length), F32),
        scratch_shapes=[pltpu.VMEM((width, width), BF16)] + [pltpu.VMEM((rows, LANES_V7X), F32)] * 4,
        compiler_params=_params("parallel"),
        name="s5",
    )(u_t.reshape(S5_GROUPS, S5_GROUP, rows, length), kern, fre, fim, ere, eim, lre, lim)
    return y.reshape(S5_WIDTH, t)


def _rotary_tables(seq):
    half = RET_QK_DIM // 2
    inv_freq = 1.0 / (ROPE_BASE ** jnp.linspace(0.0, 1.0, half, dtype=F32))
    ang = jnp.arange(seq, dtype=F32)[:, None] * inv_freq[None, :]
    cos, sin = jnp.cos(ang), jnp.sin(ang)
    cos_t = jnp.tile(jnp.concatenate([cos, cos], axis=1), (1, RET_HEADS))
    sin_t = jnp.tile(jnp.concatenate([-sin, sin], axis=1), (1, RET_HEADS))
    return cos_t, sin_t


def _row(v):
    return v.reshape(1, -1).astype(F32)


def kernel(x, norm_mix_g, norm_mlp_g, w_up, w_down, ab_w_in, ab_w_out, da_q_norm, da_k_norm,
           da_lam_q1, da_lam_k1, da_lam_q2, da_lam_k2, da_out_norm, ret_out_norm, cd_w_in, cd_w_out,
           s5_a_re, s5_a_im, s5_log_step, s5_b_re, s5_b_im, s5_c_re, s5_c_im, s5_d, s5_w_glu,
           gla_w_a2, gla_b_a2, gla_out_norm):
    bsz, seq, _ = x.shape
    t = bsz * seq
    x2 = x.reshape(t, D_MODEL)

    cos_t, sin_t = _rotary_tables(seq)
    lane_group = jnp.arange(DA_QK_WIDTH) // DA_QK_DIM
    gsum = ((lane_group[:, None] == lane_group[None, :]).astype(F32) / DA_QK_DIM).astype(BF16)
    gq = _row(jnp.tile(da_q_norm[0], 2 * DA_HEADS)) * (DA_QK_DIM ** -0.5 * math.log2(math.e))
    gk = _row(jnp.tile(da_k_norm[0], 2 * DA_HEADS))
    w_ab = ab_w_in[0].astype(BF16)
    qa, ka, va_t, qr, kr, vr, gr = _ab_in(x2, _row(norm_mix_g[0]), w_ab, w_ab[:, AB_COLS[2]:AB_COLS[3]].T, gq, gk,
                                          gsum, cos_t, sin_t, seq)
    lam_init = 0.8 - 0.6 * math.exp(-0.3 * 0)
    seq3 = lambda a: a.reshape(bsz, seq, a.shape[-1])
    o_a = _diff_attn(seq3(qa), seq3(ka), va_t, _row(da_lam_q1[0]), _row(da_lam_k1[0]),
                     _row(da_lam_q2[0]), _row(da_lam_k2[0]), _row(da_out_norm[0]), lam_init)
    o_r = _retention(seq3(qr), seq3(kr), seq3(vr), seq3(gr), _row(ret_out_norm[0]))
    x2 = _out_mlp(x2, o_a.reshape(t, DA_WIDTH), o_r.reshape(t, RET_WIDTH), ab_w_out[0].astype(BF16),
                  _row(norm_mlp_g[0]), w_up[0].astype(BF16), w_down[0].astype(BF16))

    w_cd = cd_w_in[0].astype(BF16)
    gate_pad = GATE_ROWS - GLA_GATE_RANK
    wua_t = jnp.pad(jnp.concatenate([w_cd[:, :S5_WIDTH], w_cd[:, CD_IN_MAIN:]], axis=1).T, ((0, gate_pad), (0, 0)))
    wa = jnp.pad(gla_w_a2[0], ((0, gate_pad), (0, 0))).astype(BF16)
    u_t, qg, kg, vg, rg, la = _cd_in(x2, _row(norm_mix_g[1]), w_cd[:, :CD_IN_MAIN], wua_t, wa, _row(gla_b_a2[0]))
    o_d = _gla(seq3(qg), seq3(kg), seq3(vg), seq3(rg), seq3(la), _row(gla_out_norm[0]))
    tables = _s5_tables(s5_a_re[0], s5_a_im[0], s5_log_step[0], s5_b_re[0], s5_b_im[0], s5_c_re[0], s5_c_im[0])
    y_t = _s5(u_t, tables, bsz)
    x2 = _s5_out_mlp(x2, y_t, u_t, s5_d[0].reshape(S5_WIDTH, 1).astype(F32), s5_w_glu[0].T.astype(BF16),
                     o_d.reshape(t, GLA_WIDTH), cd_w_out[0].astype(BF16), _row(norm_mlp_g[1]),
                     w_up[1].astype(BF16), w_down[1].astype(BF16))
    return x2.reshape(bsz, seq, D_MODEL)
```

```python
import functools
import math

import jax
import jax.numpy as jnp
from jax import lax
from jax.experimental import pallas as pl
from jax.experimental.pallas import tpu as pltpu

F32 = jnp.float32
BF16 = jnp.bfloat16

D_MODEL = 1024
CHUNK = 64
RMS_EPS = 1e-6
ROPE_BASE = 10000.0
DA_HEADS = 4
DA_QK_DIM = 64
DA_V_DIM = 128
DA_WIDTH = DA_HEADS * DA_V_DIM
DA_V_ROWS = DA_V_DIM + 16
RET_HEADS = 4
RET_QK_DIM = 64
RET_V_DIM = 128
RET_WIDTH = RET_HEADS * RET_V_DIM
S5_WIDTH = 256
S5_GROUP = 16
S5_GROUPS = S5_WIDTH // S5_GROUP
S5_STATE = 64
GLA_HEADS = 6
GLA_QK_DIM = 64
GLA_V_DIM = 128
GLA_WIDTH = GLA_HEADS * GLA_V_DIM
GLA_QK_WIDTH = GLA_HEADS * GLA_QK_DIM
GLA_GATE_RANK = 16
GLA_TAU = 16.0
MLP_HIDDEN = 4 * D_MODEL
DA_QK_WIDTH = DA_HEADS * 2 * DA_QK_DIM
RET_QK_WIDTH = RET_HEADS * RET_QK_DIM


def _offsets(*sizes):
    bounds = [0]
    for size in sizes:
        bounds.append(bounds[-1] + size)
    return tuple(bounds)


AB_COLS = _offsets(DA_QK_WIDTH, DA_QK_WIDTH, DA_WIDTH, RET_QK_WIDTH, RET_QK_WIDTH, RET_WIDTH, RET_WIDTH)
AB_IN = AB_COLS[-1]
CD_COLS = _offsets(S5_WIDTH, GLA_QK_WIDTH, GLA_QK_WIDTH, GLA_WIDTH, GLA_WIDTH, GLA_GATE_RANK)
CD_IN = CD_COLS[-1]

LANES_V7X = 128
VMEM_LIMIT_BYTES_V7X = 56 * 1024 * 1024

CD_IN_MAIN = CD_COLS[-2]
GATE_ROWS = 32
NEG_BIG = -1e30

TOKEN_TILE = 512
ATTN_BLOCK = 512
ATTN_TILE = 256
RET_BLOCK = 512
GLA_BLOCK = 256
S5_CHUNK = LANES_V7X
MLP_HIDDEN_TILE = 1024


def _params(*semantics):
    return pltpu.CompilerParams(dimension_semantics=semantics, vmem_limit_bytes=VMEM_LIMIT_BYTES_V7X)


def _const_spec(shape):
    zeros = (0,) * len(shape)
    return pl.BlockSpec(shape, lambda *_: zeros, pipeline_mode=pl.Buffered(1))


def _rms(xf, gain):
    return xf * lax.rsqrt(jnp.mean(xf * xf, axis=-1, keepdims=True) + RMS_EPS) * gain


def _dot(a, b):
    return jnp.dot(a, b, preferred_element_type=F32)


def _dot_nt(a, b):
    return lax.dot_general(a, b, (((1,), (1,)), ((), ())), preferred_element_type=F32)


def _dot_tn(a, b):
    return lax.dot_general(a, b, (((0,), (0,)), ((), ())), preferred_element_type=F32)


def _swap_halves(x, group):
    n = x.shape[-1]
    half = group // 2
    lane = lax.broadcasted_iota(jnp.int32, x.shape, x.ndim - 1)
    from_right = pltpu.roll(x, n - half, axis=x.ndim - 1)
    from_left = pltpu.roll(x, half, axis=x.ndim - 1)
    return jnp.where((lane % group) < half, from_right, from_left)


def _ab_in_kernel(x_ref, g_ref, w_ref, wvt_ref, gq_ref, gk_ref, gsum_ref, cos_ref, sin_ref,
                  qa_ref, ka_ref, va_ref, qr_ref, kr_ref, vr_ref, gr_ref):
    h = _rms(x_ref[...], g_ref[...]).astype(BF16)

    def proj(part):
        return _dot(h, w_ref[:, AB_COLS[part]:AB_COLS[part + 1]])

    def qk_norm(y, gain):
        ms = _dot((y * y).astype(BF16), gsum_ref[...])
        return (y * lax.rsqrt(ms + RMS_EPS) * gain).astype(BF16)

    def rotary(y):
        return y * cos_ref[...] + _swap_halves(y, RET_QK_DIM) * sin_ref[...]

    qa_ref[...] = qk_norm(proj(0), gq_ref[...])
    ka_ref[...] = qk_norm(proj(1), gk_ref[...])
    vt = _dot_nt(wvt_ref[...], h).astype(BF16)
    pad_row = lax.broadcasted_iota(jnp.int32, (DA_V_ROWS - DA_V_DIM, vt.shape[1]), 0)
    ones_rows = jnp.where(pad_row == 0, 1.0, 0.0).astype(BF16)
    for hd in range(DA_HEADS):
        va_ref[0, hd * DA_V_ROWS:hd * DA_V_ROWS + DA_V_DIM, :] = vt[hd * DA_V_DIM:(hd + 1) * DA_V_DIM]
        va_ref[0, hd * DA_V_ROWS + DA_V_DIM:(hd + 1) * DA_V_ROWS, :] = ones_rows
    qr_ref[...] = rotary(proj(3))
    kr_ref[...] = rotary(proj(4)) * (RET_QK_DIM ** -0.5)
    vr_ref[...] = proj(5).astype(BF16)
    gr_ref[...] = proj(6)


def _ab_in(x2, g, w, wv_t, gq, gk, gsum, cos_t, sin_t, seq):
    t = x2.shape[0]
    tm = min(TOKEN_TILE, seq)
    per_seq = seq // tm
    row = lambda width: pl.BlockSpec((tm, width), lambda i: (i, 0))
    va_spec = pl.BlockSpec((1, DA_HEADS * DA_V_ROWS, tm), lambda i: (i // per_seq, 0, i % per_seq))
    rot_spec = pl.BlockSpec((tm, RET_QK_WIDTH), lambda i: (i % per_seq, 0))
    out_shapes = (
        jax.ShapeDtypeStruct((t, DA_QK_WIDTH), BF16), jax.ShapeDtypeStruct((t, DA_QK_WIDTH), BF16),
        jax.ShapeDtypeStruct((t // seq, DA_HEADS * DA_V_ROWS, seq), BF16),
        jax.ShapeDtypeStruct((t, RET_QK_WIDTH), F32), jax.ShapeDtypeStruct((t, RET_QK_WIDTH), F32),
        jax.ShapeDtypeStruct((t, RET_WIDTH), BF16), jax.ShapeDtypeStruct((t, RET_WIDTH), F32))
    return pl.pallas_call(
        _ab_in_kernel,
        grid=(t // tm,),
        in_specs=[row(D_MODEL), _const_spec((1, D_MODEL)), _const_spec((D_MODEL, AB_IN)),
                  _const_spec((DA_WIDTH, D_MODEL)),
                  _const_spec((1, DA_QK_WIDTH)), _const_spec((1, DA_QK_WIDTH)),
                  _const_spec((DA_QK_WIDTH, DA_QK_WIDTH)), rot_spec, rot_spec],
        out_specs=(row(DA_QK_WIDTH), row(DA_QK_WIDTH), va_spec, row(RET_QK_WIDTH), row(RET_QK_WIDTH),
                   row(RET_WIDTH), row(RET_WIDTH)),
        out_shape=out_shapes,
        compiler_params=_params("parallel"),
        name="ab_in_proj",
    )(x2, g, w, wv_t, gq, gk, gsum, cos_t, sin_t)


def _diff_attn_kernel(lq1_ref, lk1_ref, lq2_ref, lk2_ref, gout_ref, q_ref, k_ref, vt_ref, o_ref,
                      sa_ref, sb_ref, sc_ref, mxa_ref, mxb_ref, mxc_ref, qza_ref, qzb_ref, m_ref, acc_ref,
                      *, blk, nblk, lam_init):
    lam = (jnp.exp(jnp.sum(lq1_ref[...] * lk1_ref[...], axis=-1, keepdims=True))
           - jnp.exp(jnp.sum(lq2_ref[...] * lk2_ref[...], axis=-1, keepdims=True)) + lam_init)
    gain = gout_ref[...] * (1.0 - lam_init)
    buf_a, buf_b, buf_c = (sa_ref, mxa_ref), (sb_ref, mxb_ref), (sc_ref, mxc_ref)

    def load_queries(qb, qz_ref):
        start = pl.multiple_of(qb * blk, blk)
        qt = q_ref[0, pl.ds(start, blk), :].astype(F32).T
        dim = lax.broadcasted_iota(jnp.int32, qt.shape, 0)
        qz_ref[...] = jnp.concatenate([jnp.where(dim < DA_QK_DIM, qt, 0.0), jnp.where(dim >= DA_QK_DIM, qt, 0.0)],
                                      axis=1).astype(BF16)

    def score(qz_ref, t, buf):
        s_ref, mx_ref = buf
        start = pl.multiple_of(t * blk, blk)
        st = _dot(k_ref[0, pl.ds(start, blk), :], qz_ref[...])
        s_ref[...] = st
        mx_ref[...] = jnp.max(st, axis=0, keepdims=True)

    def absorb(t, buf, diagonal=False):
        s_ref, mx_ref = buf
        start = pl.multiple_of(t * blk, blk)
        tile = ATTN_TILE
        for qg in range(2 * blk // tile):
            cs = slice(qg * tile, (qg + 1) * tile)

            def scores(kh):
                st = s_ref[kh * tile:(kh + 1) * tile, cs]
                if diagonal:
                    key = lax.broadcasted_iota(jnp.int32, st.shape, 0) + kh * tile
                    qry = lax.broadcasted_iota(jnp.int32, st.shape, 1) + (qg * tile) % blk
                    st = jnp.where(key // CHUNK <= qry // CHUNK, st, NEG_BIG)
                return st

            if diagonal:
                mx = functools.reduce(jnp.maximum, [jnp.max(scores(kh), axis=0, keepdims=True)
                                                    for kh in range(blk // tile)])
            else:
                mx = mx_ref[:, cs]
            m = m_ref[:, cs]
            m_new = jnp.maximum(m, mx)
            acc = jnp.exp2(m - m_new) * acc_ref[:, cs]
            for kh in range(blk // tile):
                p = jnp.exp2(scores(kh) - m_new)
                acc = acc + _dot(vt_ref[0, :, pl.ds(start + kh * tile, tile)], p.astype(BF16))
            m_ref[:, cs] = m_new
            acc_ref[:, cs] = acc

    def repeat(first, count, two_stages):
        def four_times(v, c):
            for k in range(4):
                two_stages(first + 4 * v + k)
            return c

        lax.fori_loop(0, lax.shift_right_logical(count, 2), four_times, 0)
        done = first + (count & ~3)

        @pl.when((count & 2) == 2)
        def _():
            two_stages(done)
            two_stages(done + 1)

        @pl.when((count & 1) == 1)
        def _():
            two_stages(done + (count & 2))

    def reset():
        m_ref[...] = jnp.full(m_ref.shape, NEG_BIG, F32)
        acc_ref[...] = jnp.zeros(acc_ref.shape, F32)

    def finish(qb):
        o = acc_ref[:DA_V_DIM, :] / acc_ref[DA_V_DIM:DA_V_DIM + 1, :]
        o = o[:, :blk] - lam * o[:, blk:]
        o = o * lax.rsqrt(jnp.mean(o * o, axis=0, keepdims=True) + RMS_EPS)
        o_ref[0, pl.ds(pl.multiple_of(qb * blk, blk), blk), :] = (o.T * gain).astype(BF16)
        reset()

    reset()
    load_queries(0, qza_ref)
    load_queries(1, qzb_ref)
    score(qza_ref, 0, buf_a)
    score(qzb_ref, 0, buf_b)
    absorb(0, buf_a, diagonal=True)
    finish(0)
    score(qzb_ref, 1, buf_a)
    absorb(0, buf_b)
    load_queries(min(2, nblk - 1), qza_ref)
    score(qza_ref, 0, buf_c)
    absorb(1, buf_a, diagonal=True)
    finish(1)

    def pair(g, carry):
        a = 2 * g
        load_queries(a + 1, qzb_ref)
        score(qza_ref, 1, buf_b)
        absorb(0, buf_c)
        score(qza_ref, 2, buf_a)
        absorb(1, buf_b)

        def stages_a(u):
            score(qza_ref, 2 * u + 1, buf_b)
            absorb(2 * u, buf_a)
            score(qza_ref, 2 * u + 2, buf_a)
            absorb(2 * u + 1, buf_b)

        repeat(1, g - 1, stages_a)
        score(qzb_ref, 0, buf_b)
        absorb(a, buf_a, diagonal=True)
        finish(a)
        score(qzb_ref, 1, buf_a)
        absorb(0, buf_b)

        def stages_b(u):
            score(qzb_ref, 2 * u + 2, buf_b)
            absorb(2 * u + 1, buf_a)
            score(qzb_ref, 2 * u + 3, buf_a)
            absorb(2 * u + 2, buf_b)

        repeat(0, g, stages_b)
        load_queries(jnp.minimum(a + 2, nblk - 1), qza_ref)
        score(qza_ref, 0, buf_c)
        absorb(a + 1, buf_a, diagonal=True)
        finish(a + 1)
        return carry

    lax.fori_loop(1, nblk // 2, pair, 0)


def _diff_attn(qa, ka, va_t, lq1, lk1, lq2, lk2, gout, lam_init):
    b, s, _ = qa.shape
    blk = min(ATTN_BLOCK, s // 2)
    nblk = s // blk
    assert nblk % 2 == 0 and blk % ATTN_TILE == 0, (s, blk)
    vec = _const_spec((1, DA_QK_DIM))
    assert 2 * DA_QK_DIM == DA_V_DIM
    tok_spec = pl.BlockSpec((1, s, DA_V_DIM), lambda bi, h: (bi, 0, h))
    vt_spec = pl.BlockSpec((1, DA_V_ROWS, s), lambda bi, h: (bi, h, 0))
    scores = pltpu.VMEM((blk, 2 * blk), F32)
    stats = pltpu.VMEM((1, 2 * blk), F32)
    queries = pltpu.VMEM((2 * DA_QK_DIM, 2 * blk), BF16)
    return pl.pallas_call(
        functools.partial(_diff_attn_kernel, blk=blk, nblk=nblk, lam_init=lam_init),
        grid=(b, DA_HEADS),
        in_specs=[vec, vec, vec, vec, _const_spec((1, DA_V_DIM)), tok_spec, tok_spec, vt_spec],
        out_specs=tok_spec,
        out_shape=jax.ShapeDtypeStruct((b, s, DA_WIDTH), BF16),
        scratch_shapes=[scores, scores, scores, stats, stats, stats, queries, queries, stats,
                        pltpu.VMEM((DA_V_ROWS, 2 * blk), F32)],
        compiler_params=_params("parallel", "parallel"),
        name="diff_attention",
    )(lq1, lk1, lq2, lk2, gout, qa, ka, va_t)


def _retention_kernel(dec_ref, qdec_ref, kdec_ref, sdec_ref, smask_ref, gain_ref,
                      q_ref, k_ref, v_ref, g_ref, o_ref, state_ref):
    @pl.when(pl.program_id(1) == 0)
    def _():
        state_ref[...] = jnp.zeros_like(state_ref)

    q = q_ref[0]
    k = k_ref[0]
    v = v_ref[0]
    kb = k.astype(BF16)
    lane = lax.broadcasted_iota(jnp.int32, q.shape, 1)
    state = state_ref[...]
    o_cross = _dot((q * qdec_ref[...]).astype(BF16), state.astype(BF16))
    for h in range(RET_HEADS):
        qh = jnp.where(lane // RET_QK_DIM == h, q, 0.0).astype(BF16)
        w = (_dot_nt(qh, kb) * dec_ref[h]).astype(BF16)
        vs = slice(h * RET_V_DIM, (h + 1) * RET_V_DIM)
        o = _dot(w, v[:, vs]) + o_cross[:, vs]
        o = _rms(o, gain_ref[...])
        g = g_ref[0, :, vs]
        o_ref[0, :, vs] = (o * (g * jax.nn.sigmoid(g))).astype(BF16)
    kv = _dot_tn((k * kdec_ref[...]).astype(BF16), v)
    state_ref[...] = sdec_ref[...] * state + kv * smask_ref[...]


def _retention_tables(blk):
    heads = jnp.arange(RET_HEADS, dtype=F32)
    log_gamma = jnp.log(1.0 - 2.0 ** (-5.0 - heads))
    idx = jnp.arange(blk)
    diff = (idx[:, None] - idx[None, :]).astype(F32)
    same_chunk = (idx[:, None] // CHUNK) == (idx[None, :] // CHUNK)
    visible = (idx[None, :] <= idx[:, None]) | same_chunk
    dec = jnp.where(visible[None], jnp.exp(log_gamma[:, None, None] * jnp.abs(diff)[None]), 0.0)
    per_lane = jnp.repeat(log_gamma, RET_QK_DIM)
    pos = jnp.arange(blk, dtype=F32)
    qdec = jnp.exp((pos[:, None] + 1.0) * per_lane[None, :])
    kdec = jnp.exp((blk - 1.0 - pos)[:, None] * per_lane[None, :])
    sdec = jnp.exp(blk * per_lane)[:, None]
    smask = (jnp.arange(RET_HEADS * RET_QK_DIM)[:, None] // RET_QK_DIM
             == jnp.arange(RET_WIDTH)[None, :] // RET_V_DIM).astype(F32)
    return dec, qdec, kdec, sdec, smask


def _retention(qr, kr, vr, gr, gain):
    b, s, _ = qr.shape
    blk = min(RET_BLOCK, s)
    dec, qdec, kdec, sdec, smask = _retention_tables(blk)
    tok = lambda width: pl.BlockSpec((1, blk, width), lambda bi, i: (bi, i, 0))
    return pl.pallas_call(
        _retention_kernel,
        grid=(b, s // blk),
        in_specs=[_const_spec(dec.shape), _const_spec(qdec.shape), _const_spec(kdec.shape),
                  _const_spec(sdec.shape), _const_spec(smask.shape), _const_spec((1, RET_V_DIM)),
                  tok(RET_QK_WIDTH), tok(RET_QK_WIDTH), tok(RET_WIDTH), tok(RET_WIDTH)],
        out_specs=tok(RET_WIDTH),
        out_shape=jax.ShapeDtypeStruct((b, s, RET_WIDTH), BF16),
        scratch_shapes=[pltpu.VMEM((RET_HEADS * RET_QK_DIM, RET_WIDTH), F32)],
        compiler_params=_params("parallel", "arbitrary"),
        name="retention",
    )(dec, qdec, kdec, sdec, smask, gain, qr, kr, vr, gr)


def _mlp_tail(x1, g_ref, wup_ref, wdn_ref, o_ref):
    h = _rms(x1, g_ref[...]).astype(BF16)
    acc = x1
    for c in range(MLP_HIDDEN // MLP_HIDDEN_TILE):
        cs = slice(c * MLP_HIDDEN_TILE, (c + 1) * MLP_HIDDEN_TILE)
        u = jnp.maximum(_dot(h, wup_ref[:, cs]), 0.0)
        acc = acc + _dot((u * u).astype(BF16), wdn_ref[cs, :])
    o_ref[...] = acc


def _out_mlp_kernel(x_ref, a_ref, b_ref, wout_ref, g_ref, wup_ref, wdn_ref, o_ref):
    ka = a_ref.shape[1]
    mixed = _dot(a_ref[...], wout_ref[:ka, :]) + _dot(b_ref[...], wout_ref[ka:, :])
    _mlp_tail(x_ref[...] + mixed, g_ref, wup_ref, wdn_ref, o_ref)


def _s5_out_mlp_kernel(x_ref, yt_ref, ut_ref, d_ref, wglut_ref, b_ref, wout_ref, g_ref, wup_ref, wdn_ref, o_ref):
    y = yt_ref[...] + d_ref[...] * ut_ref[...]
    z = jax.nn.gelu(y)
    a_t = (z * jax.nn.sigmoid(_dot(wglut_ref[...], z.astype(BF16)))).astype(BF16)
    mixed = _dot_tn(a_t, wout_ref[:S5_WIDTH, :]) + _dot(b_ref[...], wout_ref[S5_WIDTH:, :])
    _mlp_tail(x_ref[...] + mixed, g_ref, wup_ref, wdn_ref, o_ref)


def _mlp_specs():
    return [_const_spec((D_MODEL, D_MODEL)), _const_spec((1, D_MODEL)),
            _const_spec((D_MODEL, MLP_HIDDEN)), _const_spec((MLP_HIDDEN, D_MODEL))]


def _out_mlp(x2, a, bb, wout, g, wup, wdn):
    t = x2.shape[0]
    tm = TOKEN_TILE
    row = lambda width: pl.BlockSpec((tm, width), lambda i: (i, 0))
    return pl.pallas_call(
        _out_mlp_kernel,
        grid=(t // tm,),
        in_specs=[row(D_MODEL), row(a.shape[1]), row(bb.shape[1])] + _mlp_specs(),
        out_specs=row(D_MODEL),
        out_shape=jax.ShapeDtypeStruct((t, D_MODEL), F32),
        compiler_params=_params("parallel"),
        name="ab_out_mlp",
    )(x2, a, bb, wout, g, wup, wdn)


def _s5_out_mlp(x2, y_t, u_t, d_col, wglu_t, bb, wout, g, wup, wdn):
    t = x2.shape[0]
    tm = TOKEN_TILE
    row = lambda width: pl.BlockSpec((tm, width), lambda i: (i, 0))
    col = pl.BlockSpec((S5_WIDTH, tm), lambda i: (0, i))
    return pl.pallas_call(
        _s5_out_mlp_kernel,
        grid=(t // tm,),
        in_specs=[row(D_MODEL), col, col, _const_spec((S5_WIDTH, 1)),
                  _const_spec((S5_WIDTH, S5_WIDTH)), row(GLA_WIDTH)] + _mlp_specs(),
        out_specs=row(D_MODEL),
        out_shape=jax.ShapeDtypeStruct((t, D_MODEL), F32),
        compiler_params=_params("parallel"),
        name="cd_out_mlp",
    )(x2, y_t, u_t, d_col, wglu_t, bb, wout, g, wup, wdn)


def _cd_in_kernel(x_ref, g_ref, w_ref, wut_ref, wa_ref, ba_ref, ut_ref, q_ref, k_ref, v_ref, r_ref, la_ref):
    h = _rms(x_ref[...], g_ref[...]).astype(BF16)

    def proj(part):
        return _dot(h, w_ref[:, CD_COLS[part]:CD_COLS[part + 1]])

    ua_t = _dot_nt(wut_ref[...], h)
    ut_ref[...] = ua_t[:S5_WIDTH]
    q_ref[...] = proj(1) * (GLA_QK_DIM ** -0.5)
    k_ref[...] = proj(2)
    v_ref[...] = proj(3).astype(BF16)
    r_ref[...] = proj(4)
    pre = _dot_tn(ua_t[S5_WIDTH:].astype(BF16), wa_ref[...]) + ba_ref[...]
    log_sig = jnp.minimum(pre, 0.0) - jnp.log1p(jnp.exp(-jnp.abs(pre)))
    la_ref[...] = log_sig / GLA_TAU


def _cd_in(x2, g, w, wu_t, wa, ba):
    t = x2.shape[0]
    tm = TOKEN_TILE
    row = lambda width: pl.BlockSpec((tm, width), lambda i: (i, 0))
    out_shapes = (
        jax.ShapeDtypeStruct((S5_WIDTH, t), F32), jax.ShapeDtypeStruct((t, GLA_QK_WIDTH), F32),
        jax.ShapeDtypeStruct((t, GLA_QK_WIDTH), F32), jax.ShapeDtypeStruct((t, GLA_WIDTH), BF16),
        jax.ShapeDtypeStruct((t, GLA_WIDTH), F32), jax.ShapeDtypeStruct((t, GLA_QK_WIDTH), F32))
    return pl.pallas_call(
        _cd_in_kernel,
        grid=(t // tm,),
        in_specs=[row(D_MODEL), _const_spec((1, D_MODEL)), _const_spec((D_MODEL, CD_IN_MAIN)),
                  _const_spec((S5_WIDTH + GATE_ROWS, D_MODEL)),
                  _const_spec((GATE_ROWS, GLA_QK_WIDTH)), _const_spec((1, GLA_QK_WIDTH))],
        out_specs=(pl.BlockSpec((S5_WIDTH, tm), lambda i: (0, i)), row(GLA_QK_WIDTH), row(GLA_QK_WIDTH),
                   row(GLA_WIDTH), row(GLA_WIDTH), row(GLA_QK_WIDTH)),
        out_shape=out_shapes,
        compiler_params=_params("parallel"),
        name="cd_in_proj",
    )(x2, g, w, wu_t, wa, ba)


def _split3(x):
    hi = x.astype(BF16)
    r = x - hi.astype(F32)
    mid = r.astype(BF16)
    lo = (r - mid.astype(F32)).astype(BF16)
    return hi, mid, lo


def _gla_kernel(tri_ref, gain_ref, q_ref, k_ref, v_ref, r_ref, la_ref, o_ref, state_ref, *, blk, batch):
    @pl.when(pl.program_id(0) == 0)
    def _():
        state_ref[...] = jnp.zeros_like(state_ref)

    tri = tri_ref[...]
    hi, mid, lo = _split3(jnp.concatenate([la_ref[bi] for bi in range(batch)], axis=1))
    b_all = _dot(tri, hi) + _dot(tri, mid) + _dot(tri, lo)
    for bi in range(batch):
        _gla_block(bi, b_all[:, bi * GLA_QK_WIDTH:(bi + 1) * GLA_QK_WIDTH], gain_ref, q_ref, k_ref, v_ref, r_ref,
                   o_ref, state_ref, blk)


def _gla_block(bi, b, gain_ref, q_ref, k_ref, v_ref, r_ref, o_ref, state_ref, blk):
    nc = blk // CHUNK
    pairs = GLA_HEADS // 2
    q = q_ref[bi]
    k = k_ref[bi]
    b3 = b.reshape(nc, CHUNK, GLA_QK_WIDTH)
    b_last = b3[:, CHUNK - 1:CHUNK, :]
    e_pos = jnp.exp(b)
    e_neg = jnp.exp(-b)
    k_tail = jnp.exp(b_last - b3).reshape(blk, GLA_QK_WIDTH)
    chunk_decay = jnp.exp(b_last)
    qp = (q * e_pos).astype(BF16)
    qn = (q * e_neg).astype(BF16)
    kp = (k * e_pos).astype(BF16)
    kn = (k * e_neg).astype(BF16)
    kw = (k * k_tail).astype(BF16)

    row = lax.broadcasted_iota(jnp.int32, (blk, blk), 0)
    col = lax.broadcasted_iota(jnp.int32, (blk, blk), 1)
    same_chunk = row // CHUNK == col // CHUNK
    causal = row >= col
    lane = lax.broadcasted_iota(jnp.int32, (blk, 2 * GLA_QK_DIM), 1)
    smask = (lax.broadcasted_iota(jnp.int32, (2 * GLA_V_DIM, 2 * GLA_QK_DIM), 0) // GLA_V_DIM
             == lax.broadcasted_iota(jnp.int32, (2 * GLA_V_DIM, 2 * GLA_QK_DIM), 1) // GLA_QK_DIM)

    for p in range(pairs):
        ks = slice(p * 2 * GLA_QK_DIM, (p + 1) * 2 * GLA_QK_DIM)
        vs = slice(p * 2 * GLA_V_DIM, (p + 1) * 2 * GLA_V_DIM)
        v = v_ref[bi, :, vs]
        scores = []
        for half in range(2):
            sel = (lane // GLA_QK_DIM) == half
            fwd = _dot_nt(jnp.where(sel, qp[:, ks], 0).astype(BF16), kn[:, ks])
            bwd = _dot_nt(jnp.where(sel, qn[:, ks], 0).astype(BF16), kp[:, ks])
            scores.append(jnp.where(same_chunk, jnp.where(causal, fwd, bwd), 0.0).astype(BF16))
        vlane = lax.broadcasted_iota(jnp.int32, v.shape, 1) // GLA_V_DIM
        v_diag = jnp.concatenate([jnp.where(vlane == 0, v, 0), jnp.where(vlane == 1, v, 0)], axis=0)
        o_intra = _dot(jnp.concatenate(scores, axis=1), v_diag)

        state = state_ref[bi * pairs + p]
        cross = []
        for c in range(nc):
            rs = slice(c * CHUNK, (c + 1) * CHUNK)
            cross.append(_dot_nt(qp[rs, ks], state.astype(BF16)))
            kv_t = _dot_tn(v[rs, :], kw[rs, ks])
            state = state * chunk_decay[c][:, ks] + jnp.where(smask, kv_t, 0.0)
        state_ref[bi * pairs + p] = state
        o = o_intra + jnp.concatenate(cross, axis=0)
        for half in range(2):
            hs = slice(half * GLA_V_DIM, (half + 1) * GLA_V_DIM)
            os_ = slice(p * 2 * GLA_V_DIM + half * GLA_V_DIM, p * 2 * GLA_V_DIM + (half + 1) * GLA_V_DIM)
            g = r_ref[bi, :, os_]
            o_ref[bi, :, os_] = (_rms(o[:, hs], gain_ref[...]) * (g * jax.nn.sigmoid(g))).astype(BF16)


def _gla(qg, kg, vg, rg, la, gain):
    b, s, _ = qg.shape
    blk = min(GLA_BLOCK, s)
    idx = jnp.arange(blk)
    tri = (((idx[:, None] // CHUNK) == (idx[None, :] // CHUNK)) & (idx[None, :] <= idx[:, None])).astype(BF16)
    tok = lambda width: pl.BlockSpec((b, blk, width), lambda i: (0, i, 0))
    return pl.pallas_call(
        functools.partial(_gla_kernel, blk=blk, batch=b),
        grid=(s // blk,),
        in_specs=[_const_spec((blk, blk)), _const_spec((1, GLA_V_DIM)),
                  tok(GLA_QK_WIDTH), tok(GLA_QK_WIDTH), tok(GLA_WIDTH), tok(GLA_WIDTH), tok(GLA_QK_WIDTH)],
        out_specs=tok(GLA_WIDTH),
        out_shape=jax.ShapeDtypeStruct((b, s, GLA_WIDTH), BF16),
        scratch_shapes=[pltpu.VMEM((b * (GLA_HEADS // 2), 2 * GLA_V_DIM, 2 * GLA_QK_DIM), F32)],
        compiler_params=_params("arbitrary"),
        name="gla",
    )(tri, gain, qg, kg, vg, rg, la)


def _s5_kernel(u_ref, kern_ref, fre_ref, fim_ref, ere_ref, eim_ref, lre_ref, lim_ref, y_ref,
               toep_ref, vre_ref, vim_ref, hre_ref, him_ref, *, batch, chunks):
    length = S5_CHUNK
    src = lax.broadcasted_iota(jnp.int32, (length, length), 0)
    dst = lax.broadcasted_iota(jnp.int32, (length, length), 1)

    def build(ci, carry):
        for co in range(S5_GROUP):
            lags = kern_ref[0, pl.ds(ci * S5_GROUP + co, 1), :]
            blk = pltpu.roll(jnp.broadcast_to(lags, (length, length)), 0, 1, stride=1, stride_axis=0)
            blk = jnp.where(dst >= src, blk, 0.0)
            toep_ref[pl.ds(pl.multiple_of(ci * length, length), length), co * length:(co + 1) * length] = (
                blk.astype(BF16))
        return carry

    lax.fori_loop(0, S5_GROUP, build, 0)

    u = jnp.concatenate([u_ref[0, c] for c in range(S5_GROUP)], axis=1).astype(BF16)
    vre_ref[...] = _dot(u, fre_ref[0])
    vim_ref[...] = _dot(u, fim_ref[0])
    lre = lre_ref[0]
    lim = lim_ref[0]

    def step(n, carry):
        new = []
        for bi in range(batch):
            hr, hi = carry[2 * bi], carry[2 * bi + 1]
            r = bi * chunks + n
            hre_ref[pl.ds(r, 1), :] = hr
            him_ref[pl.ds(r, 1), :] = hi
            vr = vre_ref[pl.ds(r, 1), :]
            vi = vim_ref[pl.ds(r, 1), :]
            new += [lre * hr - lim * hi + vr, lre * hi + lim * hr + vi]
        return tuple(new)

    zero = jnp.zeros((1, LANES_V7X), F32)
    lax.fori_loop(0, chunks, step, (zero,) * (2 * batch))
    y = (_dot(u, toep_ref[...]) + _dot(hre_ref[...].astype(BF16), ere_ref[0])
         + _dot(him_ref[...].astype(BF16), eim_ref[0]))
    for c in range(S5_GROUP):
        y_ref[0, c] = y[:, c * length:(c + 1) * length]


def _s5_tables(a_re, a_im, log_step, b_re, b_im, c_re, c_im):
    hp = lax.Precision.HIGHEST
    g, p, c, length = S5_GROUPS, S5_STATE, S5_GROUP, S5_CHUNK
    a_re, a_im = a_re.astype(F32), a_im.astype(F32)
    delta = jnp.exp(log_step.astype(F32))[:, None]
    tau = jnp.arange(length + 1, dtype=F32)[None, :, None]
    mag = jnp.exp((a_re * delta)[:, None, :] * tau)
    ang = (a_im * delta)[:, None, :] * tau
    pw_re, pw_im = mag * jnp.cos(ang), mag * jnp.sin(ang)
    n_re, n_im = pw_re[:, 1, :] - 1.0, pw_im[:, 1, :]
    den = a_re * a_re + a_im * a_im
    q_re, q_im = (n_re * a_re + n_im * a_im) / den, (n_im * a_re - n_re * a_im) / den
    bb_re = q_re[:, :, None] * b_re - q_im[:, :, None] * b_im
    bb_im = q_re[:, :, None] * b_im + q_im[:, :, None] * b_re
    first_re, first_im = pw_re[:, :length, None, :], pw_im[:, :length, None, :]
    cp_re = c_re[:, None] * first_re - c_im[:, None] * first_im
    cp_im = c_re[:, None] * first_im + c_im[:, None] * first_re
    kern = (jnp.einsum('gtcp,gpd->gdct', cp_re, bb_re, precision=hp)
            - jnp.einsum('gtcp,gpd->gdct', cp_im, bb_im, precision=hp)).reshape(g, c * c, length)
    rev_re, rev_im = pw_re[:, length - 1::-1, :][:, None], pw_im[:, length - 1::-1, :][:, None]
    bt_re, bt_im = bb_re.transpose(0, 2, 1)[:, :, None, :], bb_im.transpose(0, 2, 1)[:, :, None, :]
    f_re = (rev_re * bt_re - rev_im * bt_im).reshape(g, c * length, p)
    f_im = (rev_re * bt_im + rev_im * bt_re).reshape(g, c * length, p)
    nx_re = pw_re[:, 1:, :].transpose(0, 2, 1)[:, :, None, :]
    nx_im = pw_im[:, 1:, :].transpose(0, 2, 1)[:, :, None, :]
    ct_re, ct_im = c_re.transpose(0, 2, 1)[:, :, :, None], c_im.transpose(0, 2, 1)[:, :, :, None]
    e_re = (ct_re * nx_re - ct_im * nx_im).reshape(g, p, c * length)
    e_im = (ct_re * nx_im + ct_im * nx_re).reshape(g, p, c * length)
    pad = lambda z: jnp.pad(z, ((0, 0), (0, 0), (0, LANES_V7X - p)))
    pad_rows = lambda z: jnp.pad(z, ((0, 0), (0, LANES_V7X - p), (0, 0)))
    return (kern, pad(f_re).astype(BF16), pad(f_im).astype(BF16), pad_rows(e_re).astype(BF16),
            pad_rows(-e_im).astype(BF16), pad(pw_re[:, length:, :]), pad(pw_im[:, length:, :]))


def _s5(u_t, tables, batch):
    t = u_t.shape[1]
    length = S5_CHUNK
    rows = t // length
    width = length * S5_GROUP
    kern, fre, fim, ere, eim, lre, lim = tables
    grp = lambda r, c: pl.BlockSpec((1, r, c), lambda gi: (gi, 0, 0))
    io_spec = pl.BlockSpec((1, S5_GROUP, rows, length), lambda gi: (gi, 0, 0, 0))
    y = pl.pallas_call(
        functools.partial(_s5_kernel, batch=batch, chunks=rows // batch),
        grid=(S5_GROUPS,),
        in_specs=[io_spec, grp(S5_GROUP * S5_GROUP, length), grp(width, LANES_V7X), grp(width, LANES_V7X),
                  grp(LANES_V7X, width), grp(LANES_V7X, width), grp(1, LANES_V7X), grp(1, LANES_V7X)],
        out_specs=io_spec,
        out_shape=jax.ShapeDtypeStruct((S5_GROUPS, S5_GROUP, rows, length), F32),
        scratch_shapes=[pltpu.VMEM((width, width), BF16)] + [pltpu.VMEM((rows, LANES_V7X), F32)] * 4,
        compiler_params=_params("parallel"),
        name="s5",
    )(u_t.reshape(S5_GROUPS, S5_GROUP, rows, length), kern, fre, fim, ere, eim, lre, lim)
    return y.reshape(S5_WIDTH, t)


def _rotary_tables(seq):
    half = RET_QK_DIM // 2
    inv_freq = 1.0 / (ROPE_BASE ** jnp.linspace(0.0, 1.0, half, dtype=F32))
    ang = jnp.arange(seq, dtype=F32)[:, None] * inv_freq[None, :]
    cos, sin = jnp.cos(ang), jnp.sin(ang)
    cos_t = jnp.tile(jnp.concatenate([cos, cos], axis=1), (1, RET_HEADS))
    sin_t = jnp.tile(jnp.concatenate([-sin, sin], axis=1), (1, RET_HEADS))
    return cos_t, sin_t


def _row(v):
    return v.reshape(1, -1).astype(F32)


def kernel(x, norm_mix_g, norm_mlp_g, w_up, w_down, ab_w_in, ab_w_out, da_q_norm, da_k_norm,
           da_lam_q1, da_lam_k1, da_lam_q2, da_lam_k2, da_out_norm, ret_out_norm, cd_w_in, cd_w_out,
           s5_a_re, s5_a_im, s5_log_step, s5_b_re, s5_b_im, s5_c_re, s5_c_im, s5_d, s5_w_glu,
           gla_w_a2, gla_b_a2, gla_out_norm):
    bsz, seq, _ = x.shape
    t = bsz * seq
    x2 = x.reshape(t, D_MODEL)

    cos_t, sin_t = _rotary_tables(seq)
    lane_group = jnp.arange(DA_QK_WIDTH) // DA_QK_DIM
    gsum = ((lane_group[:, None] == lane_group[None, :]).astype(F32) / DA_QK_DIM).astype(BF16)
    gq = _row(jnp.tile(da_q_norm[0], 2 * DA_HEADS)) * (DA_QK_DIM ** -0.5 * math.log2(math.e))
    gk = _row(jnp.tile(da_k_norm[0], 2 * DA_HEADS))
    w_ab = ab_w_in[0].astype(BF16)
    qa, ka, va_t, qr, kr, vr, gr = _ab_in(x2, _row(norm_mix_g[0]), w_ab, w_ab[:, AB_COLS[2]:AB_COLS[3]].T, gq, gk,
                                          gsum, cos_t, sin_t, seq)
    lam_init = 0.8 - 0.6 * math.exp(-0.3 * 0)
    seq3 = lambda a: a.reshape(bsz, seq, a.shape[-1])
    o_a = _diff_attn(seq3(qa), seq3(ka), va_t, _row(da_lam_q1[0]), _row(da_lam_k1[0]),
                     _row(da_lam_q2[0]), _row(da_lam_k2[0]), _row(da_out_norm[0]), lam_init)
    o_r = _retention(seq3(qr), seq3(kr), seq3(vr), seq3(gr), _row(ret_out_norm[0]))
    x2 = _out_mlp(x2, o_a.reshape(t, DA_WIDTH), o_r.reshape(t, RET_WIDTH), ab_w_out[0].astype(BF16),
                  _row(norm_mlp_g[0]), w_up[0].astype(BF16), w_down[0].astype(BF16))

    w_cd = cd_w_in[0].astype(BF16)
    gate_pad = GATE_ROWS - GLA_GATE_RANK
    wua_t = jnp.pad(jnp.concatenate([w_cd[:, :S5_WIDTH], w_cd[:, CD_IN_MAIN:]], axis=1).T, ((0, gate_pad), (0, 0)))
    wa = jnp.pad(gla_w_a2[0], ((0, gate_pad), (0, 0))).astype(BF16)
    u_t, qg, kg, vg, rg, la = _cd_in(x2, _row(norm_mix_g[1]), w_cd[:, :CD_IN_MAIN], wua_t, wa, _row(gla_b_a2[0]))
    o_d = _gla(seq3(qg), seq3(kg), seq3(vg), seq3(rg), seq3(la), _row(gla_out_norm[0]))
    tables = _s5_tables(s5_a_re[0], s5_a_im[0], s5_log_step[0], s5_b_re[0], s5_b_im[0], s5_c_re[0], s5_c_im[0])
    y_t = _s5(u_t, tables, bsz)
    x2 = _s5_out_mlp(x2, y_t, u_t, s5_d[0].reshape(S5_WIDTH, 1).astype(F32), s5_w_glu[0].T.astype(BF16),
                     o_d.reshape(t, GLA_WIDTH), cd_w_out[0].astype(BF16), _row(norm_mlp_g[1]),
                     w_up[1].astype(BF16), w_down[1].astype(BF16))
    return x2.reshape(bsz, seq, D_MODEL)
```

```python
import functools
import math

import jax
import jax.numpy as jnp
from jax import lax
from jax.experimental import pallas as pl
from jax.experimental.pallas import tpu as pltpu

F32 = jnp.float32
BF16 = jnp.bfloat16

D_MODEL = 1024
CHUNK = 64
RMS_EPS = 1e-6
ROPE_BASE = 10000.0
DA_HEADS = 4
DA_QK_DIM = 64
DA_V_DIM = 128
DA_WIDTH = DA_HEADS * DA_V_DIM
DA_V_ROWS = DA_V_DIM + 16
RET_HEADS = 4
RET_QK_DIM = 64
RET_V_DIM = 128
RET_WIDTH = RET_HEADS * RET_V_DIM
S5_WIDTH = 256
S5_GROUP = 16
S5_GROUPS = S5_WIDTH // S5_GROUP
S5_STATE = 64
GLA_HEADS = 6
GLA_QK_DIM = 64
GLA_V_DIM = 128
GLA_WIDTH = GLA_HEADS * GLA_V_DIM
GLA_QK_WIDTH = GLA_HEADS * GLA_QK_DIM
GLA_GATE_RANK = 16
GLA_TAU = 16.0
MLP_HIDDEN = 4 * D_MODEL
DA_QK_WIDTH = DA_HEADS * 2 * DA_QK_DIM
RET_QK_WIDTH = RET_HEADS * RET_QK_DIM


def _offsets(*sizes):
    bounds = [0]
    for size in sizes:
        bounds.append(bounds[-1] + size)
    return tuple(bounds)


AB_COLS = _offsets(DA_QK_WIDTH, DA_QK_WIDTH, DA_WIDTH, RET_QK_WIDTH, RET_QK_WIDTH, RET_WIDTH, RET_WIDTH)
AB_IN = AB_COLS[-1]
CD_COLS = _offsets(S5_WIDTH, GLA_QK_WIDTH, GLA_QK_WIDTH, GLA_WIDTH, GLA_WIDTH, GLA_GATE_RANK)
CD_IN = CD_COLS[-1]

LANES_V7X = 128
VMEM_LIMIT_BYTES_V7X = 56 * 1024 * 1024

CD_IN_MAIN = CD_COLS[-2]
GATE_ROWS = 32
NEG_BIG = -1e30

TOKEN_TILE = 512
ATTN_BLOCK = 512
ATTN_TILE = 256
RET_BLOCK = 512
GLA_BLOCK = 256
S5_CHUNK = LANES_V7X
MLP_HIDDEN_TILE = 1024


def _params(*semantics):
    return pltpu.CompilerParams(dimension_semantics=semantics, vmem_limit_bytes=VMEM_LIMIT_BYTES_V7X)


def _const_spec(shape):
    zeros = (0,) * len(shape)
    return pl.BlockSpec(shape, lambda *_: zeros, pipeline_mode=pl.Buffered(1))


def _rms(xf, gain):
    return xf * lax.rsqrt(jnp.mean(xf * xf, axis=-1, keepdims=True) + RMS_EPS) * gain


def _dot(a, b):
    return jnp.dot(a, b, preferred_element_type=F32)


def _dot_nt(a, b):
    return lax.dot_general(a, b, (((1,), (1,)), ((), ())), preferred_element_type=F32)


def _dot_tn(a, b):
    return lax.dot_general(a, b, (((0,), (0,)), ((), ())), preferred_element_type=F32)


def _swap_halves(x, group):
    n = x.shape[-1]
    half = group // 2
    lane = lax.broadcasted_iota(jnp.int32, x.shape, x.ndim - 1)
    from_right = pltpu.roll(x, n - half, axis=x.ndim - 1)
    from_left = pltpu.roll(x, half, axis=x.ndim - 1)
    return jnp.where((lane % group) < half, from_right, from_left)


def _ab_in_kernel(x_ref, g_ref, w_ref, wvt_ref, gq_ref, gk_ref, gsum_ref, cos_ref, sin_ref,
                  qa_ref, ka_ref, va_ref, qr_ref, kr_ref, vr_ref, gr_ref):
    h = _rms(x_ref[...], g_ref[...]).astype(BF16)

    def proj(part):
        return _dot(h, w_ref[:, AB_COLS[part]:AB_COLS[part + 1]])

    def qk_norm(y, gain):
        ms = _dot((y * y).astype(BF16), gsum_ref[...])
        return (y * lax.rsqrt(ms + RMS_EPS) * gain).astype(BF16)

    def rotary(y):
        return y * cos_ref[...] + _swap_halves(y, RET_QK_DIM) * sin_ref[...]

    qa_ref[...] = qk_norm(proj(0), gq_ref[...])
    ka_ref[...] = qk_norm(proj(1), gk_ref[...])
    vt = _dot_nt(wvt_ref[...], h).astype(BF16)
    pad_row = lax.broadcasted_iota(jnp.int32, (DA_V_ROWS - DA_V_DIM, vt.shape[1]), 0)
    ones_rows = jnp.where(pad_row == 0, 1.0, 0.0).astype(BF16)
    for hd in range(DA_HEADS):
        va_ref[0, hd * DA_V_ROWS:hd * DA_V_ROWS + DA_V_DIM, :] = vt[hd * DA_V_DIM:(hd + 1) * DA_V_DIM]
        va_ref[0, hd * DA_V_ROWS + DA_V_DIM:(hd + 1) * DA_V_ROWS, :] = ones_rows
    qr_ref[...] = rotary(proj(3))
    kr_ref[...] = rotary(proj(4)) * (RET_QK_DIM ** -0.5)
    vr_ref[...] = proj(5).astype(BF16)
    gr_ref[...] = proj(6)


def _ab_in(x2, g, w, wv_t, gq, gk, gsum, cos_t, sin_t, seq):
    t = x2.shape[0]
    tm = min(TOKEN_TILE, seq)
    per_seq = seq // tm
    row = lambda width: pl.BlockSpec((tm, width), lambda i: (i, 0))
    va_spec = pl.BlockSpec((1, DA_HEADS * DA_V_ROWS, tm), lambda i: (i // per_seq, 0, i % per_seq))
    rot_spec = pl.BlockSpec((tm, RET_QK_WIDTH), lambda i: (i % per_seq, 0))
    out_shapes = (
        jax.ShapeDtypeStruct((t, DA_QK_WIDTH), BF16), jax.ShapeDtypeStruct((t, DA_QK_WIDTH), BF16),
        jax.ShapeDtypeStruct((t // seq, DA_HEADS * DA_V_ROWS, seq), BF16),
        jax.ShapeDtypeStruct((t, RET_QK_WIDTH), F32), jax.ShapeDtypeStruct((t, RET_QK_WIDTH), F32),
        jax.ShapeDtypeStruct((t, RET_WIDTH), BF16), jax.ShapeDtypeStruct((t, RET_WIDTH), F32))
    return pl.pallas_call(
        _ab_in_kernel,
        grid=(t // tm,),
        in_specs=[row(D_MODEL), _const_spec((1, D_MODEL)), _const_spec((D_MODEL, AB_IN)),
                  _const_spec((DA_WIDTH, D_MODEL)),
                  _const_spec((1, DA_QK_WIDTH)), _const_spec((1, DA_QK_WIDTH)),
                  _const_spec((DA_QK_WIDTH, DA_QK_WIDTH)), rot_spec, rot_spec],
        out_specs=(row(DA_QK_WIDTH), row(DA_QK_WIDTH), va_spec, row(RET_QK_WIDTH), row(RET_QK_WIDTH),
                   row(RET_WIDTH), row(RET_WIDTH)),
        out_shape=out_shapes,
        compiler_params=_params("parallel"),
        name="ab_in_proj",
    )(x2, g, w, wv_t, gq, gk, gsum, cos_t, sin_t)


def _diff_attn_kernel(lq1_ref, lk1_ref, lq2_ref, lk2_ref, gout_ref, q_ref, k_ref, vt_ref, o_ref,
                      sa_ref, sb_ref, sc_ref, mxa_ref, mxb_ref, mxc_ref, qza_ref, qzb_ref, m_ref, acc_ref,
                      *, blk, nblk, lam_init):
    lam = (jnp.exp(jnp.sum(lq1_ref[...] * lk1_ref[...], axis=-1, keepdims=True))
           - jnp.exp(jnp.sum(lq2_ref[...] * lk2_ref[...], axis=-1, keepdims=True)) + lam_init)
    gain = gout_ref[...] * (1.0 - lam_init)
    buf_a, buf_b, buf_c = (sa_ref, mxa_ref), (sb_ref, mxb_ref), (sc_ref, mxc_ref)

    def load_queries(qb, qz_ref):
        start = pl.multiple_of(qb * blk, blk)
        qt = q_ref[0, pl.ds(start, blk), :].astype(F32).T
        dim = lax.broadcasted_iota(jnp.int32, qt.shape, 0)
        qz_ref[...] = jnp.concatenate([jnp.where(dim < DA_QK_DIM, qt, 0.0), jnp.where(dim >= DA_QK_DIM, qt, 0.0)],
                                      axis=1).astype(BF16)

    def score(qz_ref, t, buf):
        s_ref, mx_ref = buf
        start = pl.multiple_of(t * blk, blk)
        st = _dot(k_ref[0, pl.ds(start, blk), :], qz_ref[...])
        s_ref[...] = st
        mx_ref[...] = jnp.max(st, axis=0, keepdims=True)

    def absorb(t, buf, diagonal=False):
        s_ref, mx_ref = buf
        start = pl.multiple_of(t * blk, blk)
        tile = ATTN_TILE
        for qg in range(2 * blk // tile):
            cs = slice(qg * tile, (qg + 1) * tile)

            def scores(kh):
                st = s_ref[kh * tile:(kh + 1) * tile, cs]
                if diagonal:
                    key = lax.broadcasted_iota(jnp.int32, st.shape, 0) + kh * tile
                    qry = lax.broadcasted_iota(jnp.int32, st.shape, 1) + (qg * tile) % blk
                    st = jnp.where(key // CHUNK <= qry // CHUNK, st, NEG_BIG)
                return st

            if diagonal:
                mx = functools.reduce(jnp.maximum, [jnp.max(scores(kh), axis=0, keepdims=True)
                                                    for kh in range(blk // tile)])
            else:
                mx = mx_ref[:, cs]
            m = m_ref[:, cs]
            m_new = jnp.maximum(m, mx)
            acc = jnp.exp2(m - m_new) * acc_ref[:, cs]
            for kh in range(blk // tile):
                p = jnp.exp2(scores(kh) - m_new)
                acc = acc + _dot(vt_ref[0, :, pl.ds(start + kh * tile, tile)], p.astype(BF16))
            m_ref[:, cs] = m_new
            acc_ref[:, cs] = acc

    def repeat(first, count, two_stages):
        def four_times(v, c):
            for k in range(4):
                two_stages(first + 4 * v + k)
            return c

        lax.fori_loop(0, lax.shift_right_logical(count, 2), four_times, 0)
        done = first + (count & ~3)

        @pl.when((count & 2) == 2)
        def _():
            two_stages(done)
            two_stages(done + 1)

        @pl.when((count & 1) == 1)
        def _():
            two_stages(done + (count & 2))

    def reset():
        m_ref[...] = jnp.full(m_ref.shape, NEG_BIG, F32)
        acc_ref[...] = jnp.zeros(acc_ref.shape, F32)

    def finish(qb):
        o = acc_ref[:DA_V_DIM, :] / acc_ref[DA_V_DIM:DA_V_DIM + 1, :]
        o = o[:, :blk] - lam * o[:, blk:]
        o = o * lax.rsqrt(jnp.mean(o * o, axis=0, keepdims=True) + RMS_EPS)
        o_ref[0, pl.ds(pl.multiple_of(qb * blk, blk), blk), :] = (o.T * gain).astype(BF16)
        reset()

    reset()
    load_queries(0, qza_ref)
    load_queries(1, qzb_ref)
    score(qza_ref, 0, buf_a)
    score(qzb_ref, 0, buf_b)
    absorb(0, buf_a, diagonal=True)
    finish(0)
    score(qzb_ref, 1, buf_a)
    absorb(0, buf_b)
    load_queries(min(2, nblk - 1), qza_ref)
    score(qza_ref, 0, buf_c)
    absorb(1, buf_a, diagonal=True)
    finish(1)

    def pair(g, carry):
        a = 2 * g
        load_queries(a + 1, qzb_ref)
        score(qza_ref, 1, buf_b)
        absorb(0, buf_c)
        score(qza_ref, 2, buf_a)
        absorb(1, buf_b)

        def stages_a(u):
            score(qza_ref, 2 * u + 1, buf_b)
            absorb(2 * u, buf_a)
            score(qza_ref, 2 * u + 2, buf_a)
            absorb(2 * u + 1, buf_b)

        repeat(1, g - 1, stages_a)
        score(qzb_ref, 0, buf_b)
        absorb(a, buf_a, diagonal=True)
        finish(a)
        score(qzb_ref, 1, buf_a)
        absorb(0, buf_b)

        def stages_b(u):
            score(qzb_ref, 2 * u + 2, buf_b)
            absorb(2 * u + 1, buf_a)
            score(qzb_ref, 2 * u + 3, buf_a)
            absorb(2 * u + 2, buf_b)

        repeat(0, g, stages_b)
        load_queries(jnp.minimum(a + 2, nblk - 1), qza_ref)
        score(qza_ref, 0, buf_c)
        absorb(a + 1, buf_a, diagonal=True)
        finish(a + 1)
        return carry

    lax.fori_loop(1, nblk // 2, pair, 0)


def _diff_attn(qa, ka, va_t, lq1, lk1, lq2, lk2, gout, lam_init):
    b, s, _ = qa.shape
    blk = min(ATTN_BLOCK, s // 2)
    nblk = s // blk
    assert nblk % 2 == 0 and blk % ATTN_TILE == 0, (s, blk)
    vec = _const_spec((1, DA_QK_DIM))
    assert 2 * DA_QK_DIM == DA_V_DIM
    tok_spec = pl.BlockSpec((1, s, DA_V_DIM), lambda bi, h: (bi, 0, h))
    vt_spec = pl.BlockSpec((1, DA_V_ROWS, s), lambda bi, h: (bi, h, 0))
    scores = pltpu.VMEM((blk, 2 * blk), F32)
    stats = pltpu.VMEM((1, 2 * blk), F32)
    queries = pltpu.VMEM((2 * DA_QK_DIM, 2 * blk), BF16)
    return pl.pallas_call(
        functools.partial(_diff_attn_kernel, blk=blk, nblk=nblk, lam_init=lam_init),
        grid=(b, DA_HEADS),
        in_specs=[vec, vec, vec, vec, _const_spec((1, DA_V_DIM)), tok_spec, tok_spec, vt_spec],
        out_specs=tok_spec,
        out_shape=jax.ShapeDtypeStruct((b, s, DA_WIDTH), BF16),
        scratch_shapes=[scores, scores, scores, stats, stats, stats, queries, queries, stats,
                        pltpu.VMEM((DA_V_ROWS, 2 * blk), F32)],
        compiler_params=_params("parallel", "parallel"),
        name="diff_attention",
    )(lq1, lk1, lq2, lk2, gout, qa, ka, va_t)


def _retention_kernel(dec_ref, qdec_ref, kdec_ref, sdec_ref, smask_ref, gain_ref,
                      q_ref, k_ref, v_ref, g_ref, o_ref, state_ref):
    @pl.when(pl.program_id(1) == 0)
    def _():
        state_ref[...] = jnp.zeros_like(state_ref)

    q = q_ref[0]
    k = k_ref[0]
    v = v_ref[0]
    kb = k.astype(BF16)
    lane = lax.broadcasted_iota(jnp.int32, q.shape, 1)
    state = state_ref[...]
    o_cross = _dot((q * qdec_ref[...]).astype(BF16), state.astype(BF16))
    for h in range(RET_HEADS):
        qh = jnp.where(lane // RET_QK_DIM == h, q, 0.0).astype(BF16)
        w = (_dot_nt(qh, kb) * dec_ref[h]).astype(BF16)
        vs = slice(h * RET_V_DIM, (h + 1) * RET_V_DIM)
        o = _dot(w, v[:, vs]) + o_cross[:, vs]
        o = _rms(o, gain_ref[...])
        g = g_ref[0, :, vs]
        o_ref[0, :, vs] = (o * (g * jax.nn.sigmoid(g))).astype(BF16)
    kv = _dot_tn((k * kdec_ref[...]).astype(BF16), v)
    state_ref[...] = sdec_ref[...] * state + kv * smask_ref[...]


def _retention_tables(blk):
    heads = jnp.arange(RET_HEADS, dtype=F32)
    log_gamma = jnp.log(1.0 - 2.0 ** (-5.0 - heads))
    idx = jnp.arange(blk)
    diff = (idx[:, None] - idx[None, :]).astype(F32)
    same_chunk = (idx[:, None] // CHUNK) == (idx[None, :] // CHUNK)
    visible = (idx[None, :] <= idx[:, None]) | same_chunk
    dec = jnp.where(visible[None], jnp.exp(log_gamma[:, None, None] * jnp.abs(diff)[None]), 0.0)
    per_lane = jnp.repeat(log_gamma, RET_QK_DIM)
    pos = jnp.arange(blk, dtype=F32)
    qdec = jnp.exp((pos[:, None] + 1.0) * per_lane[None, :])
    kdec = jnp.exp((blk - 1.0 - pos)[:, None] * per_lane[None, :])
    sdec = jnp.exp(blk * per_lane)[:, None]
    smask = (jnp.arange(RET_HEADS * RET_QK_DIM)[:, None] // RET_QK_DIM
             == jnp.arange(RET_WIDTH)[None, :] // RET_V_DIM).astype(F32)
    return dec, qdec, kdec, sdec, smask


def _retention(qr, kr, vr, gr, gain):
    b, s, _ = qr.shape
    blk = min(RET_BLOCK, s)
    dec, qdec, kdec, sdec, smask = _retention_tables(blk)
    tok = lambda width: pl.BlockSpec((1, blk, width), lambda bi, i: (bi, i, 0))
    return pl.pallas_call(
        _retention_kernel,
        grid=(b, s // blk),
        in_specs=[_const_spec(dec.shape), _const_spec(qdec.shape), _const_spec(kdec.shape),
                  _const_spec(sdec.shape), _const_spec(smask.shape), _const_spec((1, RET_V_DIM)),
                  tok(RET_QK_WIDTH), tok(RET_QK_WIDTH), tok(RET_WIDTH), tok(RET_WIDTH)],
        out_specs=tok(RET_WIDTH),
        out_shape=jax.ShapeDtypeStruct((b, s, RET_WIDTH), BF16),
        scratch_shapes=[pltpu.VMEM((RET_HEADS * RET_QK_DIM, RET_WIDTH), F32)],
        compiler_params=_params("parallel", "arbitrary"),
        name="retention",
    )(dec, qdec, kdec, sdec, smask, gain, qr, kr, vr, gr)


def _mlp_tail(x1, g_ref, wup_ref, wdn_ref, o_ref):
    h = _rms(x1, g_ref[...]).astype(BF16)
    acc = x1
    for c in range(MLP_HIDDEN // MLP_HIDDEN_TILE):
        cs = slice(c * MLP_HIDDEN_TILE, (c + 1) * MLP_HIDDEN_TILE)
        u = jnp.maximum(_dot(h, wup_ref[:, cs]), 0.0)
        acc = acc + _dot((u * u).astype(BF16), wdn_ref[cs, :])
    o_ref[...] = acc


def _out_mlp_kernel(x_ref, a_ref, b_ref, wout_ref, g_ref, wup_ref, wdn_ref, o_ref):
    ka = a_ref.shape[1]
    mixed = _dot(a_ref[...], wout_ref[:ka, :]) + _dot(b_ref[...], wout_ref[ka:, :])
    _mlp_tail(x_ref[...] + mixed, g_ref, wup_ref, wdn_ref, o_ref)


def _s5_out_mlp_kernel(x_ref, yt_ref, ut_ref, d_ref, wglut_ref, b_ref, wout_ref, g_ref, wup_ref, wdn_ref, o_ref):
    y = yt_ref[...] + d_ref[...] * ut_ref[...]
    z = jax.nn.gelu(y)
    a_t = (z * jax.nn.sigmoid(_dot(wglut_ref[...], z.astype(BF16)))).astype(BF16)
    mixed = _dot_tn(a_t, wout_ref[:S5_WIDTH, :]) + _dot(b_ref[...], wout_ref[S5_WIDTH:, :])
    _mlp_tail(x_ref[...] + mixed, g_ref, wup_ref, wdn_ref, o_ref)


def _mlp_specs():
    return [_const_spec((D_MODEL, D_MODEL)), _const_spec((1, D_MODEL)),
            _const_spec((D_MODEL, MLP_HIDDEN)), _const_spec((MLP_HIDDEN, D_MODEL))]


def _out_mlp(x2, a, bb, wout, g, wup, wdn):
    t = x2.shape[0]
    tm = TOKEN_TILE
    row = lambda width: pl.BlockSpec((tm, width), lambda i: (i, 0))
    return pl.pallas_call(
        _out_mlp_kernel,
        grid=(t // tm,),
        in_specs=[row(D_MODEL), row(a.shape[1]), row(bb.shape[1])] + _mlp_specs(),
        out_specs=row(D_MODEL),
        out_shape=jax.ShapeDtypeStruct((t, D_MODEL), F32),
        compiler_params=_params("parallel"),
        name="ab_out_mlp",
    )(x2, a, bb, wout, g, wup, wdn)


def _s5_out_mlp(x2, y_t, u_t, d_col, wglu_t, bb, wout, g, wup, wdn):
    t = x2.shape[0]
    tm = TOKEN_TILE
    row = lambda width: pl.BlockSpec((tm, width), lambda i: (i, 0))
    col = pl.BlockSpec((S5_WIDTH, tm), lambda i: (0, i))
    return pl.pallas_call(
        _s5_out_mlp_kernel,
        grid=(t // tm,),
        in_specs=[row(D_MODEL), col, col, _const_spec((S5_WIDTH, 1)),
                  _const_spec((S5_WIDTH, S5_WIDTH)), row(GLA_WIDTH)] + _mlp_specs(),
        out_specs=row(D_MODEL),
        out_shape=jax.ShapeDtypeStruct((t, D_MODEL), F32),
        compiler_params=_params("parallel"),
        name="cd_out_mlp",
    )(x2, y_t, u_t, d_col, wglu_t, bb, wout, g, wup, wdn)


def _cd_in_kernel(x_ref, g_ref, w_ref, wut_ref, wa_ref, ba_ref, ut_ref, q_ref, k_ref, v_ref, r_ref, la_ref):
    h = _rms(x_ref[...], g_ref[...]).astype(BF16)

    def proj(part):
        return _dot(h, w_ref[:, CD_COLS[part]:CD_COLS[part + 1]])

    ua_t = _dot_nt(wut_ref[...], h)
    ut_ref[...] = ua_t[:S5_WIDTH]
    q_ref[...] = proj(1) * (GLA_QK_DIM ** -0.5)
    k_ref[...] = proj(2)
    v_ref[...] = proj(3).astype(BF16)
    r_ref[...] = proj(4)
    pre = _dot_tn(ua_t[S5_WIDTH:].astype(BF16), wa_ref[...]) + ba_ref[...]
    log_sig = jnp.minimum(pre, 0.0) - jnp.log1p(jnp.exp(-jnp.abs(pre)))
    la_ref[...] = log_sig / GLA_TAU


def _cd_in(x2, g, w, wu_t, wa, ba):
    t = x2.shape[0]
    tm = TOKEN_TILE
    row = lambda width: pl.BlockSpec((tm, width), lambda i: (i, 0))
    out_shapes = (
        jax.ShapeDtypeStruct((S5_WIDTH, t), F32), jax.ShapeDtypeStruct((t, GLA_QK_WIDTH), F32),
        jax.ShapeDtypeStruct((t, GLA_QK_WIDTH), F32), jax.ShapeDtypeStruct((t, GLA_WIDTH), BF16),
        jax.ShapeDtypeStruct((t, GLA_WIDTH), F32), jax.ShapeDtypeStruct((t, GLA_QK_WIDTH), F32))
    return pl.pallas_call(
        _cd_in_kernel,
        grid=(t // tm,),
        in_specs=[row(D_MODEL), _const_spec((1, D_MODEL)), _const_spec((D_MODEL, CD_IN_MAIN)),
                  _const_spec((S5_WIDTH + GATE_ROWS, D_MODEL)),
                  _const_spec((GATE_ROWS, GLA_QK_WIDTH)), _const_spec((1, GLA_QK_WIDTH))],
        out_specs=(pl.BlockSpec((S5_WIDTH, tm), lambda i: (0, i)), row(GLA_QK_WIDTH), row(GLA_QK_WIDTH),
                   row(GLA_WIDTH), row(GLA_WIDTH), row(GLA_QK_WIDTH)),
        out_shape=out_shapes,
        compiler_params=_params("parallel"),
        name="cd_in_proj",
    )(x2, g, w, wu_t, wa, ba)


def _split3(x):
    hi = x.astype(BF16)
    r = x - hi.astype(F32)
    mid = r.astype(BF16)
    lo = (r - mid.astype(F32)).astype(BF16)
    return hi, mid, lo


def _gla_kernel(tri_ref, gain_ref, q_ref, k_ref, v_ref, r_ref, la_ref, o_ref, state_ref, *, blk, batch):
    @pl.when(pl.program_id(0) == 0)
    def _():
        state_ref[...] = jnp.zeros_like(state_ref)

    tri = tri_ref[...]
    hi, mid, lo = _split3(jnp.concatenate([la_ref[bi] for bi in range(batch)], axis=1))
    b_all = _dot(tri, hi) + _dot(tri, mid) + _dot(tri, lo)
    for bi in range(batch):
        _gla_block(bi, b_all[:, bi * GLA_QK_WIDTH:(bi + 1) * GLA_QK_WIDTH], gain_ref, q_ref, k_ref, v_ref, r_ref,
                   o_ref, state_ref, blk)


def _gla_block(bi, b, gain_ref, q_ref, k_ref, v_ref, r_ref, o_ref, state_ref, blk):
    nc = blk // CHUNK
    pairs = GLA_HEADS // 2
    q = q_ref[bi]
    k = k_ref[bi]
    b3 = b.reshape(nc, CHUNK, GLA_QK_WIDTH)
    b_last = b3[:, CHUNK - 1:CHUNK, :]
    e_pos = jnp.exp(b)
    e_neg = jnp.exp(-b)
    k_tail = jnp.exp(b_last - b3).reshape(blk, GLA_QK_WIDTH)
    chunk_decay = jnp.exp(b_last)
    qp = (q * e_pos).astype(BF16)
    qn = (q * e_neg).astype(BF16)
    kp = (k * e_pos).astype(BF16)
    kn = (k * e_neg).astype(BF16)
    kw = (k * k_tail).astype(BF16)

    row = lax.broadcasted_iota(jnp.int32, (blk, blk), 0)
    col = lax.broadcasted_iota(jnp.int32, (blk, blk), 1)
    same_chunk = row // CHUNK == col // CHUNK
    causal = row >= col
    lane = lax.broadcasted_iota(jnp.int32, (blk, 2 * GLA_QK_DIM), 1)
    smask = (lax.broadcasted_iota(jnp.int32, (2 * GLA_V_DIM, 2 * GLA_QK_DIM), 0) // GLA_V_DIM
             == lax.broadcasted_iota(jnp.int32, (2 * GLA_V_DIM, 2 * GLA_QK_DIM), 1) // GLA_QK_DIM)

    for p in range(pairs):
        ks = slice(p * 2 * GLA_QK_DIM, (p + 1) * 2 * GLA_QK_DIM)
        vs = slice(p * 2 * GLA_V_DIM, (p + 1) * 2 * GLA_V_DIM)
        v = v_ref[bi, :, vs]
        scores = []
        for half in range(2):
            sel = (lane // GLA_QK_DIM) == half
            fwd = _dot_nt(jnp.where(sel, qp[:, ks], 0).astype(BF16), kn[:, ks])
            bwd = _dot_nt(jnp.where(sel, qn[:, ks], 0).astype(BF16), kp[:, ks])
            scores.append(jnp.where(same_chunk, jnp.where(causal, fwd, bwd), 0.0).astype(BF16))
        vlane = lax.broadcasted_iota(jnp.int32, v.shape, 1) // GLA_V_DIM
        v_diag = jnp.concatenate([jnp.where(vlane == 0, v, 0), jnp.where(vlane == 1, v, 0)], axis=0)
        o_intra = _dot(jnp.concatenate(scores, axis=1), v_diag)

        state = state_ref[bi * pairs + p]
        cross = []
        for c in range(nc):
            rs = slice(c * CHUNK, (c + 1) * CHUNK)
            cross.append(_dot_nt(qp[rs, ks], state.astype(BF16)))
            kv_t = _dot_tn(v[rs, :], kw[rs, ks])
            state = state * chunk_decay[c][:, ks] + jnp.where(smask, kv_t, 0.0)
        state_ref[bi * pairs + p] = state
        o = o_intra + jnp.concatenate(cross, axis=0)
        for half in range(2):
            hs = slice(half * GLA_V_DIM, (half + 1) * GLA_V_DIM)
            os_ = slice(p * 2 * GLA_V_DIM + half * GLA_V_DIM, p * 2 * GLA_V_DIM + (half + 1) * GLA_V_DIM)
            g = r_ref[bi, :, os_]
            o_ref[bi, :, os_] = (_rms(o[:, hs], gain_ref[...]) * (g * jax.nn.sigmoid(g))).astype(BF16)


def _gla(qg, kg, vg, rg, la, gain):
    b, s, _ = qg.shape
    blk = min(GLA_BLOCK, s)
    idx = jnp.arange(blk)
    tri = (((idx[:, None] // CHUNK) == (idx[None, :] // CHUNK)) & (idx[None, :] <= idx[:, None])).astype(BF16)
    tok = lambda width: pl.BlockSpec((b, blk, width), lambda i: (0, i, 0))
    return pl.pallas_call(
        functools.partial(_gla_kernel, blk=blk, batch=b),
        grid=(s // blk,),
        in_specs=[_const_spec((blk, blk)), _const_spec((1, GLA_V_DIM)),
                  tok(GLA_QK_WIDTH), tok(GLA_QK_WIDTH), tok(GLA_WIDTH), tok(GLA_WIDTH), tok(GLA_QK_WIDTH)],
        out_specs=tok(GLA_WIDTH),
        out_shape=jax.ShapeDtypeStruct((b, s, GLA_WIDTH), BF16),
        scratch_shapes=[pltpu.VMEM((b * (GLA_HEADS // 2), 2 * GLA_V_DIM, 2 * GLA_QK_DIM), F32)],
        compiler_params=_params("arbitrary"),
        name="gla",
    )(tri, gain, qg, kg, vg, rg, la)


def _s5_kernel(u_ref, kern_ref, f_ref, e_ref, lre_ref, lim_ref, y_ref,
               toep_ref, vre_ref, vim_ref, hre_ref, him_ref, *, batch, chunks):
    length = S5_CHUNK
    src = lax.broadcasted_iota(jnp.int32, (length, length), 0)
    dst = lax.broadcasted_iota(jnp.int32, (length, length), 1)

    def build(ci, carry):
        for co in range(S5_GROUP):
            lags = kern_ref[0, pl.ds(ci * S5_GROUP + co, 1), :]
            blk = pltpu.roll(jnp.broadcast_to(lags, (length, length)), 0, 1, stride=1, stride_axis=0)
            blk = jnp.where(dst >= src, blk, 0.0)
            toep_ref[pl.ds(pl.multiple_of(ci * length, length), length), co * length:(co + 1) * length] = (
                blk.astype(BF16))
        return carry

    lax.fori_loop(0, S5_GROUP, build, 0)

    u = jnp.concatenate([u_ref[0, c] for c in range(S5_GROUP)], axis=1).astype(BF16)
    v = _dot(u, f_ref[0])
    vre_ref[...] = v[:, :LANES_V7X]
    vim_ref[...] = v[:, LANES_V7X:]
    lre = lre_ref[0]
    lim = lim_ref[0]

    def step(n, carry):
        new = []
        for bi in range(batch):
            hr, hi = carry[2 * bi], carry[2 * bi + 1]
            r = bi * chunks + n
            hre_ref[pl.ds(r, 1), :] = hr
            him_ref[pl.ds(r, 1), :] = hi
            vr = vre_ref[pl.ds(r, 1), :]
            vi = vim_ref[pl.ds(r, 1), :]
            new += [lre * hr - lim * hi + vr, lre * hi + lim * hr + vi]
        return tuple(new)

    zero = jnp.zeros((1, LANES_V7X), F32)
    lax.fori_loop(0, chunks, step, (zero,) * (2 * batch))
    h = jnp.concatenate([hre_ref[...], him_ref[...]], axis=1).astype(BF16)
    y = _dot(u, toep_ref[...]) + _dot(h, e_ref[0])
    for c in range(S5_GROUP):
        y_ref[0, c] = y[:, c * length:(c + 1) * length]


def _s5_tables(a_re, a_im, log_step, b_re, b_im, c_re, c_im):
    hp = lax.Precision.HIGHEST
    g, p, c, length = S5_GROUPS, S5_STATE, S5_GROUP, S5_CHUNK
    a_re, a_im = a_re.astype(F32), a_im.astype(F32)
    delta = jnp.exp(log_step.astype(F32))[:, None]
    tau = jnp.arange(length + 1, dtype=F32)[None, :, None]
    mag = jnp.exp((a_re * delta)[:, None, :] * tau)
    ang = (a_im * delta)[:, None, :] * tau
    pw_re, pw_im = mag * jnp.cos(ang), mag * jnp.sin(ang)
    n_re, n_im = pw_re[:, 1, :] - 1.0, pw_im[:, 1, :]
    den = a_re * a_re + a_im * a_im
    q_re, q_im = (n_re * a_re + n_im * a_im) / den, (n_im * a_re - n_re * a_im) / den
    bb_re = q_re[:, :, None] * b_re - q_im[:, :, None] * b_im
    bb_im = q_re[:, :, None] * b_im + q_im[:, :, None] * b_re
    first_re, first_im = pw_re[:, :length, None, :], pw_im[:, :length, None, :]
    cp_re = c_re[:, None] * first_re - c_im[:, None] * first_im
    cp_im = c_re[:, None] * first_im + c_im[:, None] * first_re
    kern = (jnp.einsum('gtcp,gpd->gdct', cp_re, bb_re, precision=hp)
            - jnp.einsum('gtcp,gpd->gdct', cp_im, bb_im, precision=hp)).reshape(g, c * c, length)
    rev_re, rev_im = pw_re[:, length - 1::-1, :][:, None], pw_im[:, length - 1::-1, :][:, None]
    bt_re, bt_im = bb_re.transpose(0, 2, 1)[:, :, None, :], bb_im.transpose(0, 2, 1)[:, :, None, :]
    f_re = (rev_re * bt_re - rev_im * bt_im).reshape(g, c * length, p)
    f_im = (rev_re * bt_im + rev_im * bt_re).reshape(g, c * length, p)
    nx_re = pw_re[:, 1:, :].transpose(0, 2, 1)[:, :, None, :]
    nx_im = pw_im[:, 1:, :].transpose(0, 2, 1)[:, :, None, :]
    ct_re, ct_im = c_re.transpose(0, 2, 1)[:, :, :, None], c_im.transpose(0, 2, 1)[:, :, :, None]
    e_re = (ct_re * nx_re - ct_im * nx_im).reshape(g, p, c * length)
    e_im = (ct_re * nx_im + ct_im * nx_re).reshape(g, p, c * length)
    pad = lambda z: jnp.pad(z, ((0, 0), (0, 0), (0, LANES_V7X - p)))
    pad_rows = lambda z: jnp.pad(z, ((0, 0), (0, LANES_V7X - p), (0, 0)))
    f_both = jnp.concatenate([pad(f_re), pad(f_im)], axis=2).astype(BF16)
    e_both = jnp.concatenate([pad_rows(e_re), pad_rows(-e_im)], axis=1).astype(BF16)
    return kern, f_both, e_both, pad(pw_re[:, length:, :]), pad(pw_im[:, length:, :])


def _s5(u_t, tables, batch):
    t = u_t.shape[1]
    length = S5_CHUNK
    rows = t // length
    width = length * S5_GROUP
    kern, f_both, e_both, lre, lim = tables
    grp = lambda r, c: pl.BlockSpec((1, r, c), lambda gi: (gi, 0, 0))
    io_spec = pl.BlockSpec((1, S5_GROUP, rows, length), lambda gi: (gi, 0, 0, 0))
    y = pl.pallas_call(
        functools.partial(_s5_kernel, batch=batch, chunks=rows // batch),
        grid=(S5_GROUPS,),
        in_specs=[io_spec, grp(S5_GROUP * S5_GROUP, length), grp(width, 2 * LANES_V7X),
                  grp(2 * LANES_V7X, width), grp(1, LANES_V7X), grp(1, LANES_V7X)],
        out_specs=io_spec,
        out_shape=jax.ShapeDtypeStruct((S5_GROUPS, S5_GROUP, rows, length), F32),
        scratch_shapes=[pltpu.VMEM((width, width), BF16)] + [pltpu.VMEM((rows, LANES_V7X), F32)] * 4,
        compiler_params=_params("parallel"),
        name="s5",
    )(u_t.reshape(S5_GROUPS, S5_GROUP, rows, length), kern, f_both, e_both, lre, lim)
    return y.reshape(S5_WIDTH, t)


def _rotary_tables(seq):
    half = RET_QK_DIM // 2
    inv_freq = 1.0 / (ROPE_BASE ** jnp.linspace(0.0, 1.0, half, dtype=F32))
    ang = jnp.arange(seq, dtype=F32)[:, None] * inv_freq[None, :]
    cos, sin = jnp.cos(ang), jnp.sin(ang)
    cos_t = jnp.tile(jnp.concatenate([cos, cos], axis=1), (1, RET_HEADS))
    sin_t = jnp.tile(jnp.concatenate([-sin, sin], axis=1), (1, RET_HEADS))
    return cos_t, sin_t


def _row(v):
    return v.reshape(1, -1).astype(F32)


def kernel(x, norm_mix_g, norm_mlp_g, w_up, w_down, ab_w_in, ab_w_out, da_q_norm, da_k_norm,
           da_lam_q1, da_lam_k1, da_lam_q2, da_lam_k2, da_out_norm, ret_out_norm, cd_w_in, cd_w_out,
           s5_a_re, s5_a_im, s5_log_step, s5_b_re, s5_b_im, s5_c_re, s5_c_im, s5_d, s5_w_glu,
           gla_w_a2, gla_b_a2, gla_out_norm):
    bsz, seq, _ = x.shape
    t = bsz * seq
    x2 = x.reshape(t, D_MODEL)

    cos_t, sin_t = _rotary_tables(seq)
    lane_group = jnp.arange(DA_QK_WIDTH) // DA_QK_DIM
    gsum = ((lane_group[:, None] == lane_group[None, :]).astype(F32) / DA_QK_DIM).astype(BF16)
    gq = _row(jnp.tile(da_q_norm[0], 2 * DA_HEADS)) * (DA_QK_DIM ** -0.5 * math.log2(math.e))
    gk = _row(jnp.tile(da_k_norm[0], 2 * DA_HEADS))
    w_ab = ab_w_in[0].astype(BF16)
    qa, ka, va_t, qr, kr, vr, gr = _ab_in(x2, _row(norm_mix_g[0]), w_ab, w_ab[:, AB_COLS[2]:AB_COLS[3]].T, gq, gk,
                                          gsum, cos_t, sin_t, seq)
    lam_init = 0.8 - 0.6 * math.exp(-0.3 * 0)
    seq3 = lambda a: a.reshape(bsz, seq, a.shape[-1])
    o_a = _diff_attn(seq3(qa), seq3(ka), va_t, _row(da_lam_q1[0]), _row(da_lam_k1[0]),
                     _row(da_lam_q2[0]), _row(da_lam_k2[0]), _row(da_out_norm[0]), lam_init)
    o_r = _retention(seq3(qr), seq3(kr), seq3(vr), seq3(gr), _row(ret_out_norm[0]))
    x2 = _out_mlp(x2, o_a.reshape(t, DA_WIDTH), o_r.reshape(t, RET_WIDTH), ab_w_out[0].astype(BF16),
                  _row(norm_mlp_g[0]), w_up[0].astype(BF16), w_down[0].astype(BF16))

    w_cd = cd_w_in[0].astype(BF16)
    gate_pad = GATE_ROWS - GLA_GATE_RANK
    wua_t = jnp.pad(jnp.concatenate([w_cd[:, :S5_WIDTH], w_cd[:, CD_IN_MAIN:]], axis=1).T, ((0, gate_pad), (0, 0)))
    wa = jnp.pad(gla_w_a2[0], ((0, gate_pad), (0, 0))).astype(BF16)
    u_t, qg, kg, vg, rg, la = _cd_in(x2, _row(norm_mix_g[1]), w_cd[:, :CD_IN_MAIN], wua_t, wa, _row(gla_b_a2[0]))
    o_d = _gla(seq3(qg), seq3(kg), seq3(vg), seq3(rg), seq3(la), _row(gla_out_norm[0]))
    tables = _s5_tables(s5_a_re[0], s5_a_im[0], s5_log_step[0], s5_b_re[0], s5_b_im[0], s5_c_re[0], s5_c_im[0])
    y_t = _s5(u_t, tables, bsz)
    x2 = _s5_out_mlp(x2, y_t, u_t, s5_d[0].reshape(S5_WIDTH, 1).astype(F32), s5_w_glu[0].T.astype(BF16),
                     o_d.reshape(t, GLA_WIDTH), cd_w_out[0].astype(BF16), _row(norm_mlp_g[1]),
                     w_up[1].astype(BF16), w_down[1].astype(BF16))
    return x2.reshape(bsz, seq, D_MODEL)
```
